```python
import math
import jax
import jax.numpy as jnp
from jax import lax
import numpy as np

D_MODEL = 2048
BATCH = 8
SEQ = 2048
DEPTH = 4
DEC_BATCH = 8
DEC_SEQ = 16
PAST_LEN = 4096

CHUNK = 64
DK = 128
DV = 128
NK = D_MODEL // DK
NV = 2 * NK
KEY_DIM = NK * DK
VAL_DIM = NV * DV
QKV_DIM = 2 * KEY_DIM + VAL_DIM
IN_DIM = QKV_DIM + VAL_DIM + 2 * NV
CONV_W = 4
POOL_WINDOWS = (2, 4, 8, 16)
N_GROUPS = 4
GC = D_MODEL // N_GROUPS
POOL_BUF = max(POOL_WINDOWS) - 1
FFN_HIDDEN = -(-8 * D_MODEL // 768) * 256
N_A = (DEPTH + 1) // 2
N_B = DEPTH // 2
EPS = 1e-6

kernel_name = 'hybrid_gdn_pool_stream_step'


def _rms_norm(x, w):
    xf = x.astype(jnp.float32)
    y = xf * lax.rsqrt(jnp.mean(xf * xf, axis=-1, keepdims=True) + EPS)
    return (y * w.astype(jnp.float32)).astype(x.dtype)


def _l2norm(x):
    return x * lax.rsqrt(jnp.sum(x * x, axis=-1, keepdims=True) + EPS)


def _to_chunks(a, n, c):
    b, _, h = a.shape[:3]
    a = a.reshape((b, n, c, h) + a.shape[3:])
    return jnp.moveaxis(a, 3, 1)


def _gated_delta_rule(q, k, v, g, beta, S0):
    B, T, H, _ = q.shape
    C = min(CHUNK, T)
    pad = (-T) % C
    if pad:
        p4 = ((0, 0), (0, pad), (0, 0), (0, 0))
        q, k, v = jnp.pad(q, p4), jnp.pad(k, p4), jnp.pad(v, p4)
        g, beta = jnp.pad(g, p4[:3]), jnp.pad(beta, p4[:3])
    N = (T + pad) // C
    q, k, v = _to_chunks(q, N, C), _to_chunks(k, N, C), _to_chunks(v, N, C)
    g, beta = _to_chunks(g, N, C), _to_chunks(beta, N, C)
    gc = jnp.cumsum(g, axis=-1)
    causal = jnp.tril(jnp.ones((C, C), dtype=bool))
    strict = jnp.tril(jnp.ones((C, C), dtype=bool), -1)
    diff = gc[..., :, None] - gc[..., None, :]
    decay = jnp.where(causal, jnp.exp(jnp.where(causal, diff, 0.0)), 0.0)
    kb = k * beta[..., None]
    L = jnp.where(strict, jnp.einsum('bhncd,bhnsd->bhncs', kb, k) * decay, 0.0)
    eye = jnp.eye(C, dtype=jnp.float32)
    Tm = lax.linalg.triangular_solve(eye + L, jnp.broadcast_to(eye, L.shape), left_side=True, lower=True)
    w = jnp.einsum('bhncs,bhnsd->bhncd', Tm, kb * jnp.exp(gc)[..., None])
    u = jnp.einsum('bhncs,bhnsd->bhncd', Tm, v * beta[..., None])
    A = jnp.where(causal, jnp.einsum('bhncd,bhnsd->bhncs', q, k) * decay, 0.0)
    qg = q * jnp.exp(gc)[..., None]
    kd = k * jnp.exp(gc[..., -1:] - gc)[..., None]
    gl = jnp.exp(gc[..., -1])

    def step(S, xs):
        w_i, u_i, A_i, qg_i, kd_i, gl_i = xs
        v_new = u_i - jnp.einsum('bhck,bhkv->bhcv', w_i, S)
        o_i = jnp.einsum('bhck,bhkv->bhcv', qg_i, S) + jnp.einsum('bhcs,bhsv->bhcv', A_i, v_new)
        S = S * gl_i[..., None, None] + jnp.einsum('bhck,bhcv->bhkv', kd_i, v_new)
        return S, o_i

    xs = tuple(jnp.moveaxis(t, 2, 0) for t in (w, u, A, qg, kd, gl))
    S, o = lax.scan(step, S0, xs)
    o = jnp.moveaxis(o, 0, 2)
    o = jnp.moveaxis(o, 1, 3).reshape(B, N * C, H, -1)[:, :T]
    return o, S


def _gdn_mixer(h, conv_hist, S0, w_in, conv_w, A_log, dt_bias, norm_w, w_out):
    B, T, _ = h.shape
    proj = h @ w_in
    qkv = proj[..., :QKV_DIM]
    z = proj[..., QKV_DIM:QKV_DIM + VAL_DIM]
    b_logit = proj[..., QKV_DIM + VAL_DIM:QKV_DIM + VAL_DIM + NV]
    a_logit = proj[..., QKV_DIM + VAL_DIM + NV:]
    xc = jnp.concatenate([conv_hist.astype(qkv.dtype), qkv], axis=1)
    acc = xc[:, 0:T] * conv_w[0]
    for j in range(1, CONV_W):
        acc = acc + xc[:, j:j + T] * conv_w[j]
    qkv_c = jax.nn.silu(acc).astype(jnp.float32)
    q = _l2norm(qkv_c[..., :KEY_DIM].reshape(B, T, NK, DK)) * (DK ** -0.5)
    k = _l2norm(qkv_c[..., KEY_DIM:2 * KEY_DIM].reshape(B, T, NK, DK))
    v = qkv_c[..., 2 * KEY_DIM:].reshape(B, T, NV, DV)
    q = jnp.repeat(q, NV // NK, axis=2)
    k = jnp.repeat(k, NV // NK, axis=2)
    beta = jax.nn.sigmoid(b_logit.astype(jnp.float32))
    g = -jnp.exp(A_log.astype(jnp.float32)) * jax.nn.softplus(a_logit.astype(jnp.float32) + dt_bias.astype(jnp.float32))
    o, S = _gated_delta_rule(q, k, v, g, beta, S0.astype(jnp.float32))
    zf = z.reshape(B, T, NV, DV).astype(jnp.float32)
    o = o * lax.rsqrt(jnp.mean(o * o, axis=-1, keepdims=True) + EPS) * norm_w.astype(jnp.float32) * jax.nn.silu(zf)
    out = o.reshape(B, T, VAL_DIM).astype(h.dtype) @ w_out
    return out, xc[:, -(CONV_W - 1):], S


def _pool_mixer(h, hist, w, scale):
    B, T, D = h.shape
    P = hist.shape[1]
    xc = jnp.concatenate([hist.astype(h.dtype), h], axis=1)
    xf = xc.astype(jnp.float32)
    cs = jnp.concatenate([jnp.zeros((B, 1, D), jnp.float32), jnp.cumsum(xf, axis=1)], axis=1)
    end = np.arange(T) + P + 1
    groups = []
    for gi, wlen in enumerate(POOL_WINDOWS):
        sl = slice(gi * GC, (gi + 1) * GC)
        lo_idx = np.maximum(end - wlen, 0)
        cnt = jnp.asarray((end - lo_idx)[:, None], jnp.float32)
        win_sum = cs[:, P + 1:, sl] - cs[:, lo_idx, sl]
        groups.append(win_sum / cnt - xf[:, P:, sl])
    d = jnp.stack(groups, axis=2).astype(h.dtype)
    y = jnp.einsum('btgc,gce->btge', d, w).reshape(B, T, D) * scale
    return y, xc[:, -POOL_BUF:]


def _swiglu(h, w_gu, w_down):
    gu = h @ w_gu
    return (jax.nn.silu(gu[..., :FFN_HIDDEN]) * gu[..., FFN_HIDDEN:]) @ w_down


def _trunk(x, conv_hist, S_hist, pool_hist, norm_mix_w, norm_ffn_w, final_norm_w, gdn_w_in, gdn_conv_w,
           gdn_A_log, gdn_dt_bias, gdn_norm_w, gdn_w_out, pool_w, pool_scale, ffn_w_gu, ffn_w_down):
    new_conv, new_S, new_pool = [], [], []
    for i in range(DEPTH):
        j = i // 2
        h = _rms_norm(x, norm_mix_w[i])
        if i % 2 == 0:
            out, c, S = _gdn_mixer(h, conv_hist[j], S_hist[j], gdn_w_in[j], gdn_conv_w[j], gdn_A_log[j],
                                   gdn_dt_bias[j], gdn_norm_w[j], gdn_w_out[j])
            new_conv.append(c)
            new_S.append(S)
        else:
            out, p = _pool_mixer(h, pool_hist[j], pool_w[j], pool_scale[j])
            new_pool.append(p)
        x = x + out.astype(x.dtype)
        x = x + _swiglu(_rms_norm(x, norm_ffn_w[i]), ffn_w_gu[i], ffn_w_down[i]).astype(x.dtype)
    y = _rms_norm(x, final_norm_w)
    return y, jnp.stack(new_conv), jnp.stack(new_S), jnp.stack(new_pool)


def setup_inputs(seed: int = 0) -> dict:
    key = jax.random.key(seed)
    ks = jax.random.split(key, 20)
    f32 = jnp.float32

    def nrm(k, shape, s):
        return jax.random.normal(k, shape, f32) * s

    n_pool_hist = min(POOL_BUF, PAST_LEN)
    A = jax.random.uniform(ks[8], (N_A, NV), f32, 1.0, 16.0)
    dt = jnp.exp(jax.random.uniform(ks[9], (N_A, NV), f32, math.log(1e-3), math.log(1e-1)))
    dt_bias = dt + jnp.log(-jnp.expm1(-dt))
    return {
        'x_prompt': nrm(ks[0], (BATCH, SEQ, D_MODEL), 1.0),
        'x_sample': nrm(ks[1], (DEC_BATCH, DEC_SEQ, D_MODEL), 1.0),
        'state_gdn_conv': nrm(ks[2], (N_A, DEC_BATCH, CONV_W - 1, QKV_DIM), 1.0),
        'state_gdn_S': nrm(ks[3], (N_A, DEC_BATCH, NV, DK, DV), 0.1),
        'state_pool': nrm(ks[4], (N_B, DEC_BATCH, n_pool_hist, D_MODEL), 1.0),
        'norm_mix_w': 1.0 + nrm(ks[5], (DEPTH, D_MODEL), 0.02),
        'norm_ffn_w': 1.0 + nrm(ks[6], (DEPTH, D_MODEL), 0.02),
        'final_norm_w': 1.0 + nrm(ks[7], (D_MODEL,), 0.02),
        'gdn_w_in': nrm(ks[10], (N_A, D_MODEL, IN_DIM), D_MODEL ** -0.5),
        'gdn_conv_w': nrm(ks[11], (N_A, CONV_W, QKV_DIM), CONV_W ** -0.5),
        'gdn_A_log': jnp.log(A),
        'gdn_dt_bias': dt_bias,
        'gdn_norm_w': 1.0 + nrm(ks[12], (N_A, DV), 0.02),
        'gdn_w_out': nrm(ks[13], (N_A, VAL_DIM, D_MODEL), VAL_DIM ** -0.5),
        'pool_w': nrm(ks[14], (N_B, N_GROUPS, GC, GC), GC ** -0.5),
        'pool_scale': 1.0 + nrm(ks[15], (N_B, D_MODEL), 0.02),
        'ffn_w_gu': nrm(ks[16], (DEPTH, D_MODEL, 2 * FFN_HIDDEN), D_MODEL ** -0.5),
        'ffn_w_down': nrm(ks[17], (DEPTH, FFN_HIDDEN, D_MODEL), FFN_HIDDEN ** -0.5),
    }


def reference(x_prompt, x_sample, state_gdn_conv, state_gdn_S, state_pool, norm_mix_w, norm_ffn_w,
              final_norm_w, gdn_w_in, gdn_conv_w, gdn_A_log, gdn_dt_bias, gdn_norm_w, gdn_w_out, pool_w,
              pool_scale, ffn_w_gu, ffn_w_down):
    conv0 = jnp.zeros((N_A, BATCH, CONV_W - 1, QKV_DIM), x_prompt.dtype)
    S00 = jnp.zeros((N_A, BATCH, NV, DK, DV), jnp.float32)
    pool0 = jnp.zeros((N_B, BATCH, 0, D_MODEL), x_prompt.dtype)
    y_prompt, conv_p, S_p, pool_p = _trunk(
        x_prompt, conv0, S00, pool0, norm_mix_w, norm_ffn_w, final_norm_w, gdn_w_in, gdn_conv_w,
        gdn_A_log, gdn_dt_bias, gdn_norm_w, gdn_w_out, pool_w, pool_scale, ffn_w_gu, ffn_w_down)
    y_sample, conv_s, S_s, pool_s = _trunk(
        x_sample, state_gdn_conv, state_gdn_S, state_pool, norm_mix_w, norm_ffn_w, final_norm_w, gdn_w_in,
        gdn_conv_w, gdn_A_log, gdn_dt_bias, gdn_norm_w, gdn_w_out, pool_w, pool_scale, ffn_w_gu, ffn_w_down)
    S_p = S_p.astype(state_gdn_S.dtype)
    S_s = S_s.astype(state_gdn_S.dtype)
    return (y_prompt, y_sample, conv_p, S_p, pool_p, conv_s, S_s, pool_s)
```

```python
import functools

import jax
import jax.numpy as jnp
from jax import lax
from jax.experimental import pallas as pl
from jax.experimental.pallas import tpu as pltpu

F32 = jnp.float32
BF16 = jnp.bfloat16

EPS = 1e-6
LANES = 128
DK = 128
DV = 128
CONV_W = 4
HIST_ROWS = 8
CHUNK = 64
INV_BLOCK = 16
POOL_WINDOWS = (2, 4, 8, 16)
POOL_PAD = 16
GATE_SLOTS = 4
VMEM_LIMIT = 56 * 1024 * 1024


def _cparams(sem):
    return pltpu.CompilerParams(dimension_semantics=sem, vmem_limit_bytes=VMEM_LIMIT)


def _rms(x, w):
    ms = jnp.mean(x * x, axis=-1, keepdims=True)
    return x * lax.rsqrt(ms + EPS) * w


def _mm(a, b):
    return jnp.dot(a.astype(BF16), b.astype(BF16), preferred_element_type=F32)


def _norm_proj_kernel(x_ref, nw_ref, w_ref, wg_ref, o_ref, g_ref, h_ref, *, tn):
    @pl.when(pl.program_id(1) == 0)
    def _():
        h = _rms(x_ref[...], nw_ref[...]).astype(BF16)
        h_ref[...] = h
        g_ref[...] = jnp.dot(h, wg_ref[...], preferred_element_type=F32)

    acc = jnp.dot(h_ref[...], w_ref[...], preferred_element_type=F32)
    for c in range(tn // LANES):
        o_ref[c] = acc[:, c * LANES:(c + 1) * LANES]


def _norm_proj(x, nw, w, wg, *, tm, tn):
    m, d = x.shape
    n = w.shape[1]
    return pl.pallas_call(
        functools.partial(_norm_proj_kernel, tn=tn),
        grid=(m // tm, n // tn),
        in_specs=[
            pl.BlockSpec((tm, d), lambda i, j: (i, 0)),
            pl.BlockSpec((1, d), lambda i, j: (0, 0)),
            pl.BlockSpec((d, tn), lambda i, j: (0, j)),
            pl.BlockSpec((d, LANES), lambda i, j: (0, 0)),
        ],
        out_specs=[
            pl.BlockSpec((tn // LANES, tm, LANES), lambda i, j: (j, i, 0)),
            pl.BlockSpec((tm, LANES), lambda i, j: (i, 0)),
        ],
        out_shape=[
            jax.ShapeDtypeStruct((n // LANES, m, LANES), F32),
            jax.ShapeDtypeStruct((m, LANES), F32),
        ],
        scratch_shapes=[pltpu.VMEM((tm, d), BF16)],
        compiler_params=_cparams(("parallel", "arbitrary")),
        name="norm_proj",
    )(x, nw, w, wg)


def _gates_kernel(ba_ref, alog_ref, dtb_ref, tri_ref, o_ref, *, c):
    x = ba_ref[...]
    beta = jax.nn.sigmoid(x)
    y = x + dtb_ref[...]
    softplus = jnp.maximum(y, 0.0) + jnp.log1p(jnp.exp(-jnp.abs(y)))
    g = -jnp.exp(alog_ref[...]) * softplus
    lane = lax.broadcasted_iota(jnp.int32, (c, x.shape[1]), 1)
    is_beta = (lane & (GATE_SLOTS - 1)) == 0
    tri = tri_ref[...]
    for n in range(x.shape[0] // c):
        sl = slice(n * c, (n + 1) * c)
        gc = jnp.dot(tri, g[sl], preferred_element_type=F32, precision=lax.Precision.HIGHEST)
        o_ref[sl, :] = jnp.where(is_beta, beta[sl], gc)


def _gates(ba, alog, dtb, *, c, tt):
    m = ba.shape[0]
    tri = jnp.tril(jnp.ones((c, c), F32))
    return pl.pallas_call(
        functools.partial(_gates_kernel, c=c),
        grid=(m // tt,),
        in_specs=[
            pl.BlockSpec((tt, LANES), lambda i: (i, 0)),
            pl.BlockSpec((1, LANES), lambda i: (0, 0)),
            pl.BlockSpec((1, LANES), lambda i: (0, 0)),
            pl.BlockSpec((c, c), lambda i: (0, 0)),
        ],
        out_specs=pl.BlockSpec((tt, LANES), lambda i: (i, 0)),
        out_shape=jax.ShapeDtypeStruct((m, LANES), F32),
        compiler_params=_cparams(("parallel",)),
        name="gates",
    )(ba, alog, dtb, tri)


def _inv_unit_lower(l_mat):
    c = l_mat.shape[0]
    r = lax.broadcasted_iota(jnp.int32, (c, c), 0)
    s = lax.broadcasted_iota(jnp.int32, (c, c), 1)
    shift = INV_BLOCK.bit_length() - 1
    same = lax.shift_right_logical(r, shift) == lax.shift_right_logical(s, shift)
    eye = jnp.where(r == s, 1.0, 0.0).astype(F32)
    d1 = jnp.where(same, l_mat, 0.0)
    d2 = _mm(d1, d1)
    t = _mm(jnp.concatenate([d2, d1], axis=0), d2)
    d4 = t[:c]
    y = eye - d1 + d2 - t[c:]
    t = _mm(jnp.concatenate([d4, y], axis=0), d4)
    y = y + t[c:]
    x0 = y + _mm(y, t[:c])
    nblk = c // INV_BLOCK
    if nblk == 1:
        return x0
    e = jnp.where(same, 0.0, l_mat)
    nmat = _mm(x0, e)
    z = eye - nmat
    if nblk > 2:
        assert nblk == 4
        n2 = _mm(nmat, nmat)
        n3 = _mm(nmat, n2)
        z = z + n2 - n3
    return _mm(z, x0)


def _gdn_kernel(q_ref, k_ref, v_ref, z_ref, hq_ref, hk_ref, hv_ref, cq_ref, ck_ref, cv_ref, g_ref, s0_ref,
                nw_ref, o_ref, s_ref, xc_s, q_s, k_s, v_s, g_s, w_s, u_s, qg_s, kd_s, a_s, o_s, st_s, *, t, c):
    n_chunks = t // c
    hv = pl.program_id(1)

    def conv_silu(x_ref, h_ref, cw_ref):
        xc_s[0:HIST_ROWS, :] = h_ref[0, 0]
        xc_s[HIST_ROWS:HIST_ROWS + t, :] = x_ref[0]
        cw = cw_ref[0]
        base = HIST_ROWS - (CONV_W - 1)
        acc = xc_s[base:base + t, :] * cw[0:1, :]
        for j in range(1, CONV_W):
            acc = acc + xc_s[base + j:base + j + t, :] * cw[j:j + 1, :]
        return acc * jax.nn.sigmoid(acc)

    def l2n(x):
        return x * lax.rsqrt(jnp.sum(x * x, axis=-1, keepdims=True) + EPS)

    q_s[...] = l2n(conv_silu(q_ref, hq_ref, cq_ref)) * (DK ** -0.5)
    k_s[...] = l2n(conv_silu(k_ref, hk_ref, ck_ref))
    v_s[...] = conv_silu(v_ref, hv_ref, cv_ref)
    g_s[...] = pltpu.roll(g_ref[...], (LANES - GATE_SLOTS * hv) % LANES, 1)
    st_s[...] = s0_ref[0, 0]

    r = lax.broadcasted_iota(jnp.int32, (c, c), 0)
    s = lax.broadcasted_iota(jnp.int32, (c, c), 1)
    causal = r >= s
    strict = r > s
    diag = r == s

    def intra(n, carry):
        row0 = pl.multiple_of(n * c, c)
        sl = pl.ds(row0, c)
        qc, kc, vc, gg = q_s[sl, :], k_s[sl, :], v_s[sl, :], g_s[sl, :]
        beta = gg[:, 0:1]
        gc = gg[:, 1:2]
        gc_row = jnp.sum(jnp.where(diag, gc, 0.0), axis=0, keepdims=True)
        decay = jnp.where(causal, jnp.exp(jnp.where(causal, gc - gc_row, 0.0)), 0.0)
        k16 = kc.astype(BF16)
        qkk = lax.dot_general(jnp.concatenate([qc.astype(BF16), k16], axis=0), k16,
                              (((1,), (1,)), ((), ())), preferred_element_type=F32)
        a_mat = qkk[:c] * decay
        l_mat = jnp.where(strict, qkk[c:] * decay * beta, 0.0)
        tm = _inv_unit_lower(l_mat)
        eg = jnp.exp(gc)
        wu = _mm(tm, jnp.concatenate([kc * (beta * eg), vc * beta], axis=1))
        w_s[sl, :] = wu[:, :DK]
        u_s[sl, :] = wu[:, DK:]
        qg_s[sl, :] = qc * eg
        kd_s[sl, :] = kc * jnp.exp(gc[c - 1:c, :] - gc)
        a_s[n] = a_mat
        return carry

    def scan(n, carry):
        row0 = pl.multiple_of(n * c, c)
        sl = pl.ds(row0, c)
        st = st_s[...]
        wq = _mm(jnp.concatenate([w_s[sl, :], qg_s[sl, :]], axis=0), st)
        v_new = (u_s[sl, :] - wq[:c]).astype(BF16)
        o_s[sl, :] = wq[c:] + _mm(a_s[n], v_new)
        gl = jnp.exp(g_s[pl.ds(row0 + c - 1, 1), 1:2])
        st_s[...] = st * gl + lax.dot_general(kd_s[sl, :].astype(BF16), v_new, (((0,), (0,)), ((), ())),
                                              preferred_element_type=F32)
        return carry

    if n_chunks == 1:
        intra(0, 0)
        scan(0, 0)
    else:
        lax.fori_loop(0, n_chunks, intra, 0)
        lax.fori_loop(0, n_chunks, scan, 0)

    s_ref[0, 0] = st_s[...]
    o = o_s[...]
    zf = z_ref[0]
    o = o * lax.rsqrt(jnp.mean(o * o, axis=-1, keepdims=True) + EPS) * nw_ref[...] * (zf * jax.nn.sigmoid(zf))
    o_ref[...] = o.astype(o_ref.dtype)


def _gdn_core(ph, hist, cw, gates, s0, nw, *, b, t, nk, nv):
    c = min(CHUNK, t)
    assert t % c == 0 and c % INV_BLOCK == 0
    rep = nv // nk
    m = b * t
    n_chunks = t // c
    row = lambda off: (lambda bi, h: (off + h, bi, 0))
    rowk = lambda off: (lambda bi, h: (off + h // rep, bi, 0))
    hrow = lambda off: (lambda bi, h: (off + h, bi, 0, 0))
    hrowk = lambda off: (lambda bi, h: (off + h // rep, bi, 0, 0))
    crow = lambda off: (lambda bi, h: (off + h, 0, 0))
    crowk = lambda off: (lambda bi, h: (off + h // rep, 0, 0))
    blk = (1, t, LANES)
    hblk = (1, 1, HIST_ROWS, LANES)
    cblk = (1, HIST_ROWS, LANES)
    big = pltpu.VMEM((t, LANES), F32)
    return pl.pallas_call(
        functools.partial(_gdn_kernel, t=t, c=c),
        grid=(b, nv),
        in_specs=[
            pl.BlockSpec(blk, rowk(0)), pl.BlockSpec(blk, rowk(nk)),
            pl.BlockSpec(blk, row(2 * nk)), pl.BlockSpec(blk, row(2 * nk + nv)),
            pl.BlockSpec(hblk, hrowk(0)), pl.BlockSpec(hblk, hrowk(nk)), pl.BlockSpec(hblk, hrow(2 * nk)),
            pl.BlockSpec(cblk, crowk(0)), pl.BlockSpec(cblk, crowk(nk)), pl.BlockSpec(cblk, crow(2 * nk)),
            pl.BlockSpec((t, LANES), lambda bi, h: (bi, 0)),
            pl.BlockSpec((1, 1, DK, DV), lambda bi, h: (bi, h, 0, 0)),
            pl.BlockSpec((1, DV), lambda bi, h: (0, 0)),
        ],
        out_specs=[
            pl.BlockSpec((t, DV), lambda bi, h: (bi, h)),
            pl.BlockSpec((1, 1, DK, DV), lambda bi, h: (bi, h, 0, 0)),
        ],
        out_shape=[
            jax.ShapeDtypeStruct((m, nv * DV), BF16),
            jax.ShapeDtypeStruct((b, nv, DK, DV), F32),
        ],
        scratch_shapes=[
            pltpu.VMEM((t + HIST_ROWS, LANES), F32),
            big, big, big, big, big, big, big, big,
            pltpu.VMEM((n_chunks, c, c), F32),
            big,
            pltpu.VMEM((DK, DV), F32),
        ],
        compiler_params=_cparams(("parallel", "arbitrary")),
        name="gdn_core",
    )(ph, ph, ph, ph, hist, hist, hist, cw, cw, cw, gates, s0, nw)


def _mm_res_kernel(a_ref, w_ref, r_ref, o_ref):
    o_ref[...] = r_ref[...] + jnp.dot(a_ref[...], w_ref[...], preferred_element_type=F32)


def _mm_residual(a, w, res, *, tm, tn):
    m, k = a.shape
    n = w.shape[1]
    return pl.pallas_call(
        _mm_res_kernel,
        grid=(m // tm, n // tn),
        in_specs=[
            pl.BlockSpec((tm, k), lambda i, j: (i, 0)),
            pl.BlockSpec((k, tn), lambda i, j: (0, j)),
            pl.BlockSpec((tm, tn), lambda i, j: (i, j)),
        ],
        out_specs=pl.BlockSpec((tm, tn), lambda i, j: (i, j)),
        out_shape=jax.ShapeDtypeStruct((m, n), F32),
        compiler_params=_cparams(("parallel", "arbitrary")),
        name="mm_residual",
    )(a, w, res)


def _ffn_kernel(x_ref, nw_ref, wg_ref, wu_ref, wd_ref, fw_ref, o_ref, h_ref, *, final):
    j = pl.program_id(1)

    @pl.when(j == 0)
    def _():
        x = x_ref[...]
        h_ref[...] = _rms(x, nw_ref[...]).astype(BF16)
        o_ref[...] = x

    h = h_ref[...]
    g = jnp.dot(h, wg_ref[...], preferred_element_type=F32)
    u = jnp.dot(h, wu_ref[...], preferred_element_type=F32)
    act = (g * jax.nn.sigmoid(g) * u).astype(BF16)
    o_ref[...] += jnp.dot(act, wd_ref[...], preferred_element_type=F32)

    if final:
        @pl.when(j == pl.num_programs(1) - 1)
        def _():
            o_ref[...] = _rms(o_ref[...], fw_ref[...])


def _ffn(x, nw, wgu, wd, fw, *, tm, tf, final):
    m, d = x.shape
    f = wd.shape[0]
    nf = f // tf
    return pl.pallas_call(
        functools.partial(_ffn_kernel, final=final),
        grid=(m // tm, nf),
        in_specs=[
            pl.BlockSpec((tm, d), lambda i, j: (i, 0)),
            pl.BlockSpec((1, d), lambda i, j: (0, 0)),
            pl.BlockSpec((d, tf), lambda i, j: (0, j)),
            pl.BlockSpec((d, tf), lambda i, j: (0, nf + j)),
            pl.BlockSpec((tf, d), lambda i, j: (j, 0)),
            pl.BlockSpec((1, d), lambda i, j: (0, 0)),
        ],
        out_specs=pl.BlockSpec((tm, d), lambda i, j: (i, 0)),
        out_shape=jax.ShapeDtypeStruct((m, d), F32),
        scratch_shapes=[pltpu.VMEM((tm, d), BF16)],
        compiler_params=_cparams(("parallel", "arbitrary")),
        name="ffn",
    )(x, nw, wgu, wgu, wd, fw)


def _pool_kernel(x_ref, nw_ref, hist_ref, pw_ref, sc_ref, o_ref, hs_ref, hbuf, *, tt, past):
    ti = pl.program_id(1)

    @pl.when(ti == 0)
    def _():
        hbuf[0:POOL_PAD, :] = hist_ref[0]

    @pl.when(ti > 0)
    def _():
        hbuf[0:POOL_PAD, :] = hbuf[tt:tt + POOL_PAD, :]

    x = x_ref[...]
    h = _rms(x, nw_ref[...])
    hbuf[POOL_PAD:POOL_PAD + tt, :] = h
    tg = ti * tt + lax.broadcasted_iota(jnp.int32, (tt, 1), 0)
    gcw = x.shape[1] // len(POOL_WINDOWS)
    for gi, wlen in enumerate(POOL_WINDOWS):
        cols = slice(gi * gcw, (gi + 1) * gcw)
        acc = h[:, cols]
        for i in range(1, wlen):
            acc = acc + hbuf[POOL_PAD - i:POOL_PAD - i + tt, cols]
        cnt = jnp.minimum(wlen, past + tg + 1).astype(F32)
        dlt = acc / cnt - h[:, cols]
        y = jnp.dot(dlt.astype(BF16), pw_ref[gi], preferred_element_type=F32)
        o_ref[:, cols] = x[:, cols] + y * sc_ref[:, cols]
    hs_ref[0] = hbuf[tt:tt + POOL_PAD, :]


def _pool_mixer(x, nw, hist, pw, sc, *, b, t, tt, past):
    m, d = x.shape
    nt = t // tt
    g, gcw, _ = pw.shape
    return pl.pallas_call(
        functools.partial(_pool_kernel, tt=tt, past=past),
        grid=(b, nt),
        in_specs=[
            pl.BlockSpec((tt, d), lambda bi, ti: (bi * nt + ti, 0)),
            pl.BlockSpec((1, d), lambda bi, ti: (0, 0)),
            pl.BlockSpec((1, POOL_PAD, d), lambda bi, ti: (bi, 0, 0)),
            pl.BlockSpec((g, gcw, gcw), lambda bi, ti: (0, 0, 0)),
            pl.BlockSpec((1, d), lambda bi, ti: (0, 0)),
        ],
        out_specs=[
            pl.BlockSpec((tt, d), lambda bi, ti: (bi * nt + ti, 0)),
            pl.BlockSpec((1, POOL_PAD, d), lambda bi, ti: (bi, 0, 0)),
        ],
        out_shape=[
            jax.ShapeDtypeStruct((m, d), F32),
            jax.ShapeDtypeStruct((b, POOL_PAD, d), F32),
        ],
        scratch_shapes=[pltpu.VMEM((tt + POOL_PAD, d), F32)],
        compiler_params=_cparams(("parallel", "arbitrary")),
        name="pool_mixer",
    )(x, nw, hist, pw, sc)


def _prep_weights(norm_mix_w, norm_ffn_w, final_norm_w, gdn_w_in, gdn_conv_w, gdn_A_log, gdn_dt_bias, gdn_norm_w,
                  gdn_w_out, pool_w, pool_scale, ffn_w_gu, ffn_w_down):
    n_a, d, _ = gdn_w_in.shape
    nv = gdn_A_log.shape[1]
    val_dim = nv * DV
    qkv_dim = gdn_conv_w.shape[2]
    main = qkv_dim + val_dim
    zeros = jnp.zeros((n_a, d, nv), F32)
    w_gate = jnp.stack([gdn_w_in[:, :, main:main + nv], gdn_w_in[:, :, main + nv:main + 2 * nv], zeros, zeros],
                       axis=-1).reshape(n_a, d, nv * GATE_SLOTS).astype(BF16)
    zv = jnp.zeros((n_a, nv), F32)
    alog = jnp.stack([zv, gdn_A_log.astype(F32), zv, zv], axis=-1).reshape(n_a, 1, nv * GATE_SLOTS)
    dtb = jnp.stack([zv, gdn_dt_bias.astype(F32), zv, zv], axis=-1).reshape(n_a, 1, nv * GATE_SLOTS)
    nh = qkv_dim // LANES
    cw = gdn_conv_w.reshape(n_a, CONV_W, nh, LANES).transpose(0, 2, 1, 3)
    cw = jnp.pad(cw, ((0, 0), (0, 0), (0, HIST_ROWS - CONV_W), (0, 0)))
    return dict(
        norm_mix=norm_mix_w[:, None, :], norm_ffn=norm_ffn_w[:, None, :], final=final_norm_w[None, :],
        w_in=gdn_w_in[:, :, :main].astype(BF16), w_gate=w_gate, alog=alog, dtb=dtb, cw=cw,
        gnorm=gdn_norm_w[:, None, :], w_out=gdn_w_out.astype(BF16), pool_w=pool_w.astype(BF16),
        pool_scale=pool_scale[:, None, :], w_gu=ffn_w_gu.astype(BF16), w_down=ffn_w_down.astype(BF16),
    )


def _trunk(x3, conv_hist, s_hist, pool_hist, wts, depth):
    b, t, d = x3.shape
    m = b * t
    x = x3.reshape(m, d)
    nv = s_hist.shape[2]
    nk = nv // 2
    qkv_dim = conv_hist.shape[-1]
    nh = qkv_dim // LANES
    c = min(CHUNK, t)
    tm = min(512, m)
    past = pool_hist.shape[2]
    assert t >= CONV_W - 1 and t >= POOL_PAD - 1
    new_conv, new_s, new_pool = [], [], []
    for i in range(depth):
        j = i // 2
        if i % 2 == 0:
            ph, ba = _norm_proj(x, wts["norm_mix"][i], wts["w_in"][j], wts["w_gate"][j], tm=tm, tn=512)
            gates = _gates(ba, wts["alog"][j], wts["dtb"][j], c=c, tt=min(m, 2048))
            hist = conv_hist[j].reshape(b, CONV_W - 1, nh, LANES).transpose(2, 0, 1, 3)
            hist = jnp.pad(hist, ((0, 0), (0, 0), (HIST_ROWS - (CONV_W - 1), 0), (0, 0)))
            o, s_new = _gdn_core(ph, hist, wts["cw"][j], gates, s_hist[j].astype(F32), wts["gnorm"][j],
                                 b=b, t=t, nk=nk, nv=nv)
            x = _mm_residual(o, wts["w_out"][j], x, tm=tm, tn=512)
            tail = ph[:nh].reshape(nh, b, t, LANES)[:, :, t - (CONV_W - 1):, :]
            new_conv.append(tail.transpose(1, 2, 0, 3).reshape(b, CONV_W - 1, qkv_dim))
            new_s.append(s_new)
        else:
            hist = pool_hist[j]
            hist = jnp.pad(hist, ((0, 0), (POOL_PAD - past, 0), (0, 0)))
            x, hs = _pool_mixer(x, wts["norm_mix"][i], hist, wts["pool_w"][j], wts["pool_scale"][j],
                                b=b, t=t, tt=min(t, 256), past=past)
            new_pool.append(hs[:, POOL_PAD - (POOL_PAD - 1):, :])
        x = _ffn(x, wts["norm_ffn"][i], wts["w_gu"][i], wts["w_down"][i], wts["final"], tm=tm, tf=512,
                 final=(i == depth - 1))
    return x.reshape(b, t, d), jnp.stack(new_conv), jnp.stack(new_s), jnp.stack(new_pool)


def kernel(x_prompt, x_sample, state_gdn_conv, state_gdn_S, state_pool, norm_mix_w, norm_ffn_w, final_norm_w,
           gdn_w_in, gdn_conv_w, gdn_A_log, gdn_dt_bias, gdn_norm_w, gdn_w_out, pool_w, pool_scale, ffn_w_gu,
           ffn_w_down):
    depth = norm_mix_w.shape[0]
    n_a, n_b = state_gdn_conv.shape[0], state_pool.shape[0]
    bp = x_prompt.shape[0]
    wts = _prep_weights(norm_mix_w, norm_ffn_w, final_norm_w, gdn_w_in, gdn_conv_w, gdn_A_log, gdn_dt_bias,
                        gdn_norm_w, gdn_w_out, pool_w, pool_scale, ffn_w_gu, ffn_w_down)
    conv0 = jnp.zeros((n_a, bp) + state_gdn_conv.shape[2:], x_prompt.dtype)
    s00 = jnp.zeros((n_a, bp) + state_gdn_S.shape[2:], F32)
    pool0 = jnp.zeros((n_b, bp, 0, x_prompt.shape[-1]), x_prompt.dtype)
    y_p, conv_p, s_p, pool_p = _trunk(x_prompt, conv0, s00, pool0, wts, depth)
    y_s, conv_s, s_s, pool_s = _trunk(x_sample, state_gdn_conv, state_gdn_S, state_pool, wts, depth)
    return (y_p, y_s, conv_p, s_p.astype(state_gdn_S.dtype), pool_p, conv_s, s_s.astype(state_gdn_S.dtype), pool_s)
```

```python
import functools

import jax
import jax.numpy as jnp
from jax import lax
from jax.experimental import pallas as pl
from jax.experimental.pallas import tpu as pltpu

F32 = jnp.float32
BF16 = jnp.bfloat16

EPS = 1e-6
LANES = 128
DK = 128
DV = 128
CONV_W = 4
HIST_ROWS = 8
CHUNK = 64
INV_BLOCK = 16
GDN_UNROLL = 16
POOL_WINDOWS = (2, 4, 8, 16)
POOL_PAD = 16
GATE_SLOTS = 4
VMEM_LIMIT = 56 * 1024 * 1024


def _cparams(sem):
    return pltpu.CompilerParams(dimension_semantics=sem, vmem_limit_bytes=VMEM_LIMIT)


def _rms(x, w):
    ms = jnp.mean(x * x, axis=-1, keepdims=True)
    return x * lax.rsqrt(ms + EPS) * w


def _mm(a, b):
    return jnp.dot(a.astype(BF16), b.astype(BF16), preferred_element_type=F32)


def _norm_proj_kernel(x_ref, nw_ref, w_ref, wg_ref, o_ref, g_ref, h_ref, *, tn):
    @pl.when(pl.program_id(1) == 0)
    def _():
        h = _rms(x_ref[...], nw_ref[...]).astype(BF16)
        h_ref[...] = h
        g_ref[...] = jnp.dot(h, wg_ref[...], preferred_element_type=F32)

    acc = jnp.dot(h_ref[...], w_ref[...], preferred_element_type=F32)
    for c in range(tn // LANES):
        o_ref[c] = acc[:, c * LANES:(c + 1) * LANES]


def _norm_proj(x, nw, w, wg, *, tm, tn):
    m, d = x.shape
    n = w.shape[1]
    return pl.pallas_call(
        functools.partial(_norm_proj_kernel, tn=tn),
        grid=(m // tm, n // tn),
        in_specs=[
            pl.BlockSpec((tm, d), lambda i, j: (i, 0)),
            pl.BlockSpec((1, d), lambda i, j: (0, 0)),
            pl.BlockSpec((d, tn), lambda i, j: (0, j)),
            pl.BlockSpec((d, LANES), lambda i, j: (0, 0)),
        ],
        out_specs=[
            pl.BlockSpec((tn // LANES, tm, LANES), lambda i, j: (j, i, 0)),
            pl.BlockSpec((tm, LANES), lambda i, j: (i, 0)),
        ],
        out_shape=[
            jax.ShapeDtypeStruct((n // LANES, m, LANES), F32),
            jax.ShapeDtypeStruct((m, LANES), F32),
        ],
        scratch_shapes=[pltpu.VMEM((tm, d), BF16)],
        compiler_params=_cparams(("parallel", "arbitrary")),
        name="norm_proj",
    )(x, nw, w, wg)


def _gates_kernel(ba_ref, alog_ref, dtb_ref, tri_ref, o_ref, *, c):
    x = ba_ref[...]
    beta = jax.nn.sigmoid(x)
    y = x + dtb_ref[...]
    softplus = jnp.maximum(y, 0.0) + jnp.log1p(jnp.exp(-jnp.abs(y)))
    g = -jnp.exp(alog_ref[...]) * softplus
    lane = lax.broadcasted_iota(jnp.int32, (c, x.shape[1]), 1)
    is_beta = (lane & (GATE_SLOTS - 1)) == 0
    tri = tri_ref[...]
    for n in range(x.shape[0] // c):
        sl = slice(n * c, (n + 1) * c)
        gc = jnp.dot(tri, g[sl], preferred_element_type=F32, precision=lax.Precision.HIGHEST)
        o_ref[sl, :] = jnp.where(is_beta, beta[sl], gc)


def _gates(ba, alog, dtb, *, c, tt):
    m = ba.shape[0]
    tri = jnp.tril(jnp.ones((c, c), F32))
    return pl.pallas_call(
        functools.partial(_gates_kernel, c=c),
        grid=(m // tt,),
        in_specs=[
            pl.BlockSpec((tt, LANES), lambda i: (i, 0)),
            pl.BlockSpec((1, LANES), lambda i: (0, 0)),
            pl.BlockSpec((1, LANES), lambda i: (0, 0)),
            pl.BlockSpec((c, c), lambda i: (0, 0)),
        ],
        out_specs=pl.BlockSpec((tt, LANES), lambda i: (i, 0)),
        out_shape=jax.ShapeDtypeStruct((m, LANES), F32),
        compiler_params=_cparams(("parallel",)),
        name="gates",
    )(ba, alog, dtb, tri)


def _inv_unit_lower(l_mats):
    c, w = l_mats[0].shape
    npack = w // c
    r = lax.broadcasted_iota(jnp.int32, (c, w), 0)
    lane = lax.broadcasted_iota(jnp.int32, (c, w), 1)
    s = lane & (c - 1)
    shift = INV_BLOCK.bit_length() - 1
    same = lax.shift_right_logical(r, shift) == lax.shift_right_logical(s, shift)
    eye = jnp.where(r == s, 1.0, 0.0).astype(F32)
    part = [lax.shift_right_logical(lane, c.bit_length() - 1) == p for p in range(npack)]

    def mm(xs, ys):
        if npack > 1:
            ys = [jnp.concatenate([jnp.where(part[p], y, 0.0) for p in range(npack)], axis=0) for y in ys]
        return [_mm(x, y) for x, y in zip(xs, ys)]

    def stack(xs, ys):
        return [jnp.concatenate([x, y], axis=0) for x, y in zip(xs, ys)]

    d1 = [jnp.where(same, l, 0.0) for l in l_mats]
    d2 = mm(d1, d1)
    t = mm(stack(d2, d1), d2)
    d4 = [x[:c] for x in t]
    y = [eye - a + b - x[c:] for a, b, x in zip(d1, d2, t)]
    t = mm(stack(d4, y), d4)
    y = [a + x[c:] for a, x in zip(y, t)]
    x0 = [a + b for a, b in zip(y, mm(y, [x[:c] for x in t]))]
    nblk = c // INV_BLOCK
    if nblk == 1:
        return x0
    e = [jnp.where(same, 0.0, l) for l in l_mats]
    nmat = mm(x0, e)
    z = [eye - a for a in nmat]
    if nblk > 2:
        assert nblk == 4
        n2 = mm(nmat, nmat)
        n3 = mm(nmat, n2)
        z = [a + b - x for a, b, x in zip(z, n2, n3)]
    return mm(z, x0)


def _gdn2_kernel(q_ref, k_ref, v_ref, z_ref, hq_ref, hk_ref, hv_ref, cq_ref, ck_ref, cv_ref, g_ref, s0_ref,
                 nw_ref, o_ref, s_ref, xc_s, gcol_s, bcol_s, begcol_s, gl_s, qk_s, kk_s, kbd_s, vbd_s, kd_s,
                 wq_s, u_s, a_s, o_s, st_s, *, t, c, unroll):
    n_chunks = t // c
    hk = pl.program_id(1)
    half = LANES // 2
    assert c == half

    def conv_silu(x, hist, cw):
        xc_s[0:HIST_ROWS, :] = hist
        xc_s[HIST_ROWS:HIST_ROWS + t, :] = x
        base = HIST_ROWS - (CONV_W - 1)
        acc = xc_s[base:base + t, :] * cw[0:1, :]
        for j in range(1, CONV_W):
            acc = acc + xc_s[base + j:base + j + t, :] * cw[j:j + 1, :]
        return acc * jax.nn.sigmoid(acc)

    def l2n(x):
        return x * lax.rsqrt(jnp.sum(x * x, axis=-1, keepdims=True) + EPS)

    def chunked(x):
        return x.reshape(n_chunks, c, x.shape[-1])

    def chunk_last(x):
        x3 = chunked(x)
        return jnp.broadcast_to(x3[:, c - 1:c, :], x3.shape).reshape(x.shape)

    g = pltpu.roll(g_ref[...], (LANES - 2 * GATE_SLOTS * hk) % LANES, 1)
    left = lax.broadcasted_iota(jnp.int32, (t, LANES), 1) < half
    full = (t, LANES)
    gc = [jnp.broadcast_to(g[:, GATE_SLOTS * h + 1:GATE_SLOTS * h + 2], full) for h in range(2)]
    beta = [jnp.broadcast_to(g[:, GATE_SLOTS * h:GATE_SLOTS * h + 1], full) for h in range(2)]
    eg = [jnp.exp(x) for x in gc]
    gcol_s[...] = jnp.where(left, gc[0], gc[1])
    bcol = jnp.where(left, beta[0], beta[1])
    bcol_s[...] = bcol
    begcol_s[...] = bcol * jnp.where(left, eg[0], eg[1])
    gl = [chunk_last(x) for x in gc]
    ek = [jnp.exp(gl[h] - gc[h]) for h in range(2)]
    for h in range(2):
        gl_s[:, :, h * LANES:(h + 1) * LANES] = jnp.exp(chunked(gl[h])[:, 0:1, :])

    q = l2n(conv_silu(q_ref[0], hq_ref[0, 0], cq_ref[0])) * (DK ** -0.5)
    qk_s[:, 0:c, :] = chunked(q).astype(BF16)
    for h in range(2):
        wq_s[:, (2 * h + 1) * c:(2 * h + 2) * c, :] = chunked(q * eg[h]).astype(BF16)

    k = l2n(conv_silu(k_ref[0], hk_ref[0, 0], ck_ref[0]))
    k3 = chunked(k).astype(BF16)
    zero3 = jnp.zeros_like(k3)
    qk_s[:, c:2 * c, :] = k3
    kk_s[:, 0:c, :] = k3
    kk_s[:, c:2 * c, :] = k3
    for h in range(2):
        kbd_s[:, h * c:(h + 1) * c, h * LANES:(h + 1) * LANES] = k3
        kbd_s[:, h * c:(h + 1) * c, (1 - h) * LANES:(2 - h) * LANES] = zero3
        kd_s[:, h * c:(h + 1) * c, :] = chunked(k * ek[h]).astype(BF16)

    for h in range(2):
        v3 = chunked(conv_silu(v_ref[h], hv_ref[h, 0], cv_ref[h])).astype(BF16)
        vbd_s[:, h * c:(h + 1) * c, h * LANES:(h + 1) * LANES] = v3
        vbd_s[:, h * c:(h + 1) * c, (1 - h) * LANES:(2 - h) * LANES] = zero3

    for h in range(2):
        st_s[:, h * DV:(h + 1) * DV] = s0_ref[0, h]

    r = lax.broadcasted_iota(jnp.int32, (c, LANES), 0)
    sloc = lax.broadcasted_iota(jnp.int32, (c, LANES), 1) & (c - 1)
    causal = r >= sloc
    strict = r > sloc
    diag = r == sloc

    def as_row(x):
        return jnp.sum(jnp.where(diag, x, 0.0), axis=0, keepdims=True)

    def intra(i, carry):
        ns = [i * unroll + j for j in range(unroll)]
        sls = [pl.ds(pl.multiple_of(n * c, c), c) for n in ns]
        gcol = [gcol_s[sl, :] for sl in sls]
        bcl = [bcol_s[sl, :] for sl in sls]
        begcol = [begcol_s[sl, :] for sl in sls]
        decay = [jnp.where(causal, jnp.exp(jnp.where(causal, x - as_row(x), 0.0)), 0.0) for x in gcol]
        qkk = [lax.dot_general(qk_s[n], kk_s[n], (((1,), (1,)), ((), ())), preferred_element_type=F32)
               for n in ns]
        for n, x, d in zip(ns, qkk, decay):
            a_s[n] = (x[:c] * d).astype(BF16)
        tm = _inv_unit_lower([jnp.where(strict, x[c:] * d * b, 0.0) for x, d, b in zip(qkk, decay, bcl)])
        w = [jnp.dot((x * as_row(g)).astype(BF16), kbd_s[n], preferred_element_type=F32)
             for n, x, g in zip(ns, tm, begcol)]
        u = [jnp.dot((x * as_row(b)).astype(BF16), vbd_s[n], preferred_element_type=F32)
             for n, x, b in zip(ns, tm, bcl)]
        for n, wn, un in zip(ns, w, u):
            u_s[n] = un
            for h in range(2):
                wq_s[n, 2 * h * c:(2 * h + 1) * c, :] = wn[:, h * DK:(h + 1) * DK].astype(BF16)
        return carry

    lax.fori_loop(0, n_chunks // unroll, intra, 0)

    zero_c = jnp.zeros((c, DV), F32)

    def scan(n, carry):
        sl = pl.ds(pl.multiple_of(n * c, c), c)
        sp = st_s[...]
        r1 = jnp.dot(wq_s[n], sp.astype(BF16), preferred_element_type=F32)
        u = u_s[n]
        vn0 = u[:, :DV] - r1[0:c, :DV]
        vn1 = u[:, DV:] - r1[2 * c:3 * c, DV:]
        vnbd = jnp.concatenate([jnp.concatenate([vn0, zero_c], axis=1),
                                jnp.concatenate([zero_c, vn1], axis=1)], axis=0).astype(BF16)
        av = jnp.dot(a_s[n], vnbd, preferred_element_type=F32)
        o_s[sl, :] = jnp.concatenate([r1[c:2 * c, :DV], r1[3 * c:4 * c, DV:]], axis=1) + av
        st_s[...] = sp * gl_s[n] + lax.dot_general(kd_s[n], vnbd, (((0,), (0,)), ((), ())),
                                                   preferred_element_type=F32)
        return carry

    lax.fori_loop(0, n_chunks, scan, 0)

    for h in range(2):
        s_ref[0, h] = st_s[:, h * DV:(h + 1) * DV]
        o = o_s[:, h * DV:(h + 1) * DV]
        zf = z_ref[h]
        o = o * lax.rsqrt(jnp.mean(o * o, axis=-1, keepdims=True) + EPS) * nw_ref[...] * (zf * jax.nn.sigmoid(zf))
        o_ref[:, h * DV:(h + 1) * DV] = o.astype(o_ref.dtype)


def _gdn_core_packed(ph, hist, cw, gates, s0, nw, *, b, t, nk, nv, unroll):
    c = CHUNK
    assert t % (c * unroll) == 0 and nv == 2 * nk and DK == LANES and DV == LANES
    m = b * t
    n = t // c
    one = lambda off: (lambda bi, h: (off + h, bi, 0))
    hone = lambda off: (lambda bi, h: (off + h, bi, 0, 0))
    cone = lambda off: (lambda bi, h: (off + h, 0, 0))
    f32 = lambda *shape: pltpu.VMEM(shape, F32)
    bf16 = lambda *shape: pltpu.VMEM(shape, BF16)
    return pl.pallas_call(
        functools.partial(_gdn2_kernel, t=t, c=c, unroll=unroll),
        grid=(b, nk),
        in_specs=[
            pl.BlockSpec((1, t, LANES), one(0)), pl.BlockSpec((1, t, LANES), one(nk)),
            pl.BlockSpec((2, t, LANES), one(nk)), pl.BlockSpec((2, t, LANES), one(nk + nv // 2)),
            pl.BlockSpec((1, 1, HIST_ROWS, LANES), hone(0)), pl.BlockSpec((1, 1, HIST_ROWS, LANES), hone(nk)),
            pl.BlockSpec((2, 1, HIST_ROWS, LANES), hone(nk)),
            pl.BlockSpec((1, HIST_ROWS, LANES), cone(0)), pl.BlockSpec((1, HIST_ROWS, LANES), cone(nk)),
            pl.BlockSpec((2, HIST_ROWS, LANES), cone(nk)),
            pl.BlockSpec((t, LANES), lambda bi, h: (bi, 0)),
            pl.BlockSpec((1, 2, DK, DV), lambda bi, h: (bi, h, 0, 0)),
            pl.BlockSpec((1, DV), lambda bi, h: (0, 0)),
        ],
        out_specs=[
            pl.BlockSpec((t, 2 * DV), lambda bi, h: (bi, h)),
            pl.BlockSpec((1, 2, DK, DV), lambda bi, h: (bi, h, 0, 0)),
        ],
        out_shape=[
            jax.ShapeDtypeStruct((m, nv * DV), BF16),
            jax.ShapeDtypeStruct((b, nv, DK, DV), F32),
        ],
        scratch_shapes=[
            f32(t + HIST_ROWS, LANES), f32(t, LANES), f32(t, LANES), f32(t, LANES), f32(n, 1, 2 * LANES),
            bf16(n, 2 * c, LANES), bf16(n, 2 * c, LANES), bf16(n, 2 * c, 2 * LANES), bf16(n, 2 * c, 2 * LANES),
            bf16(n, 2 * c, LANES), bf16(n, 4 * c, LANES), f32(n, c, 2 * DV), bf16(n, c, LANES),
            f32(t, 2 * DV), f32(DK, 2 * DV),
        ],
        compiler_params=_cparams(("parallel", "arbitrary")),
        name="gdn_core_packed",
    )(ph, ph, ph, ph, hist, hist, hist, cw, cw, cw, gates, s0, nw)


def _gdn_kernel(q_ref, k_ref, v_ref, z_ref, hq_ref, hk_ref, hv_ref, cq_ref, ck_ref, cv_ref, g_ref, s0_ref,
                nw_ref, o_ref, s_ref, xc_s, q_s, k_s, v_s, g_s, w_s, u_s, qg_s, kd_s, a_s, o_s, st_s, *, t, c):
    n_chunks = t // c
    hv = pl.program_id(1)

    def conv_silu(x_ref, h_ref, cw_ref):
        xc_s[0:HIST_ROWS, :] = h_ref[0, 0]
        xc_s[HIST_ROWS:HIST_ROWS + t, :] = x_ref[0]
        cw = cw_ref[0]
        base = HIST_ROWS - (CONV_W - 1)
        acc = xc_s[base:base + t, :] * cw[0:1, :]
        for j in range(1, CONV_W):
            acc = acc + xc_s[base + j:base + j + t, :] * cw[j:j + 1, :]
        return acc * jax.nn.sigmoid(acc)

    def l2n(x):
        return x * lax.rsqrt(jnp.sum(x * x, axis=-1, keepdims=True) + EPS)

    q_s[...] = l2n(conv_silu(q_ref, hq_ref, cq_ref)) * (DK ** -0.5)
    k_s[...] = l2n(conv_silu(k_ref, hk_ref, ck_ref))
    v_s[...] = conv_silu(v_ref, hv_ref, cv_ref)
    g_s[...] = pltpu.roll(g_ref[...], (LANES - GATE_SLOTS * hv) % LANES, 1)
    st_s[...] = s0_ref[0, 0]

    r = lax.broadcasted_iota(jnp.int32, (c, c), 0)
    s = lax.broadcasted_iota(jnp.int32, (c, c), 1)
    causal = r >= s
    strict = r > s
    diag = r == s

    def intra(n, carry):
        row0 = pl.multiple_of(n * c, c)
        sl = pl.ds(row0, c)
        qc, kc, vc, gg = q_s[sl, :], k_s[sl, :], v_s[sl, :], g_s[sl, :]
        beta = gg[:, 0:1]
        gc = gg[:, 1:2]
        gc_row = jnp.sum(jnp.where(diag, gc, 0.0), axis=0, keepdims=True)
        decay = jnp.where(causal, jnp.exp(jnp.where(causal, gc - gc_row, 0.0)), 0.0)
        k16 = kc.astype(BF16)
        qkk = lax.dot_general(jnp.concatenate([qc.astype(BF16), k16], axis=0), k16,
                              (((1,), (1,)), ((), ())), preferred_element_type=F32)
        a_mat = qkk[:c] * decay
        l_mat = jnp.where(strict, qkk[c:] * decay * beta, 0.0)
        tm = _inv_unit_lower([l_mat])[0]
        eg = jnp.exp(gc)
        wu = _mm(tm, jnp.concatenate([kc * (beta * eg), vc * beta], axis=1))
        w_s[sl, :] = wu[:, :DK]
        u_s[sl, :] = wu[:, DK:]
        qg_s[sl, :] = qc * eg
        kd_s[sl, :] = kc * jnp.exp(gc[c - 1:c, :] - gc)
        a_s[n] = a_mat
        return carry

    def scan(n, carry):
        row0 = pl.multiple_of(n * c, c)
        sl = pl.ds(row0, c)
        st = st_s[...]
        wq = _mm(jnp.concatenate([w_s[sl, :], qg_s[sl, :]], axis=0), st)
        v_new = (u_s[sl, :] - wq[:c]).astype(BF16)
        o_s[sl, :] = wq[c:] + _mm(a_s[n], v_new)
        gl = jnp.exp(g_s[pl.ds(row0 + c - 1, 1), 1:2])
        st_s[...] = st * gl + lax.dot_general(kd_s[sl, :].astype(BF16), v_new, (((0,), (0,)), ((), ())),
                                              preferred_element_type=F32)
        return carry

    if n_chunks == 1:
        intra(0, 0)
        scan(0, 0)
    else:
        lax.fori_loop(0, n_chunks, intra, 0)
        lax.fori_loop(0, n_chunks, scan, 0)

    s_ref[0, 0] = st_s[...]
    o = o_s[...]
    zf = z_ref[0]
    o = o * lax.rsqrt(jnp.mean(o * o, axis=-1, keepdims=True) + EPS) * nw_ref[...] * (zf * jax.nn.sigmoid(zf))
    o_ref[...] = o.astype(o_ref.dtype)


def _gdn_core(ph, hist, cw, gates, s0, nw, *, b, t, nk, nv):
    c = min(CHUNK, t)
    assert t % c == 0 and c % INV_BLOCK == 0
    rep = nv // nk
    m = b * t
    n_chunks = t // c
    row = lambda off: (lambda bi, h: (off + h, bi, 0))
    rowk = lambda off: (lambda bi, h: (off + h // rep, bi, 0))
    hrow = lambda off: (lambda bi, h: (off + h, bi, 0, 0))
    hrowk = lambda off: (lambda bi, h: (off + h // rep, bi, 0, 0))
    crow = lambda off: (lambda bi, h: (off + h, 0, 0))
    crowk = lambda off: (lambda bi, h: (off + h // rep, 0, 0))
    blk = (1, t, LANES)
    hblk = (1, 1, HIST_ROWS, LANES)
    cblk = (1, HIST_ROWS, LANES)
    big = pltpu.VMEM((t, LANES), F32)
    return pl.pallas_call(
        functools.partial(_gdn_kernel, t=t, c=c),
        grid=(b, nv),
        in_specs=[
            pl.BlockSpec(blk, rowk(0)), pl.BlockSpec(blk, rowk(nk)),
            pl.BlockSpec(blk, row(2 * nk)), pl.BlockSpec(blk, row(2 * nk + nv)),
            pl.BlockSpec(hblk, hrowk(0)), pl.BlockSpec(hblk, hrowk(nk)), pl.BlockSpec(hblk, hrow(2 * nk)),
            pl.BlockSpec(cblk, crowk(0)), pl.BlockSpec(cblk, crowk(nk)), pl.BlockSpec(cblk, crow(2 * nk)),
            pl.BlockSpec((t, LANES), lambda bi, h: (bi, 0)),
            pl.BlockSpec((1, 1, DK, DV), lambda bi, h: (bi, h, 0, 0)),
            pl.BlockSpec((1, DV), lambda bi, h: (0, 0)),
        ],
        out_specs=[
            pl.BlockSpec((t, DV), lambda bi, h: (bi, h)),
            pl.BlockSpec((1, 1, DK, DV), lambda bi, h: (bi, h, 0, 0)),
        ],
        out_shape=[
            jax.ShapeDtypeStruct((m, nv * DV), BF16),
            jax.ShapeDtypeStruct((b, nv, DK, DV), F32),
        ],
        scratch_shapes=[
            pltpu.VMEM((t + HIST_ROWS, LANES), F32),
            big, big, big, big, big, big, big, big,
            pltpu.VMEM((n_chunks, c, c), F32),
            big,
            pltpu.VMEM((DK, DV), F32),
        ],
        compiler_params=_cparams(("parallel", "arbitrary")),
        name="gdn_core",
    )(ph, ph, ph, ph, hist, hist, hist, cw, cw, cw, gates, s0, nw)


def _mm_res_kernel(a_ref, w_ref, r_ref, o_ref):
    o_ref[...] = r_ref[...] + jnp.dot(a_ref[...], w_ref[...], preferred_element_type=F32)


def _mm_residual(a, w, res, *, tm, tn):
    m, k = a.shape
    n = w.shape[1]
    return pl.pallas_call(
        _mm_res_kernel,
        grid=(m // tm, n // tn),
        in_specs=[
            pl.BlockSpec((tm, k), lambda i, j: (i, 0)),
            pl.BlockSpec((k, tn), lambda i, j: (0, j)),
            pl.BlockSpec((tm, tn), lambda i, j: (i, j)),
        ],
        out_specs=pl.BlockSpec((tm, tn), lambda i, j: (i, j)),
        out_shape=jax.ShapeDtypeStruct((m, n), F32),
        compiler_params=_cparams(("parallel", "arbitrary")),
        name="mm_residual",
    )(a, w, res)


def _ffn_kernel(x_ref, nw_ref, wg_ref, wu_ref, wd_ref, fw_ref, o_ref, h_ref, *, final):
    j = pl.program_id(1)

    @pl.when(j == 0)
    def _():
        x = x_ref[...]
        h_ref[...] = _rms(x, nw_ref[...]).astype(BF16)
        o_ref[...] = x

    h = h_ref[...]
    g = jnp.dot(h, wg_ref[...], preferred_element_type=F32)
    u = jnp.dot(h, wu_ref[...], preferred_element_type=F32)
    act = (g * jax.nn.sigmoid(g) * u).astype(BF16)
    o_ref[...] += jnp.dot(act, wd_ref[...], preferred_element_type=F32)

    if final:
        @pl.when(j == pl.num_programs(1) - 1)
        def _():
            o_ref[...] = _rms(o_ref[...], fw_ref[...])


def _ffn(x, nw, wgu, wd, fw, *, tm, tf, final):
    m, d = x.shape
    f = wd.shape[0]
    nf = f // tf
    return pl.pallas_call(
        functools.partial(_ffn_kernel, final=final),
        grid=(m // tm, nf),
        in_specs=[
            pl.BlockSpec((tm, d), lambda i, j: (i, 0)),
            pl.BlockSpec((1, d), lambda i, j: (0, 0)),
            pl.BlockSpec((d, tf), lambda i, j: (0, j)),
            pl.BlockSpec((d, tf), lambda i, j: (0, nf + j)),
            pl.BlockSpec((tf, d), lambda i, j: (j, 0)),
            pl.BlockSpec((1, d), lambda i, j: (0, 0)),
        ],
        out_specs=pl.BlockSpec((tm, d), lambda i, j: (i, 0)),
        out_shape=jax.ShapeDtypeStruct((m, d), F32),
        scratch_shapes=[pltpu.VMEM((tm, d), BF16)],
        compiler_params=_cparams(("parallel", "arbitrary")),
        name="ffn",
    )(x, nw, wgu, wgu, wd, fw)


def _pool_kernel(x_ref, nw_ref, hist_ref, pw_ref, sc_ref, o_ref, hs_ref, hbuf, *, tt, past):
    ti = pl.program_id(1)

    @pl.when(ti == 0)
    def _():
        hbuf[0:POOL_PAD, :] = hist_ref[0]

    @pl.when(ti > 0)
    def _():
        hbuf[0:POOL_PAD, :] = hbuf[tt:tt + POOL_PAD, :]

    x = x_ref[...]
    h = _rms(x, nw_ref[...])
    hbuf[POOL_PAD:POOL_PAD + tt, :] = h
    tg = ti * tt + lax.broadcasted_iota(jnp.int32, (tt, 1), 0)
    gcw = x.shape[1] // len(POOL_WINDOWS)
    for gi, wlen in enumerate(POOL_WINDOWS):
        cols = slice(gi * gcw, (gi + 1) * gcw)
        acc = h[:, cols]
        for i in range(1, wlen):
            acc = acc + hbuf[POOL_PAD - i:POOL_PAD - i + tt, cols]
        cnt = jnp.minimum(wlen, past + tg + 1).astype(F32)
        dlt = acc / cnt - h[:, cols]
        y = jnp.dot(dlt.astype(BF16), pw_ref[gi], preferred_element_type=F32)
        o_ref[:, cols] = x[:, cols] + y * sc_ref[:, cols]
    hs_ref[0] = hbuf[tt:tt + POOL_PAD, :]


def _pool_mixer(x, nw, hist, pw, sc, *, b, t, tt, past):
    m, d = x.shape
    nt = t // tt
    g, gcw, _ = pw.shape
    return pl.pallas_call(
        functools.partial(_pool_kernel, tt=tt, past=past),
        grid=(b, nt),
        in_specs=[
            pl.BlockSpec((tt, d), lambda bi, ti: (bi * nt + ti, 0)),
            pl.BlockSpec((1, d), lambda bi, ti: (0, 0)),
            pl.BlockSpec((1, POOL_PAD, d), lambda bi, ti: (bi, 0, 0)),
            pl.BlockSpec((g, gcw, gcw), lambda bi, ti: (0, 0, 0)),
            pl.BlockSpec((1, d), lambda bi, ti: (0, 0)),
        ],
        out_specs=[
            pl.BlockSpec((tt, d), lambda bi, ti: (bi * nt + ti, 0)),
            pl.BlockSpec((1, POOL_PAD, d), lambda bi, ti: (bi, 0, 0)),
        ],
        out_shape=[
            jax.ShapeDtypeStruct((m, d), F32),
            jax.ShapeDtypeStruct((b, POOL_PAD, d), F32),
        ],
        scratch_shapes=[pltpu.VMEM((tt + POOL_PAD, d), F32)],
        compiler_params=_cparams(("parallel", "arbitrary")),
        name="pool_mixer",
    )(x, nw, hist, pw, sc)


def _prep_weights(norm_mix_w, norm_ffn_w, final_norm_w, gdn_w_in, gdn_conv_w, gdn_A_log, gdn_dt_bias, gdn_norm_w,
                  gdn_w_out, pool_w, pool_scale, ffn_w_gu, ffn_w_down):
    n_a, d, _ = gdn_w_in.shape
    nv = gdn_A_log.shape[1]
    val_dim = nv * DV
    qkv_dim = gdn_conv_w.shape[2]
    main = qkv_dim + val_dim
    zeros = jnp.zeros((n_a, d, nv), F32)
    w_gate = jnp.stack([gdn_w_in[:, :, main:main + nv], gdn_w_in[:, :, main + nv:main + 2 * nv], zeros, zeros],
                       axis=-1).reshape(n_a, d, nv * GATE_SLOTS).astype(BF16)
    zv = jnp.zeros((n_a, nv), F32)
    alog = jnp.stack([zv, gdn_A_log.astype(F32), zv, zv], axis=-1).reshape(n_a, 1, nv * GATE_SLOTS)
    dtb = jnp.stack([zv, gdn_dt_bias.astype(F32), zv, zv], axis=-1).reshape(n_a, 1, nv * GATE_SLOTS)
    nh = qkv_dim // LANES
    cw = gdn_conv_w.reshape(n_a, CONV_W, nh, LANES).transpose(0, 2, 1, 3)
    cw = jnp.pad(cw, ((0, 0), (0, 0), (0, HIST_ROWS - CONV_W), (0, 0)))
    return dict(
        norm_mix=norm_mix_w[:, None, :], norm_ffn=norm_ffn_w[:, None, :], final=final_norm_w[None, :],
        w_in=gdn_w_in[:, :, :main].astype(BF16), w_gate=w_gate, alog=alog, dtb=dtb, cw=cw,
        gnorm=gdn_norm_w[:, None, :], w_out=gdn_w_out.astype(BF16), pool_w=pool_w.astype(BF16),
        pool_scale=pool_scale[:, None, :], w_gu=ffn_w_gu.astype(BF16), w_down=ffn_w_down.astype(BF16),
    )


def _trunk(x3, conv_hist, s_hist, pool_hist, wts, depth):
    b, t, d = x3.shape
    m = b * t
    x = x3.reshape(m, d)
    nv = s_hist.shape[2]
    nk = nv // 2
    qkv_dim = conv_hist.shape[-1]
    nh = qkv_dim // LANES
    c = min(CHUNK, t)
    tm = min(512, m)
    past = pool_hist.shape[2]
    assert t >= CONV_W - 1 and t >= POOL_PAD - 1
    new_conv, new_s, new_pool = [], [], []
    for i in range(depth):
        j = i // 2
        if i % 2 == 0:
            ph, ba = _norm_proj(x, wts["norm_mix"][i], wts["w_in"][j], wts["w_gate"][j], tm=tm, tn=512)
            gates = _gates(ba, wts["alog"][j], wts["dtb"][j], c=c, tt=min(m, 2048))
            hist = conv_hist[j].reshape(b, CONV_W - 1, nh, LANES).transpose(2, 0, 1, 3)
            hist = jnp.pad(hist, ((0, 0), (0, 0), (HIST_ROWS - (CONV_W - 1), 0), (0, 0)))
            if t % (CHUNK * GDN_UNROLL) == 0:
                o, s_new = _gdn_core_packed(ph, hist, wts["cw"][j], gates, s_hist[j].astype(F32), wts["gnorm"][j],
                                            b=b, t=t, nk=nk, nv=nv, unroll=GDN_UNROLL)
            else:
                o, s_new = _gdn_core(ph, hist, wts["cw"][j], gates, s_hist[j].astype(F32), wts["gnorm"][j],
                                     b=b, t=t, nk=nk, nv=nv)
            x = _mm_residual(o, wts["w_out"][j], x, tm=tm, tn=512)
            tail = ph[:nh].reshape(nh, b, t, LANES)[:, :, t - (CONV_W - 1):, :]
            new_conv.append(tail.transpose(1, 2, 0, 3).reshape(b, CONV_W - 1, qkv_dim))
            new_s.append(s_new)
        else:
            hist = pool_hist[j]
            hist = jnp.pad(hist, ((0, 0), (POOL_PAD - past, 0), (0, 0)))
            x, hs = _pool_mixer(x, wts["norm_mix"][i], hist, wts["pool_w"][j], wts["pool_scale"][j],
                                b=b, t=t, tt=min(t, 256), past=past)
            new_pool.append(hs[:, POOL_PAD - (POOL_PAD - 1):, :])
        x = _ffn(x, wts["norm_ffn"][i], wts["w_gu"][i], wts["w_down"][i], wts["final"], tm=tm, tf=512,
                 final=(i == depth - 1))
    return x.reshape(b, t, d), jnp.stack(new_conv), jnp.stack(new_s), jnp.stack(new_pool)


def kernel(x_prompt, x_sample, state_gdn_conv, state_gdn_S, state_pool, norm_mix_w, norm_ffn_w, final_norm_w,
           gdn_w_in, gdn_conv_w, gdn_A_log, gdn_dt_bias, gdn_norm_w, gdn_w_out, pool_w, pool_scale, ffn_w_gu,
           ffn_w_down):
    depth = norm_mix_w.shape[0]
    n_a, n_b = state_gdn_conv.shape[0], state_pool.shape[0]
    bp = x_prompt.shape[0]
    wts = _prep_weights(norm_mix_w, norm_ffn_w, final_norm_w, gdn_w_in, gdn_conv_w, gdn_A_log, gdn_dt_bias,
                        gdn_norm_w, gdn_w_out, pool_w, pool_scale, ffn_w_gu, ffn_w_down)
    conv0 = jnp.zeros((n_a, bp) + state_gdn_conv.shape[2:], x_prompt.dtype)
    s00 = jnp.zeros((n_a, bp) + state_gdn_S.shape[2:], F32)
    pool0 = jnp.zeros((n_b, bp, 0, x_prompt.shape[-1]), x_prompt.dtype)
    y_p, conv_p, s_p, pool_p = _trunk(x_prompt, conv0, s00, pool0, wts, depth)
    y_s, conv_s, s_s, pool_s = _trunk(x_sample, state_gdn_conv, state_gdn_S, state_pool, wts, depth)
    return (y_p, y_s, conv_p, s_p.astype(state_gdn_S.dtype), pool_p, conv_s, s_s.astype(state_gdn_S.dtype), pool_s)
```

```python
import functools

import jax
import jax.numpy as jnp
from jax import lax
from jax.experimental import pallas as pl
from jax.experimental.pallas import tpu as pltpu

F32 = jnp.float32
BF16 = jnp.bfloat16

EPS = 1e-6
LANES = 128
DK = 128
DV = 128
CONV_W = 4
HIST_ROWS = 8
CHUNK = 64
INV_BLOCK = 16
GDN_TILE = 512
GDN_HEADS = 4
GDN_UNROLL = 16
POOL_WINDOWS = (2, 4, 8, 16)
POOL_PAD = 16
GATE_SLOTS = 4
VMEM_LIMIT = 56 * 1024 * 1024


def _cparams(sem):
    return pltpu.CompilerParams(dimension_semantics=sem, vmem_limit_bytes=VMEM_LIMIT)


def _rms(x, w):
    ms = jnp.mean(x * x, axis=-1, keepdims=True)
    return x * lax.rsqrt(ms + EPS) * w


def _mm(a, b):
    return jnp.dot(a.astype(BF16), b.astype(BF16), preferred_element_type=F32)


def _norm_proj_kernel(x_ref, nw_ref, w_ref, wg_ref, o_ref, g_ref, h_ref, *, tn):
    @pl.when(pl.program_id(1) == 0)
    def _():
        h = _rms(x_ref[...], nw_ref[...]).astype(BF16)
        h_ref[...] = h
        g_ref[...] = jnp.dot(h, wg_ref[...], preferred_element_type=F32)

    acc = jnp.dot(h_ref[...], w_ref[...], preferred_element_type=F32)
    for c in range(tn // LANES):
        o_ref[c] = acc[:, c * LANES:(c + 1) * LANES]


def _norm_proj(x, nw, w, wg, *, n, tm, tn):
    m, d = x.shape
    assert n % tn == 0 and n <= w.shape[1]
    return pl.pallas_call(
        functools.partial(_norm_proj_kernel, tn=tn),
        grid=(m // tm, n // tn),
        in_specs=[
            pl.BlockSpec((tm, d), lambda i, j: (i, 0)),
            pl.BlockSpec((1, d), lambda i, j: (0, 0)),
            pl.BlockSpec((d, tn), lambda i, j: (0, j)),
            pl.BlockSpec((d, LANES), lambda i, j: (0, 0)),
        ],
        out_specs=[
            pl.BlockSpec((tn // LANES, tm, LANES), lambda i, j: (j, i, 0)),
            pl.BlockSpec((tm, LANES), lambda i, j: (i, 0)),
        ],
        out_shape=[
            jax.ShapeDtypeStruct((n // LANES, m, LANES), F32),
            jax.ShapeDtypeStruct((m, LANES), F32),
        ],
        scratch_shapes=[pltpu.VMEM((tm, d), BF16)],
        compiler_params=_cparams(("parallel", "arbitrary")),
        name="norm_proj",
    )(x, nw, w, wg)


def _gates_kernel(ba_ref, alog_ref, dtb_ref, tri_ref, o_ref, *, c):
    x = ba_ref[...]
    beta = jax.nn.sigmoid(x)
    y = x + dtb_ref[...]
    softplus = jnp.maximum(y, 0.0) + jnp.log1p(jnp.exp(-jnp.abs(y)))
    g = -jnp.exp(alog_ref[...]) * softplus
    lane = lax.broadcasted_iota(jnp.int32, (c, x.shape[1]), 1)
    is_beta = (lane & (GATE_SLOTS - 1)) == 0
    tri = tri_ref[...]
    for n in range(x.shape[0] // c):
        sl = slice(n * c, (n + 1) * c)
        gc = jnp.dot(tri, g[sl], preferred_element_type=F32, precision=lax.Precision.HIGHEST)
        o_ref[sl, :] = jnp.where(is_beta, beta[sl], gc)


def _gates(ba, alog, dtb, *, c, tt):
    m = ba.shape[0]
    tri = jnp.tril(jnp.ones((c, c), F32))
    return pl.pallas_call(
        functools.partial(_gates_kernel, c=c),
        grid=(m // tt,),
        in_specs=[
            pl.BlockSpec((tt, LANES), lambda i: (i, 0)),
            pl.BlockSpec((1, LANES), lambda i: (0, 0)),
            pl.BlockSpec((1, LANES), lambda i: (0, 0)),
            pl.BlockSpec((c, c), lambda i: (0, 0)),
        ],
        out_specs=pl.BlockSpec((tt, LANES), lambda i: (i, 0)),
        out_shape=jax.ShapeDtypeStruct((m, LANES), F32),
        compiler_params=_cparams(("parallel",)),
        name="gates",
    )(ba, alog, dtb, tri)


def _inv_unit_lower(l_mats):
    c, w = l_mats[0].shape
    npack = w // c
    r = lax.broadcasted_iota(jnp.int32, (c, w), 0)
    lane = lax.broadcasted_iota(jnp.int32, (c, w), 1)
    s = lane & (c - 1)
    shift = INV_BLOCK.bit_length() - 1
    same = lax.shift_right_logical(r, shift) == lax.shift_right_logical(s, shift)
    eye = jnp.where(r == s, 1.0, 0.0).astype(F32)
    part = [lax.shift_right_logical(lane, c.bit_length() - 1) == p for p in range(npack)]

    def mm(xs, ys):
        if npack > 1:
            ys = [jnp.concatenate([jnp.where(part[p], y, 0.0) for p in range(npack)], axis=0) for y in ys]
        return [_mm(x, y) for x, y in zip(xs, ys)]

    def stack(xs, ys):
        return [jnp.concatenate([x, y], axis=0) for x, y in zip(xs, ys)]

    d1 = [jnp.where(same, l, 0.0) for l in l_mats]
    d2 = mm(d1, d1)
    t = mm(stack(d2, d1), d2)
    d4 = [x[:c] for x in t]
    y = [eye - a + b - x[c:] for a, b, x in zip(d1, d2, t)]
    t = mm(stack(d4, y), d4)
    y = [a + x[c:] for a, x in zip(y, t)]
    x0 = [a + b for a, b in zip(y, mm(y, [x[:c] for x in t]))]
    nblk = c // INV_BLOCK
    if nblk == 1:
        return x0
    e = [jnp.where(same, 0.0, l) for l in l_mats]
    nmat = mm(x0, e)
    z = [eye - a for a in nmat]
    if nblk > 2:
        assert nblk == 4
        n2 = mm(nmat, nmat)
        n3 = mm(nmat, n2)
        z = [a + b - x for a, b, x in zip(z, n2, n3)]
    return mm(z, x0)


def _gdn2_kernel(q_ref, k_ref, v_ref, z_ref, hq_ref, hk_ref, hv_ref, cq_ref, ck_ref, cv_ref, g_ref, s0_ref,
                 nw_ref, o_ref, s_ref, xc_s, gcol_s, bcol_s, begcol_s, gl_s, qk_s, kk_s, kbd_s, vbd_s, kd_s,
                 wq_s, u_s, a_s, o_s, st_s, *, t, c, unroll):
    n_chunks = t // c
    hk = pl.program_id(1)
    half = LANES // 2
    assert c == half

    def conv_silu(x, hist, cw):
        xc_s[0:HIST_ROWS, :] = hist
        xc_s[HIST_ROWS:HIST_ROWS + t, :] = x
        base = HIST_ROWS - (CONV_W - 1)
        acc = xc_s[base:base + t, :] * cw[0:1, :]
        for j in range(1, CONV_W):
            acc = acc + xc_s[base + j:base + j + t, :] * cw[j:j + 1, :]
        return acc * jax.nn.sigmoid(acc)

    def l2n(x):
        return x * lax.rsqrt(jnp.sum(x * x, axis=-1, keepdims=True) + EPS)

    def chunked(x):
        return x.reshape(n_chunks, c, x.shape[-1])

    def chunk_last(x):
        x3 = chunked(x)
        return jnp.broadcast_to(x3[:, c - 1:c, :], x3.shape).reshape(x.shape)

    g = pltpu.roll(g_ref[...], (LANES - 2 * GATE_SLOTS * hk) % LANES, 1)
    left = lax.broadcasted_iota(jnp.int32, (t, LANES), 1) < half
    full = (t, LANES)
    gc = [jnp.broadcast_to(g[:, GATE_SLOTS * h + 1:GATE_SLOTS * h + 2], full) for h in range(2)]
    beta = [jnp.broadcast_to(g[:, GATE_SLOTS * h:GATE_SLOTS * h + 1], full) for h in range(2)]
    eg = [jnp.exp(x) for x in gc]
    gcol_s[...] = jnp.where(left, gc[0], gc[1])
    bcol = jnp.where(left, beta[0], beta[1])
    bcol_s[...] = bcol
    begcol_s[...] = bcol * jnp.where(left, eg[0], eg[1])
    gl = [chunk_last(x) for x in gc]
    ek = [jnp.exp(gl[h] - gc[h]) for h in range(2)]
    for h in range(2):
        gl_s[:, :, h * LANES:(h + 1) * LANES] = jnp.exp(chunked(gl[h])[:, 0:1, :])

    q = l2n(conv_silu(q_ref[0], hq_ref[0, 0], cq_ref[0])) * (DK ** -0.5)
    qk_s[:, 0:c, :] = chunked(q).astype(BF16)
    for h in range(2):
        wq_s[:, (2 * h + 1) * c:(2 * h + 2) * c, :] = chunked(q * eg[h]).astype(BF16)

    k = l2n(conv_silu(k_ref[0], hk_ref[0, 0], ck_ref[0]))
    k3 = chunked(k).astype(BF16)
    zero3 = jnp.zeros_like(k3)
    qk_s[:, c:2 * c, :] = k3
    kk_s[:, 0:c, :] = k3
    kk_s[:, c:2 * c, :] = k3
    for h in range(2):
        kbd_s[:, h * c:(h + 1) * c, h * LANES:(h + 1) * LANES] = k3
        kbd_s[:, h * c:(h + 1) * c, (1 - h) * LANES:(2 - h) * LANES] = zero3
        kd_s[:, h * c:(h + 1) * c, :] = chunked(k * ek[h]).astype(BF16)

    for h in range(2):
        v3 = chunked(conv_silu(v_ref[h], hv_ref[h, 0], cv_ref[h])).astype(BF16)
        vbd_s[:, h * c:(h + 1) * c, h * LANES:(h + 1) * LANES] = v3
        vbd_s[:, h * c:(h + 1) * c, (1 - h) * LANES:(2 - h) * LANES] = zero3

    for h in range(2):
        st_s[:, h * DV:(h + 1) * DV] = s0_ref[0, h]

    r = lax.broadcasted_iota(jnp.int32, (c, LANES), 0)
    sloc = lax.broadcasted_iota(jnp.int32, (c, LANES), 1) & (c - 1)
    causal = r >= sloc
    strict = r > sloc
    diag = r == sloc

    def as_row(x):
        return jnp.sum(jnp.where(diag, x, 0.0), axis=0, keepdims=True)

    def intra(i, carry):
        ns = [i * unroll + j for j in range(unroll)]
        sls = [pl.ds(pl.multiple_of(n * c, c), c) for n in ns]
        gcol = [gcol_s[sl, :] for sl in sls]
        bcl = [bcol_s[sl, :] for sl in sls]
        begcol = [begcol_s[sl, :] for sl in sls]
        decay = [jnp.where(causal, jnp.exp(jnp.where(causal, x - as_row(x), 0.0)), 0.0) for x in gcol]
        qkk = [lax.dot_general(qk_s[n], kk_s[n], (((1,), (1,)), ((), ())), preferred_element_type=F32)
               for n in ns]
        for n, x, d in zip(ns, qkk, decay):
            a_s[n] = (x[:c] * d).astype(BF16)
        tm = _inv_unit_lower([jnp.where(strict, x[c:] * d * b, 0.0) for x, d, b in zip(qkk, decay, bcl)])
        w = [jnp.dot((x * as_row(g)).astype(BF16), kbd_s[n], preferred_element_type=F32)
             for n, x, g in zip(ns, tm, begcol)]
        u = [jnp.dot((x * as_row(b)).astype(BF16), vbd_s[n], preferred_element_type=F32)
             for n, x, b in zip(ns, tm, bcl)]
        for n, wn, un in zip(ns, w, u):
            u_s[n] = un
            for h in range(2):
                wq_s[n, 2 * h * c:(2 * h + 1) * c, :] = wn[:, h * DK:(h + 1) * DK].astype(BF16)
        return carry

    lax.fori_loop(0, n_chunks // unroll, intra, 0)

    zero_c = jnp.zeros((c, DV), F32)

    def scan(n, carry):
        sl = pl.ds(pl.multiple_of(n * c, c), c)
        sp = st_s[...]
        r1 = jnp.dot(wq_s[n], sp.astype(BF16), preferred_element_type=F32)
        u = u_s[n]
        vn0 = u[:, :DV] - r1[0:c, :DV]
        vn1 = u[:, DV:] - r1[2 * c:3 * c, DV:]
        vnbd = jnp.concatenate([jnp.concatenate([vn0, zero_c], axis=1),
                                jnp.concatenate([zero_c, vn1], axis=1)], axis=0).astype(BF16)
        av = jnp.dot(a_s[n], vnbd, preferred_element_type=F32)
        o_s[sl, :] = jnp.concatenate([r1[c:2 * c, :DV], r1[3 * c:4 * c, DV:]], axis=1) + av
        st_s[...] = sp * gl_s[n] + lax.dot_general(kd_s[n], vnbd, (((0,), (0,)), ((), ())),
                                                   preferred_element_type=F32)
        return carry

    lax.fori_loop(0, n_chunks, scan, 0)

    for h in range(2):
        s_ref[0, h] = st_s[:, h * DV:(h + 1) * DV]
        o = o_s[:, h * DV:(h + 1) * DV]
        zf = z_ref[h]
        o = o * lax.rsqrt(jnp.mean(o * o, axis=-1, keepdims=True) + EPS) * nw_ref[...] * (zf * jax.nn.sigmoid(zf))
        o_ref[:, h * DV:(h + 1) * DV] = o.astype(o_ref.dtype)


def _gdn_core_packed(ph, hist, cw, gates, s0, nw, *, b, t, nk, nv, unroll):
    c = CHUNK
    assert t % (c * unroll) == 0 and nv == 2 * nk and DK == LANES and DV == LANES
    m = b * t
    n = t // c
    one = lambda off: (lambda bi, h: (off + h, bi, 0))
    hone = lambda off: (lambda bi, h: (off + h, bi, 0, 0))
    cone = lambda off: (lambda bi, h: (off + h, 0, 0))
    f32 = lambda *shape: pltpu.VMEM(shape, F32)
    bf16 = lambda *shape: pltpu.VMEM(shape, BF16)
    return pl.pallas_call(
        functools.partial(_gdn2_kernel, t=t, c=c, unroll=unroll),
        grid=(b, nk),
        in_specs=[
            pl.BlockSpec((1, t, LANES), one(0)), pl.BlockSpec((1, t, LANES), one(nk)),
            pl.BlockSpec((2, t, LANES), one(nk)), pl.BlockSpec((2, t, LANES), one(nk + nv // 2)),
            pl.BlockSpec((1, 1, HIST_ROWS, LANES), hone(0)), pl.BlockSpec((1, 1, HIST_ROWS, LANES), hone(nk)),
            pl.BlockSpec((2, 1, HIST_ROWS, LANES), hone(nk)),
            pl.BlockSpec((1, HIST_ROWS, LANES), cone(0)), pl.BlockSpec((1, HIST_ROWS, LANES), cone(nk)),
            pl.BlockSpec((2, HIST_ROWS, LANES), cone(nk)),
            pl.BlockSpec((t, LANES), lambda bi, h: (bi, 0)),
            pl.BlockSpec((1, 2, DK, DV), lambda bi, h: (bi, h, 0, 0)),
            pl.BlockSpec((1, DV), lambda bi, h: (0, 0)),
        ],
        out_specs=[
            pl.BlockSpec((t, 2 * DV), lambda bi, h: (bi, h)),
            pl.BlockSpec((1, 2, DK, DV), lambda bi, h: (bi, h, 0, 0)),
        ],
        out_shape=[
            jax.ShapeDtypeStruct((m, nv * DV), BF16),
            jax.ShapeDtypeStruct((b, nv, DK, DV), F32),
        ],
        scratch_shapes=[
            f32(t + HIST_ROWS, LANES), f32(t, LANES), f32(t, LANES), f32(t, LANES), f32(n, 1, 2 * LANES),
            bf16(n, 2 * c, LANES), bf16(n, 2 * c, LANES), bf16(n, 2 * c, 2 * LANES), bf16(n, 2 * c, 2 * LANES),
            bf16(n, 2 * c, LANES), bf16(n, 4 * c, LANES), f32(n, c, 2 * DV), bf16(n, c, LANES),
            f32(t, 2 * DV), f32(DK, 2 * DV),
        ],
        compiler_params=_cparams(("parallel", "arbitrary")),
        name="gdn_core_packed",
    )(ph, ph, ph, ph, hist, hist, hist, cw, cw, cw, gates, s0, nw)


def _gdn3_kernel(q_ref, k_ref, v_ref, z_ref, hq_ref, hk_ref, hv_ref, cq_ref, ck_ref, cv_ref, g_ref, s0_ref,
                 nw_ref, o_ref, s_ref, xc_s, hist_s, gcol_s, bcol_s, begcol_s, gl_s, qk_s, kk_s, kbd_s, vbd_s,
                 kd_s, wq_s, u_s, a_s, o_s, st_s, *, tt, c, heads, unroll):
    n = tt // c
    hg = pl.program_id(1)
    half = LANES // 2
    assert c == half

    @pl.when(pl.program_id(2) == 0)
    def _():
        for g in range(heads):
            hist_s[g, 0] = hq_ref[g, 0]
            hist_s[g, 1] = hk_ref[g, 0]
            for h in range(2):
                hist_s[g, 2 + h] = hv_ref[2 * g + h, 0]
                st_s[g, :, h * DV:(h + 1) * DV] = s0_ref[0, 2 * g + h]

    def conv_silu(x, g, slot, cw):
        xc_s[0:HIST_ROWS, :] = hist_s[g, slot]
        xc_s[HIST_ROWS:HIST_ROWS + tt, :] = x
        hist_s[g, slot] = xc_s[tt:tt + HIST_ROWS, :]
        base = HIST_ROWS - (CONV_W - 1)
        acc = xc_s[base:base + tt, :] * cw[0:1, :]
        for j in range(1, CONV_W):
            acc = acc + xc_s[base + j:base + j + tt, :] * cw[j:j + 1, :]
        return acc * jax.nn.sigmoid(acc)

    def l2n(x):
        return x * lax.rsqrt(jnp.sum(x * x, axis=-1, keepdims=True) + EPS)

    def chunked(x):
        return x.reshape(n, c, x.shape[-1])

    def chunk_last(x):
        x3 = chunked(x)
        return jnp.broadcast_to(x3[:, c - 1:c, :], x3.shape).reshape(x.shape)

    left = lax.broadcasted_iota(jnp.int32, (tt, LANES), 1) < half
    full = (tt, LANES)
    for g in range(heads):
        rows = slice(g * tt, (g + 1) * tt)
        jobs = slice(g * n, (g + 1) * n)
        shift = (LANES - 2 * GATE_SLOTS * (hg * heads + g)) & (LANES - 1)
        gt = pltpu.roll(g_ref[...], shift, 1)
        gc = [jnp.broadcast_to(gt[:, GATE_SLOTS * h + 1:GATE_SLOTS * h + 2], full) for h in range(2)]
        beta = [jnp.broadcast_to(gt[:, GATE_SLOTS * h:GATE_SLOTS * h + 1], full) for h in range(2)]
        eg = [jnp.exp(x) for x in gc]
        gcol_s[rows, :] = jnp.where(left, gc[0], gc[1])
        bcol = jnp.where(left, beta[0], beta[1])
        bcol_s[rows, :] = bcol
        begcol_s[rows, :] = bcol * jnp.where(left, eg[0], eg[1])
        gl = [chunk_last(x) for x in gc]
        ek = [jnp.exp(gl[h] - gc[h]) for h in range(2)]
        for h in range(2):
            gl_s[jobs, :, h * LANES:(h + 1) * LANES] = jnp.exp(chunked(gl[h])[:, 0:1, :])

        q = l2n(conv_silu(q_ref[g], g, 0, cq_ref[g])) * (DK ** -0.5)
        qk_s[jobs, 0:c, :] = chunked(q).astype(BF16)
        for h in range(2):
            wq_s[jobs, (2 * h + 1) * c:(2 * h + 2) * c, :] = chunked(q * eg[h]).astype(BF16)

        k = l2n(conv_silu(k_ref[g], g, 1, ck_ref[g]))
        k3 = chunked(k).astype(BF16)
        zero3 = jnp.zeros_like(k3)
        qk_s[jobs, c:2 * c, :] = k3
        kk_s[jobs, 0:c, :] = k3
        kk_s[jobs, c:2 * c, :] = k3
        for h in range(2):
            kbd_s[jobs, h * c:(h + 1) * c, h * LANES:(h + 1) * LANES] = k3
            kbd_s[jobs, h * c:(h + 1) * c, (1 - h) * LANES:(2 - h) * LANES] = zero3
            kd_s[jobs, h * c:(h + 1) * c, :] = chunked(k * ek[h]).astype(BF16)
            v3 = chunked(conv_silu(v_ref[2 * g + h], g, 2 + h, cv_ref[2 * g + h])).astype(BF16)
            vbd_s[jobs, h * c:(h + 1) * c, h * LANES:(h + 1) * LANES] = v3
            vbd_s[jobs, h * c:(h + 1) * c, (1 - h) * LANES:(2 - h) * LANES] = zero3

    r = lax.broadcasted_iota(jnp.int32, (c, LANES), 0)
    sloc = lax.broadcasted_iota(jnp.int32, (c, LANES), 1) & (c - 1)
    causal = r >= sloc
    strict = r > sloc
    diag = r == sloc

    def as_row(x):
        return jnp.sum(jnp.where(diag, x, 0.0), axis=0, keepdims=True)

    def intra(i, carry):
        js = [i * unroll + j for j in range(unroll)]
        sls = [pl.ds(pl.multiple_of(j * c, c), c) for j in js]
        gcol = [gcol_s[sl, :] for sl in sls]
        bcl = [bcol_s[sl, :] for sl in sls]
        begcol = [begcol_s[sl, :] for sl in sls]
        decay = [jnp.where(causal, jnp.exp(jnp.where(causal, x - as_row(x), 0.0)), 0.0) for x in gcol]
        qkk = [lax.dot_general(qk_s[j], kk_s[j], (((1,), (1,)), ((), ())), preferred_element_type=F32)
               for j in js]
        for j, x, d in zip(js, qkk, decay):
            a_s[j] = (x[:c] * d).astype(BF16)
        tm = _inv_unit_lower([jnp.where(strict, x[c:] * d * b, 0.0) for x, d, b in zip(qkk, decay, bcl)])
        w = [jnp.dot((x * as_row(gg)).astype(BF16), kbd_s[j], preferred_element_type=F32)
             for j, x, gg in zip(js, tm, begcol)]
        u = [jnp.dot((x * as_row(b)).astype(BF16), vbd_s[j], preferred_element_type=F32)
             for j, x, b in zip(js, tm, bcl)]
        for j, wn, un in zip(js, w, u):
            u_s[j] = un
            for h in range(2):
                wq_s[j, 2 * h * c:(2 * h + 1) * c, :] = wn[:, h * DK:(h + 1) * DK].astype(BF16)
        return carry

    lax.fori_loop(0, heads * n // unroll, intra, 0)

    zero_c = jnp.zeros((c, DV), F32)

    def scan(nn, carry):
        js = [g * n + nn for g in range(heads)]
        sp = [st_s[g] for g in range(heads)]
        r1 = [jnp.dot(wq_s[j], x.astype(BF16), preferred_element_type=F32) for j, x in zip(js, sp)]
        u = [u_s[j] for j in js]
        vnbd = [jnp.concatenate([jnp.concatenate([un[:, :DV] - x[0:c, :DV], zero_c], axis=1),
                                 jnp.concatenate([zero_c, un[:, DV:] - x[2 * c:3 * c, DV:]], axis=1)],
                                axis=0).astype(BF16) for un, x in zip(u, r1)]
        av = [jnp.dot(a_s[j], v, preferred_element_type=F32) for j, v in zip(js, vnbd)]
        upd = [lax.dot_general(kd_s[j], v, (((0,), (0,)), ((), ())), preferred_element_type=F32)
               for j, v in zip(js, vnbd)]
        for g, j in enumerate(js):
            sl = pl.ds(pl.multiple_of(j * c, c), c)
            o_s[sl, :] = jnp.concatenate([r1[g][c:2 * c, :DV], r1[g][3 * c:4 * c, DV:]], axis=1) + av[g]
            st_s[g] = sp[g] * gl_s[j] + upd[g]
        return carry

    lax.fori_loop(0, n, scan, 0)

    for g in range(heads):
        for h in range(2):
            hv = 2 * g + h
            s_ref[0, hv] = st_s[g, :, h * DV:(h + 1) * DV]
            o = o_s[g * tt:(g + 1) * tt, h * DV:(h + 1) * DV]
            zf = z_ref[hv]
            o = (o * lax.rsqrt(jnp.mean(o * o, axis=-1, keepdims=True) + EPS) * nw_ref[...]
                 * (zf * jax.nn.sigmoid(zf)))
            o_ref[:, hv * DV:(hv + 1) * DV] = o.astype(o_ref.dtype)


def _gdn_core_tiled(ph, hist, cw, gates, s0, nw, *, b, t, nk, nv, tt, heads, unroll):
    c = CHUNK
    assert t % tt == 0 and tt % c == 0 and nk % heads == 0 and nv == 2 * nk and DK == LANES and DV == LANES
    assert (heads * tt // c) % unroll == 0
    m = b * t
    nt = t // tt
    jobs = heads * tt // c
    vh = 2 * heads
    off_k, off_v, off_z = nk // heads, 2 * nk // vh, (2 * nk + nv) // vh
    seq = lambda off: (lambda bi, h, ti: (off + h, bi * nt + ti, 0))
    hst = lambda off: (lambda bi, h, ti: (off + h, bi, 0, 0))
    cwt = lambda off: (lambda bi, h, ti: (off + h, 0, 0))
    f32 = lambda *shape: pltpu.VMEM(shape, F32)
    bf16 = lambda *shape: pltpu.VMEM(shape, BF16)
    return pl.pallas_call(
        functools.partial(_gdn3_kernel, tt=tt, c=c, heads=heads, unroll=unroll),
        grid=(b, nk // heads, nt),
        in_specs=[
            pl.BlockSpec((heads, tt, LANES), seq(0)), pl.BlockSpec((heads, tt, LANES), seq(off_k)),
            pl.BlockSpec((vh, tt, LANES), seq(off_v)), pl.BlockSpec((vh, tt, LANES), seq(off_z)),
            pl.BlockSpec((heads, 1, HIST_ROWS, LANES), hst(0)), pl.BlockSpec((heads, 1, HIST_ROWS, LANES), hst(off_k)),
            pl.BlockSpec((vh, 1, HIST_ROWS, LANES), hst(off_v)),
            pl.BlockSpec((heads, HIST_ROWS, LANES), cwt(0)), pl.BlockSpec((heads, HIST_ROWS, LANES), cwt(off_k)),
            pl.BlockSpec((vh, HIST_ROWS, LANES), cwt(off_v)),
            pl.BlockSpec((tt, LANES), lambda bi, h, ti: (bi * nt + ti, 0)),
            pl.BlockSpec((1, vh, DK, DV), lambda bi, h, ti: (bi, h, 0, 0)),
            pl.BlockSpec((1, DV), lambda bi, h, ti: (0, 0)),
        ],
        out_specs=[
            pl.BlockSpec((tt, vh * DV), lambda bi, h, ti: (bi * nt + ti, h)),
            pl.BlockSpec((1, vh, DK, DV), lambda bi, h, ti: (bi, h, 0, 0)),
        ],
        out_shape=[
            jax.ShapeDtypeStruct((m, nv * DV), BF16),
            jax.ShapeDtypeStruct((b, nv, DK, DV), F32),
        ],
        scratch_shapes=[
            f32(tt + HIST_ROWS, LANES), f32(heads, 4, HIST_ROWS, LANES),
            f32(heads * tt, LANES), f32(heads * tt, LANES), f32(heads * tt, LANES), f32(jobs, 1, 2 * LANES),
            bf16(jobs, 2 * c, LANES), bf16(jobs, 2 * c, LANES), bf16(jobs, 2 * c, 2 * LANES),
            bf16(jobs, 2 * c, 2 * LANES), bf16(jobs, 2 * c, LANES), bf16(jobs, 4 * c, LANES),
            f32(jobs, c, 2 * DV), bf16(jobs, c, LANES), f32(heads * tt, 2 * DV), f32(heads, DK, 2 * DV),
        ],
        compiler_params=_cparams(("parallel", "parallel", "arbitrary")),
        name="gdn_core_tiled",
    )(ph, ph, ph, ph, hist, hist, hist, cw, cw, cw, gates, s0, nw)


def _gdn_kernel(q_ref, k_ref, v_ref, z_ref, hq_ref, hk_ref, hv_ref, cq_ref, ck_ref, cv_ref, g_ref, s0_ref,
                nw_ref, o_ref, s_ref, xc_s, q_s, k_s, v_s, g_s, w_s, u_s, qg_s, kd_s, a_s, o_s, st_s, *, t, c):
    n_chunks = t // c
    hv = pl.program_id(1)

    def conv_silu(x_ref, h_ref, cw_ref):
        xc_s[0:HIST_ROWS, :] = h_ref[0, 0]
        xc_s[HIST_ROWS:HIST_ROWS + t, :] = x_ref[0]
        cw = cw_ref[0]
        base = HIST_ROWS - (CONV_W - 1)
        acc = xc_s[base:base + t, :] * cw[0:1, :]
        for j in range(1, CONV_W):
            acc = acc + xc_s[base + j:base + j + t, :] * cw[j:j + 1, :]
        return acc * jax.nn.sigmoid(acc)

    def l2n(x):
        return x * lax.rsqrt(jnp.sum(x * x, axis=-1, keepdims=True) + EPS)

    q_s[...] = l2n(conv_silu(q_ref, hq_ref, cq_ref)) * (DK ** -0.5)
    k_s[...] = l2n(conv_silu(k_ref, hk_ref, ck_ref))
    v_s[...] = conv_silu(v_ref, hv_ref, cv_ref)
    g_s[...] = pltpu.roll(g_ref[...], (LANES - GATE_SLOTS * hv) % LANES, 1)
    st_s[...] = s0_ref[0, 0]

    r = lax.broadcasted_iota(jnp.int32, (c, c), 0)
    s = lax.broadcasted_iota(jnp.int32, (c, c), 1)
    causal = r >= s
    strict = r > s
    diag = r == s

    def intra(n, carry):
        row0 = pl.multiple_of(n * c, c)
        sl = pl.ds(row0, c)
        qc, kc, vc, gg = q_s[sl, :], k_s[sl, :], v_s[sl, :], g_s[sl, :]
        beta = gg[:, 0:1]
        gc = gg[:, 1:2]
        gc_row = jnp.sum(jnp.where(diag, gc, 0.0), axis=0, keepdims=True)
        decay = jnp.where(causal, jnp.exp(jnp.where(causal, gc - gc_row, 0.0)), 0.0)
        k16 = kc.astype(BF16)
        qkk = lax.dot_general(jnp.concatenate([qc.astype(BF16), k16], axis=0), k16,
                              (((1,), (1,)), ((), ())), preferred_element_type=F32)
        a_mat = qkk[:c] * decay
        l_mat = jnp.where(strict, qkk[c:] * decay * beta, 0.0)
        tm = _inv_unit_lower([l_mat])[0]
        eg = jnp.exp(gc)
        wu = _mm(tm, jnp.concatenate([kc * (beta * eg), vc * beta], axis=1))
        w_s[sl, :] = wu[:, :DK]
        u_s[sl, :] = wu[:, DK:]
        qg_s[sl, :] = qc * eg
        kd_s[sl, :] = kc * jnp.exp(gc[c - 1:c, :] - gc)
        a_s[n] = a_mat
        return carry

    def scan(n, carry):
        row0 = pl.multiple_of(n * c, c)
        sl = pl.ds(row0, c)
        st = st_s[...]
        wq = _mm(jnp.concatenate([w_s[sl, :], qg_s[sl, :]], axis=0), st)
        v_new = (u_s[sl, :] - wq[:c]).astype(BF16)
        o_s[sl, :] = wq[c:] + _mm(a_s[n], v_new)
        gl = jnp.exp(g_s[pl.ds(row0 + c - 1, 1), 1:2])
        st_s[...] = st * gl + lax.dot_general(kd_s[sl, :].astype(BF16), v_new, (((0,), (0,)), ((), ())),
                                              preferred_element_type=F32)
        return carry

    if n_chunks == 1:
        intra(0, 0)
        scan(0, 0)
    else:
        lax.fori_loop(0, n_chunks, intra, 0)
        lax.fori_loop(0, n_chunks, scan, 0)

    s_ref[0, 0] = st_s[...]
    o = o_s[...]
    zf = z_ref[0]
    o = o * lax.rsqrt(jnp.mean(o * o, axis=-1, keepdims=True) + EPS) * nw_ref[...] * (zf * jax.nn.sigmoid(zf))
    o_ref[...] = o.astype(o_ref.dtype)


def _gdn_core(ph, hist, cw, gates, s0, nw, *, b, t, nk, nv):
    c = min(CHUNK, t)
    assert t % c == 0 and c % INV_BLOCK == 0
    rep = nv // nk
    m = b * t
    n_chunks = t // c
    row = lambda off: (lambda bi, h: (off + h, bi, 0))
    rowk = lambda off: (lambda bi, h: (off + h // rep, bi, 0))
    hrow = lambda off: (lambda bi, h: (off + h, bi, 0, 0))
    hrowk = lambda off: (lambda bi, h: (off + h // rep, bi, 0, 0))
    crow = lambda off: (lambda bi, h: (off + h, 0, 0))
    crowk = lambda off: (lambda bi, h: (off + h // rep, 0, 0))
    blk = (1, t, LANES)
    hblk = (1, 1, HIST_ROWS, LANES)
    cblk = (1, HIST_ROWS, LANES)
    big = pltpu.VMEM((t, LANES), F32)
    return pl.pallas_call(
        functools.partial(_gdn_kernel, t=t, c=c),
        grid=(b, nv),
        in_specs=[
            pl.BlockSpec(blk, rowk(0)), pl.BlockSpec(blk, rowk(nk)),
            pl.BlockSpec(blk, row(2 * nk)), pl.BlockSpec(blk, row(2 * nk + nv)),
            pl.BlockSpec(hblk, hrowk(0)), pl.BlockSpec(hblk, hrowk(nk)), pl.BlockSpec(hblk, hrow(2 * nk)),
            pl.BlockSpec(cblk, crowk(0)), pl.BlockSpec(cblk, crowk(nk)), pl.BlockSpec(cblk, crow(2 * nk)),
            pl.BlockSpec((t, LANES), lambda bi, h: (bi, 0)),
            pl.BlockSpec((1, 1, DK, DV), lambda bi, h: (bi, h, 0, 0)),
            pl.BlockSpec((1, DV), lambda bi, h: (0, 0)),
        ],
        out_specs=[
            pl.BlockSpec((t, DV), lambda bi, h: (bi, h)),
            pl.BlockSpec((1, 1, DK, DV), lambda bi, h: (bi, h, 0, 0)),
        ],
        out_shape=[
            jax.ShapeDtypeStruct((m, nv * DV), BF16),
            jax.ShapeDtypeStruct((b, nv, DK, DV), F32),
        ],
        scratch_shapes=[
            pltpu.VMEM((t + HIST_ROWS, LANES), F32),
            big, big, big, big, big, big, big, big,
            pltpu.VMEM((n_chunks, c, c), F32),
            big,
            pltpu.VMEM((DK, DV), F32),
        ],
        compiler_params=_cparams(("parallel", "arbitrary")),
        name="gdn_core",
    )(ph, ph, ph, ph, hist, hist, hist, cw, cw, cw, gates, s0, nw)


def _mm_res_kernel(a_ref, w_ref, r_ref, o_ref):
    o_ref[...] = r_ref[...] + jnp.dot(a_ref[...], w_ref[...], preferred_element_type=F32)


def _mm_residual(a, w, res, *, tm, tn):
    m, k = a.shape
    n = w.shape[1]
    return pl.pallas_call(
        _mm_res_kernel,
        grid=(m // tm, n // tn),
        in_specs=[
            pl.BlockSpec((tm, k), lambda i, j: (i, 0)),
            pl.BlockSpec((k, tn), lambda i, j: (0, j)),
            pl.BlockSpec((tm, tn), lambda i, j: (i, j)),
        ],
        out_specs=pl.BlockSpec((tm, tn), lambda i, j: (i, j)),
        out_shape=jax.ShapeDtypeStruct((m, n), F32),
        compiler_params=_cparams(("parallel", "arbitrary")),
        name="mm_residual",
    )(a, w, res)


def _ffn_kernel(x_ref, nw_ref, wg_ref, wu_ref, wd_ref, fw_ref, o_ref, h_ref, *, final):
    j = pl.program_id(1)

    @pl.when(j == 0)
    def _():
        x = x_ref[...]
        h_ref[...] = _rms(x, nw_ref[...]).astype(BF16)
        o_ref[...] = x

    h = h_ref[...]
    g = jnp.dot(h, wg_ref[...], preferred_element_type=F32)
    u = jnp.dot(h, wu_ref[...], preferred_element_type=F32)
    act = (g * jax.nn.sigmoid(g) * u).astype(BF16)
    o_ref[...] += jnp.dot(act, wd_ref[...], preferred_element_type=F32)

    if final:
        @pl.when(j == pl.num_programs(1) - 1)
        def _():
            o_ref[...] = _rms(o_ref[...], fw_ref[...])


def _ffn(x, nw, wgu, wd, fw, *, tm, tf, final):
    m, d = x.shape
    f = wd.shape[0]
    nf = f // tf
    return pl.pallas_call(
        functools.partial(_ffn_kernel, final=final),
        grid=(m // tm, nf),
        in_specs=[
            pl.BlockSpec((tm, d), lambda i, j: (i, 0)),
            pl.BlockSpec((1, d), lambda i, j: (0, 0)),
            pl.BlockSpec((d, tf), lambda i, j: (0, j)),
            pl.BlockSpec((d, tf), lambda i, j: (0, nf + j)),
            pl.BlockSpec((tf, d), lambda i, j: (j, 0)),
            pl.BlockSpec((1, d), lambda i, j: (0, 0)),
        ],
        out_specs=pl.BlockSpec((tm, d), lambda i, j: (i, 0)),
        out_shape=jax.ShapeDtypeStruct((m, d), F32),
        scratch_shapes=[pltpu.VMEM((tm, d), BF16)],
        compiler_params=_cparams(("parallel", "arbitrary")),
        name="ffn",
    )(x, nw, wgu, wgu, wd, fw)


def _pool_kernel(x_ref, nw_ref, hist_ref, pw_ref, sc_ref, o_ref, hs_ref, hbuf, *, tt, past):
    ti = pl.program_id(1)

    @pl.when(ti == 0)
    def _():
        hbuf[0:POOL_PAD, :] = hist_ref[0]

    @pl.when(ti > 0)
    def _():
        hbuf[0:POOL_PAD, :] = hbuf[tt:tt + POOL_PAD, :]

    x = x_ref[...]
    h = _rms(x, nw_ref[...])
    hbuf[POOL_PAD:POOL_PAD + tt, :] = h
    tg = ti * tt + lax.broadcasted_iota(jnp.int32, (tt, 1), 0)
    gcw = x.shape[1] // len(POOL_WINDOWS)
    for gi, wlen in enumerate(POOL_WINDOWS):
        cols = slice(gi * gcw, (gi + 1) * gcw)
        acc = h[:, cols]
        for i in range(1, wlen):
            acc = acc + hbuf[POOL_PAD - i:POOL_PAD - i + tt, cols]
        cnt = jnp.minimum(wlen, past + tg + 1).astype(F32)
        dlt = acc / cnt - h[:, cols]
        y = jnp.dot(dlt.astype(BF16), pw_ref[gi], preferred_element_type=F32)
        o_ref[:, cols] = x[:, cols] + y * sc_ref[:, cols]
    hs_ref[0] = hbuf[tt:tt + POOL_PAD, :]


def _pool_mixer(x, nw, hist, pw, sc, *, b, t, tt, past):
    m, d = x.shape
    nt = t // tt
    g, gcw, _ = pw.shape
    return pl.pallas_call(
        functools.partial(_pool_kernel, tt=tt, past=past),
        grid=(b, nt),
        in_specs=[
            pl.BlockSpec((tt, d), lambda bi, ti: (bi * nt + ti, 0)),
            pl.BlockSpec((1, d), lambda bi, ti: (0, 0)),
            pl.BlockSpec((1, POOL_PAD, d), lambda bi, ti: (bi, 0, 0)),
            pl.BlockSpec((g, gcw, gcw), lambda bi, ti: (0, 0, 0)),
            pl.BlockSpec((1, d), lambda bi, ti: (0, 0)),
        ],
        out_specs=[
            pl.BlockSpec((tt, d), lambda bi, ti: (bi * nt + ti, 0)),
            pl.BlockSpec((1, POOL_PAD, d), lambda bi, ti: (bi, 0, 0)),
        ],
        out_shape=[
            jax.ShapeDtypeStruct((m, d), F32),
            jax.ShapeDtypeStruct((b, POOL_PAD, d), F32),
        ],
        scratch_shapes=[pltpu.VMEM((tt + POOL_PAD, d), F32)],
        compiler_params=_cparams(("parallel", "arbitrary")),
        name="pool_mixer",
    )(x, nw, hist, pw, sc)


def _prep_weights(norm_mix_w, norm_ffn_w, final_norm_w, gdn_w_in, gdn_conv_w, gdn_A_log, gdn_dt_bias, gdn_norm_w,
                  gdn_w_out, pool_w, pool_scale, ffn_w_gu, ffn_w_down):
    n_a, d, _ = gdn_w_in.shape
    nv = gdn_A_log.shape[1]
    val_dim = nv * DV
    qkv_dim = gdn_conv_w.shape[2]
    main = qkv_dim + val_dim
    zeros = jnp.zeros((n_a, d, nv), F32)
    w_gate = jnp.stack([gdn_w_in[:, :, main:main + nv], gdn_w_in[:, :, main + nv:main + 2 * nv], zeros, zeros],
                       axis=-1).reshape(n_a, d, nv * GATE_SLOTS).astype(BF16)
    zv = jnp.zeros((n_a, nv), F32)
    alog = jnp.stack([zv, gdn_A_log.astype(F32), zv, zv], axis=-1).reshape(n_a, 1, nv * GATE_SLOTS)
    dtb = jnp.stack([zv, gdn_dt_bias.astype(F32), zv, zv], axis=-1).reshape(n_a, 1, nv * GATE_SLOTS)
    nh = qkv_dim // LANES
    cw = gdn_conv_w.reshape(n_a, CONV_W, nh, LANES).transpose(0, 2, 1, 3)
    cw = jnp.pad(cw, ((0, 0), (0, 0), (0, HIST_ROWS - CONV_W), (0, 0)))
    return dict(
        norm_mix=norm_mix_w[:, None, :], norm_ffn=norm_ffn_w[:, None, :], final=final_norm_w[None, :],
        w_in=gdn_w_in.astype(BF16), n_main=main, w_gate=w_gate, alog=alog, dtb=dtb, cw=cw,
        gnorm=gdn_norm_w[:, None, :], w_out=gdn_w_out.astype(BF16), pool_w=pool_w.astype(BF16),
        pool_scale=pool_scale[:, None, :], w_gu=ffn_w_gu.astype(BF16), w_down=ffn_w_down.astype(BF16),
    )


def _trunk(x3, conv_hist, s_hist, pool_hist, wts, depth):
    b, t, d = x3.shape
    m = b * t
    x = x3.reshape(m, d)
    nv = s_hist.shape[2]
    nk = nv // 2
    qkv_dim = conv_hist.shape[-1]
    nh = qkv_dim // LANES
    c = min(CHUNK, t)
    tm = min(512, m)
    past = pool_hist.shape[2]
    assert t >= CONV_W - 1 and t >= POOL_PAD - 1
    new_conv, new_s, new_pool = [], [], []
    for i in range(depth):
        j = i // 2
        if i % 2 == 0:
            ph, ba = _norm_proj(x, wts["norm_mix"][i], wts["w_in"][j], wts["w_gate"][j], n=wts["n_main"],
                                tm=min(1024, m), tn=512)
            gates = _gates(ba, wts["alog"][j], wts["dtb"][j], c=c, tt=min(m, 2048))
            hist = conv_hist[j].reshape(b, CONV_W - 1, nh, LANES).transpose(2, 0, 1, 3)
            hist = jnp.pad(hist, ((0, 0), (0, 0), (HIST_ROWS - (CONV_W - 1), 0), (0, 0)))
            if t % GDN_TILE == 0:
                o, s_new = _gdn_core_tiled(ph, hist, wts["cw"][j], gates, s_hist[j].astype(F32), wts["gnorm"][j],
                                           b=b, t=t, nk=nk, nv=nv, tt=GDN_TILE, heads=GDN_HEADS,
                                           unroll=GDN_UNROLL)
            else:
                o, s_new = _gdn_core(ph, hist, wts["cw"][j], gates, s_hist[j].astype(F32), wts["gnorm"][j],
                                     b=b, t=t, nk=nk, nv=nv)
            x = _mm_residual(o, wts["w_out"][j], x, tm=tm, tn=512)
            tail = ph[:nh].reshape(nh, b, t, LANES)[:, :, t - (CONV_W - 1):, :]
            new_conv.append(tail.transpose(1, 2, 0, 3).reshape(b, CONV_W - 1, qkv_dim))
            new_s.append(s_new)
        else:
            hist = pool_hist[j]
            hist = jnp.pad(hist, ((0, 0), (POOL_PAD - past, 0), (0, 0)))
            x, hs = _pool_mixer(x, wts["norm_mix"][i], hist, wts["pool_w"][j], wts["pool_scale"][j],
                                b=b, t=t, tt=min(t, 256), past=past)
            new_pool.append(hs[:, POOL_PAD - (POOL_PAD - 1):, :])
        x = _ffn(x, wts["norm_ffn"][i], wts["w_gu"][i], wts["w_down"][i], wts["final"], tm=tm, tf=512,
                 final=(i == depth - 1))
    return x.reshape(b, t, d), jnp.stack(new_conv), jnp.stack(new_s), jnp.stack(new_pool)


def kernel(x_prompt, x_sample, state_gdn_conv, state_gdn_S, state_pool, norm_mix_w, norm_ffn_w, final_norm_w,
           gdn_w_in, gdn_conv_w, gdn_A_log, gdn_dt_bias, gdn_norm_w, gdn_w_out, pool_w, pool_scale, ffn_w_gu,
           ffn_w_down):
    depth = norm_mix_w.shape[0]
    n_a, n_b = state_gdn_conv.shape[0], state_pool.shape[0]
    bp = x_prompt.shape[0]
    wts = _prep_weights(norm_mix_w, norm_ffn_w, final_norm_w, gdn_w_in, gdn_conv_w, gdn_A_log, gdn_dt_bias,
                        gdn_norm_w, gdn_w_out, pool_w, pool_scale, ffn_w_gu, ffn_w_down)
    conv0 = jnp.zeros((n_a, bp) + state_gdn_conv.shape[2:], x_prompt.dtype)
    s00 = jnp.zeros((n_a, bp) + state_gdn_S.shape[2:], F32)
    pool0 = jnp.zeros((n_b, bp, 0, x_prompt.shape[-1]), x_prompt.dtype)
    y_p, conv_p, s_p, pool_p = _trunk(x_prompt, conv0, s00, pool0, wts, depth)
    y_s, conv_s, s_s, pool_s = _trunk(x_sample, state_gdn_conv, state_gdn_S, state_pool, wts, depth)
    return (y_p, y_s, conv_p, s_p.astype(state_gdn_S.dtype), pool_p, conv_s, s_s.astype(state_gdn_S.dtype), pool_s)
```

```python
import functools

import jax
import jax.numpy as jnp
from jax import lax
from jax.experimental import pallas as pl
from jax.experimental.pallas import tpu as pltpu

F32 = jnp.float32
BF16 = jnp.bfloat16

EPS = 1e-6
LANES = 128
DK = 128
DV = 128
CONV_W = 4
HIST_ROWS = 8
CHUNK = 64
INV_BLOCK = 16
GDN_TILE = 512
GDN_HEADS = 4
GDN_UNROLL = 16
POOL_WINDOWS = (2, 4, 8, 16)
POOL_PAD = 16
GATE_SLOTS = 4
VMEM_LIMIT = 56 * 1024 * 1024


def _cparams(sem):
    return pltpu.CompilerParams(dimension_semantics=sem, vmem_limit_bytes=VMEM_LIMIT)


def _rms(x, w):
    ms = jnp.mean(x * x, axis=-1, keepdims=True)
    return x * lax.rsqrt(ms + EPS) * w


def _mm(a, b):
    return jnp.dot(a.astype(BF16), b.astype(BF16), preferred_element_type=F32)


def _norm_proj_kernel(x_ref, nw_ref, w_ref, wg_ref, o_ref, g_ref, h_ref, *, tn):
    @pl.when(pl.program_id(1) == 0)
    def _():
        h = _rms(x_ref[...], nw_ref[...]).astype(BF16)
        h_ref[...] = h
        g_ref[...] = jnp.dot(h, wg_ref[...], preferred_element_type=F32)

    acc = jnp.dot(h_ref[...], w_ref[...], preferred_element_type=F32)
    for c in range(tn // LANES):
        o_ref[c] = acc[:, c * LANES:(c + 1) * LANES]


def _norm_proj(x, nw, w, wg, *, n, tm, tn):
    m, d = x.shape
    assert n % tn == 0 and n <= w.shape[1] and m % tm == 0
    return pl.pallas_call(
        functools.partial(_norm_proj_kernel, tn=tn),
        grid=(m // tm, n // tn),
        in_specs=[
            pl.BlockSpec((tm, d), lambda i, j: (i, 0)),
            pl.BlockSpec((1, d), lambda i, j: (0, 0)),
            pl.BlockSpec((d, tn), lambda i, j: (0, j)),
            pl.BlockSpec((d, LANES), lambda i, j: (0, 0)),
        ],
        out_specs=[
            pl.BlockSpec((tn // LANES, tm, LANES), lambda i, j: (j, i, 0)),
            pl.BlockSpec((tm, LANES), lambda i, j: (i, 0)),
        ],
        out_shape=[
            jax.ShapeDtypeStruct((n // LANES, m, LANES), F32),
            jax.ShapeDtypeStruct((m, LANES), F32),
        ],
        scratch_shapes=[pltpu.VMEM((tm, d), BF16)],
        compiler_params=_cparams(("parallel", "arbitrary")),
        name="norm_proj",
    )(x, nw, w, wg)


def _gates_kernel(ba_ref, alog_ref, dtb_ref, tri_ref, o_ref, *, c, valid, period):
    x = ba_ref[...]
    beta = jax.nn.sigmoid(x)
    y = x + dtb_ref[...]
    softplus = jnp.maximum(y, 0.0) + jnp.log1p(jnp.exp(-jnp.abs(y)))
    g = -jnp.exp(alog_ref[...]) * softplus
    if valid != period:
        assert period & (period - 1) == 0 and x.shape[0] % period == 0
        row = lax.broadcasted_iota(jnp.int32, x.shape, 0) & (period - 1)
        beta = jnp.where(row < valid, beta, 0.0)
        g = jnp.where(row < valid, g, 0.0)
    lane = lax.broadcasted_iota(jnp.int32, (c, x.shape[1]), 1)
    is_beta = (lane & (GATE_SLOTS - 1)) == 0
    tri = tri_ref[...]
    for n in range(x.shape[0] // c):
        sl = slice(n * c, (n + 1) * c)
        gc = jnp.dot(tri, g[sl], preferred_element_type=F32, precision=lax.Precision.HIGHEST)
        o_ref[sl, :] = jnp.where(is_beta, beta[sl], gc)


def _gates(ba, alog, dtb, *, c, tt, valid, period):
    m = ba.shape[0]
    tri = jnp.tril(jnp.ones((c, c), F32))
    return pl.pallas_call(
        functools.partial(_gates_kernel, c=c, valid=valid, period=period),
        grid=(m // tt,),
        in_specs=[
            pl.BlockSpec((tt, LANES), lambda i: (i, 0)),
            pl.BlockSpec((1, LANES), lambda i: (0, 0)),
            pl.BlockSpec((1, LANES), lambda i: (0, 0)),
            pl.BlockSpec((c, c), lambda i: (0, 0)),
        ],
        out_specs=pl.BlockSpec((tt, LANES), lambda i: (i, 0)),
        out_shape=jax.ShapeDtypeStruct((m, LANES), F32),
        compiler_params=_cparams(("parallel",)),
        name="gates",
    )(ba, alog, dtb, tri)


def _inv_unit_lower(l_mats):
    c, w = l_mats[0].shape
    npack = w // c
    r = lax.broadcasted_iota(jnp.int32, (c, w), 0)
    lane = lax.broadcasted_iota(jnp.int32, (c, w), 1)
    s = lane & (c - 1)
    shift = INV_BLOCK.bit_length() - 1
    same = lax.shift_right_logical(r, shift) == lax.shift_right_logical(s, shift)
    eye = jnp.where(r == s, 1.0, 0.0).astype(F32)
    part = [lax.shift_right_logical(lane, c.bit_length() - 1) == p for p in range(npack)]

    def mm(xs, ys):
        if npack > 1:
            ys = [jnp.concatenate([jnp.where(part[p], y, 0.0) for p in range(npack)], axis=0) for y in ys]
        return [_mm(x, y) for x, y in zip(xs, ys)]

    def stack(xs, ys):
        return [jnp.concatenate([x, y], axis=0) for x, y in zip(xs, ys)]

    d1 = [jnp.where(same, l, 0.0) for l in l_mats]
    d2 = mm(d1, d1)
    t = mm(stack(d2, d1), d2)
    d4 = [x[:c] for x in t]
    y = [eye - a + b - x[c:] for a, b, x in zip(d1, d2, t)]
    t = mm(stack(d4, y), d4)
    y = [a + x[c:] for a, x in zip(y, t)]
    x0 = [a + b for a, b in zip(y, mm(y, [x[:c] for x in t]))]
    nblk = c // INV_BLOCK
    if nblk == 1:
        return x0
    e = [jnp.where(same, 0.0, l) for l in l_mats]
    nmat = mm(x0, e)
    z = [eye - a for a in nmat]
    if nblk > 2:
        assert nblk == 4
        n2 = mm(nmat, nmat)
        n3 = mm(nmat, n2)
        z = [a + b - x for a, b, x in zip(z, n2, n3)]
    return mm(z, x0)


def _gdn_kernel(q_ref, k_ref, v_ref, z_ref, hq_ref, hk_ref, hv_ref, cq_ref, ck_ref, cv_ref, g_ref, s0_ref,
                nw_ref, o_ref, s_ref, xc_s, hist_s, gcol_s, bcol_s, begcol_s, gl_s, qk_s, kk_s, kbd_s, vbd_s,
                kd_s, wq_s, u_s, a_s, o_s, st_s, *, tt, c, heads, unroll):
    n = tt // c
    hg = pl.program_id(1)
    half = LANES // 2
    assert c == half

    @pl.when(pl.program_id(2) == 0)
    def _():
        for g in range(heads):
            hist_s[g, 0] = hq_ref[g, 0]
            hist_s[g, 1] = hk_ref[g, 0]
            for h in range(2):
                hist_s[g, 2 + h] = hv_ref[2 * g + h, 0]
                st_s[g, :, h * DV:(h + 1) * DV] = s0_ref[0, 2 * g + h]

    def conv_silu(x, g, slot, cw):
        xc_s[0:HIST_ROWS, :] = hist_s[g, slot]
        xc_s[HIST_ROWS:HIST_ROWS + tt, :] = x
        hist_s[g, slot] = xc_s[tt:tt + HIST_ROWS, :]
        base = HIST_ROWS - (CONV_W - 1)
        acc = xc_s[base:base + tt, :] * cw[0:1, :]
        for j in range(1, CONV_W):
            acc = acc + xc_s[base + j:base + j + tt, :] * cw[j:j + 1, :]
        return acc * jax.nn.sigmoid(acc)

    def l2n(x):
        return x * lax.rsqrt(jnp.sum(x * x, axis=-1, keepdims=True) + EPS)

    def chunked(x):
        return x.reshape(n, c, x.shape[-1])

    def chunk_last(x):
        x3 = chunked(x)
        return jnp.broadcast_to(x3[:, c - 1:c, :], x3.shape).reshape(x.shape)

    left = lax.broadcasted_iota(jnp.int32, (tt, LANES), 1) < half
    full = (tt, LANES)
    for g in range(heads):
        rows = slice(g * tt, (g + 1) * tt)
        jobs = slice(g * n, (g + 1) * n)
        shift = (LANES - 2 * GATE_SLOTS * (hg * heads + g)) & (LANES - 1)
        gt = pltpu.roll(g_ref[...], shift, 1)
        gc = [jnp.broadcast_to(gt[:, GATE_SLOTS * h + 1:GATE_SLOTS * h + 2], full) for h in range(2)]
        beta = [jnp.broadcast_to(gt[:, GATE_SLOTS * h:GATE_SLOTS * h + 1], full) for h in range(2)]
        eg = [jnp.exp(x) for x in gc]
        gcol_s[rows, :] = jnp.where(left, gc[0], gc[1])
        bcol = jnp.where(left, beta[0], beta[1])
        bcol_s[rows, :] = bcol
        begcol_s[rows, :] = bcol * jnp.where(left, eg[0], eg[1])
        gl = [chunk_last(x) for x in gc]
        ek = [jnp.exp(gl[h] - gc[h]) for h in range(2)]
        for h in range(2):
            gl_s[jobs, :, h * LANES:(h + 1) * LANES] = jnp.exp(chunked(gl[h])[:, 0:1, :])

        q = l2n(conv_silu(q_ref[g], g, 0, cq_ref[g])) * (DK ** -0.5)
        qk_s[jobs, 0:c, :] = chunked(q).astype(BF16)
        for h in range(2):
            wq_s[jobs, (2 * h + 1) * c:(2 * h + 2) * c, :] = chunked(q * eg[h]).astype(BF16)

        k = l2n(conv_silu(k_ref[g], g, 1, ck_ref[g]))
        k3 = chunked(k).astype(BF16)
        zero3 = jnp.zeros_like(k3)
        qk_s[jobs, c:2 * c, :] = k3
        kk_s[jobs, 0:c, :] = k3
        kk_s[jobs, c:2 * c, :] = k3
        for h in range(2):
            kbd_s[jobs, h * c:(h + 1) * c, h * LANES:(h + 1) * LANES] = k3
            kbd_s[jobs, h * c:(h + 1) * c, (1 - h) * LANES:(2 - h) * LANES] = zero3
            kd_s[jobs, h * c:(h + 1) * c, :] = chunked(k * ek[h]).astype(BF16)
            v3 = chunked(conv_silu(v_ref[2 * g + h], g, 2 + h, cv_ref[2 * g + h])).astype(BF16)
            vbd_s[jobs, h * c:(h + 1) * c, h * LANES:(h + 1) * LANES] = v3
            vbd_s[jobs, h * c:(h + 1) * c, (1 - h) * LANES:(2 - h) * LANES] = zero3

    r = lax.broadcasted_iota(jnp.int32, (c, LANES), 0)
    sloc = lax.broadcasted_iota(jnp.int32, (c, LANES), 1) & (c - 1)
    causal = r >= sloc
    strict = r > sloc
    diag = r == sloc

    def as_row(x):
        return jnp.sum(jnp.where(diag, x, 0.0), axis=0, keepdims=True)

    def intra(i, carry):
        js = [i * unroll + j for j in range(unroll)]
        sls = [pl.ds(pl.multiple_of(j * c, c), c) for j in js]
        gcol = [gcol_s[sl, :] for sl in sls]
        bcl = [bcol_s[sl, :] for sl in sls]
        begcol = [begcol_s[sl, :] for sl in sls]
        decay = [jnp.where(causal, jnp.exp(jnp.where(causal, x - as_row(x), 0.0)), 0.0) for x in gcol]
        qkk = [lax.dot_general(qk_s[j], kk_s[j], (((1,), (1,)), ((), ())), preferred_element_type=F32)
               for j in js]
        for j, x, d in zip(js, qkk, decay):
            a_s[j] = (x[:c] * d).astype(BF16)
        tm = _inv_unit_lower([jnp.where(strict, x[c:] * d * b, 0.0) for x, d, b in zip(qkk, decay, bcl)])
        w = [jnp.dot((x * as_row(gg)).astype(BF16), kbd_s[j], preferred_element_type=F32)
             for j, x, gg in zip(js, tm, begcol)]
        u = [jnp.dot((x * as_row(b)).astype(BF16), vbd_s[j], preferred_element_type=F32)
             for j, x, b in zip(js, tm, bcl)]
        for j, wn, un in zip(js, w, u):
            u_s[j] = un
            for h in range(2):
                wq_s[j, 2 * h * c:(2 * h + 1) * c, :] = wn[:, h * DK:(h + 1) * DK].astype(BF16)
        return carry

    if heads * n == unroll:
        intra(0, 0)
    else:
        lax.fori_loop(0, heads * n // unroll, intra, 0)

    zero_c = jnp.zeros((c, DV), F32)

    def scan(nn, carry):
        js = [g * n + nn for g in range(heads)]
        sp = [st_s[g] for g in range(heads)]
        r1 = [jnp.dot(wq_s[j], x.astype(BF16), preferred_element_type=F32) for j, x in zip(js, sp)]
        u = [u_s[j] for j in js]
        vnbd = [jnp.concatenate([jnp.concatenate([un[:, :DV] - x[0:c, :DV], zero_c], axis=1),
                                 jnp.concatenate([zero_c, un[:, DV:] - x[2 * c:3 * c, DV:]], axis=1)],
                                axis=0).astype(BF16) for un, x in zip(u, r1)]
        av = [jnp.dot(a_s[j], v, preferred_element_type=F32) for j, v in zip(js, vnbd)]
        upd = [lax.dot_general(kd_s[j], v, (((0,), (0,)), ((), ())), preferred_element_type=F32)
               for j, v in zip(js, vnbd)]
        for g, j in enumerate(js):
            sl = pl.ds(pl.multiple_of(j * c, c), c)
            o_s[sl, :] = jnp.concatenate([r1[g][c:2 * c, :DV], r1[g][3 * c:4 * c, DV:]], axis=1) + av[g]
            st_s[g] = sp[g] * gl_s[j] + upd[g]
        return carry

    if n == 1:
        scan(0, 0)
    else:
        lax.fori_loop(0, n, scan, 0)

    for g in range(heads):
        for h in range(2):
            hv = 2 * g + h
            s_ref[0, hv] = st_s[g, :, h * DV:(h + 1) * DV]
            o = o_s[g * tt:(g + 1) * tt, h * DV:(h + 1) * DV]
            zf = z_ref[hv]
            o = (o * lax.rsqrt(jnp.mean(o * o, axis=-1, keepdims=True) + EPS) * nw_ref[...]
                 * (zf * jax.nn.sigmoid(zf)))
            o_ref[:, hv * DV:(hv + 1) * DV] = o.astype(o_ref.dtype)


def _gdn_core(ph, hist, cw, gates, s0, nw, *, b, t, nk, nv, tt, heads, unroll):
    c = CHUNK
    assert t % tt == 0 and tt % c == 0 and nk % heads == 0 and nv == 2 * nk and DK == LANES and DV == LANES
    assert (heads * tt // c) % unroll == 0
    m = b * t
    nt = t // tt
    jobs = heads * tt // c
    vh = 2 * heads
    off_k, off_v, off_z = nk // heads, 2 * nk // vh, (2 * nk + nv) // vh
    seq = lambda off: (lambda bi, h, ti: (off + h, bi * nt + ti, 0))
    hst = lambda off: (lambda bi, h, ti: (off + h, bi, 0, 0))
    cwt = lambda off: (lambda bi, h, ti: (off + h, 0, 0))
    f32 = lambda *shape: pltpu.VMEM(shape, F32)
    bf16 = lambda *shape: pltpu.VMEM(shape, BF16)
    return pl.pallas_call(
        functools.partial(_gdn_kernel, tt=tt, c=c, heads=heads, unroll=unroll),
        grid=(b, nk // heads, nt),
        in_specs=[
            pl.BlockSpec((heads, tt, LANES), seq(0)), pl.BlockSpec((heads, tt, LANES), seq(off_k)),
            pl.BlockSpec((vh, tt, LANES), seq(off_v)), pl.BlockSpec((vh, tt, LANES), seq(off_z)),
            pl.BlockSpec((heads, 1, HIST_ROWS, LANES), hst(0)), pl.BlockSpec((heads, 1, HIST_ROWS, LANES), hst(off_k)),
            pl.BlockSpec((vh, 1, HIST_ROWS, LANES), hst(off_v)),
            pl.BlockSpec((heads, HIST_ROWS, LANES), cwt(0)), pl.BlockSpec((heads, HIST_ROWS, LANES), cwt(off_k)),
            pl.BlockSpec((vh, HIST_ROWS, LANES), cwt(off_v)),
            pl.BlockSpec((tt, LANES), lambda bi, h, ti: (bi * nt + ti, 0)),
            pl.BlockSpec((1, vh, DK, DV), lambda bi, h, ti: (bi, h, 0, 0)),
            pl.BlockSpec((1, DV), lambda bi, h, ti: (0, 0)),
        ],
        out_specs=[
            pl.BlockSpec((tt, vh * DV), lambda bi, h, ti: (bi * nt + ti, h)),
            pl.BlockSpec((1, vh, DK, DV), lambda bi, h, ti: (bi, h, 0, 0)),
        ],
        out_shape=[
            jax.ShapeDtypeStruct((m, nv * DV), BF16),
            jax.ShapeDtypeStruct((b, nv, DK, DV), F32),
        ],
        scratch_shapes=[
            f32(tt + HIST_ROWS, LANES), f32(heads, 4, HIST_ROWS, LANES),
            f32(heads * tt, LANES), f32(heads * tt, LANES), f32(heads * tt, LANES), f32(jobs, 1, 2 * LANES),
            bf16(jobs, 2 * c, LANES), bf16(jobs, 2 * c, LANES), bf16(jobs, 2 * c, 2 * LANES),
            bf16(jobs, 2 * c, 2 * LANES), bf16(jobs, 2 * c, LANES), bf16(jobs, 4 * c, LANES),
            f32(jobs, c, 2 * DV), bf16(jobs, c, LANES), f32(heads * tt, 2 * DV), f32(heads, DK, 2 * DV),
        ],
        compiler_params=_cparams(("parallel", "parallel", "arbitrary")),
        name="gdn_core",
    )(ph, ph, ph, ph, hist, hist, hist, cw, cw, cw, gates, s0, nw)


def _mm_res_kernel(a_ref, w_ref, r_ref, o_ref):
    o_ref[...] = r_ref[...] + jnp.dot(a_ref[...], w_ref[...], preferred_element_type=F32)


def _mm_residual(a, w, res, *, tm, tn):
    m, k = a.shape
    n = w.shape[1]
    return pl.pallas_call(
        _mm_res_kernel,
        grid=(m // tm, n // tn),
        in_specs=[
            pl.BlockSpec((tm, k), lambda i, j: (i, 0)),
            pl.BlockSpec((k, tn), lambda i, j: (0, j)),
            pl.BlockSpec((tm, tn), lambda i, j: (i, j)),
        ],
        out_specs=pl.BlockSpec((tm, tn), lambda i, j: (i, j)),
        out_shape=jax.ShapeDtypeStruct((m, n), F32),
        compiler_params=_cparams(("parallel", "arbitrary")),
        name="mm_residual",
    )(a, w, res)


def _ffn_kernel(x_ref, nw_ref, wg_ref, wu_ref, wd_ref, fw_ref, o_ref, h_ref, *, final):
    j = pl.program_id(1)

    @pl.when(j == 0)
    def _():
        x = x_ref[...]
        h_ref[...] = _rms(x, nw_ref[...]).astype(BF16)
        o_ref[...] = x

    h = h_ref[...]
    g = jnp.dot(h, wg_ref[...], preferred_element_type=F32)
    u = jnp.dot(h, wu_ref[...], preferred_element_type=F32)
    act = (g * jax.nn.sigmoid(g) * u).astype(BF16)
    o_ref[...] += jnp.dot(act, wd_ref[...], preferred_element_type=F32)

    if final:
        @pl.when(j == pl.num_programs(1) - 1)
        def _():
            o_ref[...] = _rms(o_ref[...], fw_ref[...])


def _ffn(x, nw, wgu, wd, fw, *, tm, tf, final):
    m, d = x.shape
    f = wd.shape[0]
    nf = f // tf
    return pl.pallas_call(
        functools.partial(_ffn_kernel, final=final),
        grid=(m // tm, nf),
        in_specs=[
            pl.BlockSpec((tm, d), lambda i, j: (i, 0)),
            pl.BlockSpec((1, d), lambda i, j: (0, 0)),
            pl.BlockSpec((d, tf), lambda i, j: (0, j)),
            pl.BlockSpec((d, tf), lambda i, j: (0, nf + j)),
            pl.BlockSpec((tf, d), lambda i, j: (j, 0)),
            pl.BlockSpec((1, d), lambda i, j: (0, 0)),
        ],
        out_specs=pl.BlockSpec((tm, d), lambda i, j: (i, 0)),
        out_shape=jax.ShapeDtypeStruct((m, d), F32),
        scratch_shapes=[pltpu.VMEM((tm, d), BF16)],
        compiler_params=_cparams(("parallel", "arbitrary")),
        name="ffn",
    )(x, nw, wgu, wgu, wd, fw)


def _pool_kernel(x_ref, nw_ref, hist_ref, pw_ref, sc_ref, o_ref, hs_ref, hbuf, *, tt, past):
    ti = pl.program_id(1)

    @pl.when(ti == 0)
    def _():
        hbuf[0:POOL_PAD, :] = hist_ref[0]

    @pl.when(ti > 0)
    def _():
        hbuf[0:POOL_PAD, :] = hbuf[tt:tt + POOL_PAD, :]

    x = x_ref[...]
    h = _rms(x, nw_ref[...])
    hbuf[POOL_PAD:POOL_PAD + tt, :] = h
    tg = ti * tt + lax.broadcasted_iota(jnp.int32, (tt, 1), 0)
    gcw = x.shape[1] // len(POOL_WINDOWS)
    for gi, wlen in enumerate(POOL_WINDOWS):
        cols = slice(gi * gcw, (gi + 1) * gcw)
        acc = h[:, cols]
        for i in range(1, wlen):
            acc = acc + hbuf[POOL_PAD - i:POOL_PAD - i + tt, cols]
        cnt = jnp.minimum(wlen, past + tg + 1).astype(F32)
        dlt = acc / cnt - h[:, cols]
        y = jnp.dot(dlt.astype(BF16), pw_ref[gi], preferred_element_type=F32)
        o_ref[:, cols] = x[:, cols] + y * sc_ref[:, cols]
    hs_ref[0] = hbuf[tt:tt + POOL_PAD, :]


def _pool_mixer(x, nw, hist, pw, sc, *, b, t, tt, past):
    m, d = x.shape
    nt = t // tt
    g, gcw, _ = pw.shape
    return pl.pallas_call(
        functools.partial(_pool_kernel, tt=tt, past=past),
        grid=(b, nt),
        in_specs=[
            pl.BlockSpec((tt, d), lambda bi, ti: (bi * nt + ti, 0)),
            pl.BlockSpec((1, d), lambda bi, ti: (0, 0)),
            pl.BlockSpec((1, POOL_PAD, d), lambda bi, ti: (bi, 0, 0)),
            pl.BlockSpec((g, gcw, gcw), lambda bi, ti: (0, 0, 0)),
            pl.BlockSpec((1, d), lambda bi, ti: (0, 0)),
        ],
        out_specs=[
            pl.BlockSpec((tt, d), lambda bi, ti: (bi * nt + ti, 0)),
            pl.BlockSpec((1, POOL_PAD, d), lambda bi, ti: (bi, 0, 0)),
        ],
        out_shape=[
            jax.ShapeDtypeStruct((m, d), F32),
            jax.ShapeDtypeStruct((b, POOL_PAD, d), F32),
        ],
        scratch_shapes=[pltpu.VMEM((tt + POOL_PAD, d), F32)],
        compiler_params=_cparams(("parallel", "arbitrary")),
        name="pool_mixer",
    )(x, nw, hist, pw, sc)


def _prep_weights(norm_mix_w, norm_ffn_w, final_norm_w, gdn_w_in, gdn_conv_w, gdn_A_log, gdn_dt_bias, gdn_norm_w,
                  gdn_w_out, pool_w, pool_scale, ffn_w_gu, ffn_w_down):
    n_a, d, _ = gdn_w_in.shape
    nv = gdn_A_log.shape[1]
    val_dim = nv * DV
    qkv_dim = gdn_conv_w.shape[2]
    main = qkv_dim + val_dim
    zeros = jnp.zeros((n_a, d, nv), F32)
    w_gate = jnp.stack([gdn_w_in[:, :, main:main + nv], gdn_w_in[:, :, main + nv:main + 2 * nv], zeros, zeros],
                       axis=-1).reshape(n_a, d, nv * GATE_SLOTS).astype(BF16)
    zv = jnp.zeros((n_a, nv), F32)
    alog = jnp.stack([zv, gdn_A_log.astype(F32), zv, zv], axis=-1).reshape(n_a, 1, nv * GATE_SLOTS)
    dtb = jnp.stack([zv, gdn_dt_bias.astype(F32), zv, zv], axis=-1).reshape(n_a, 1, nv * GATE_SLOTS)
    nh = qkv_dim // LANES
    cw = gdn_conv_w.reshape(n_a, CONV_W, nh, LANES).transpose(0, 2, 1, 3)
    cw = jnp.pad(cw, ((0, 0), (0, 0), (0, HIST_ROWS - CONV_W), (0, 0)))
    return dict(
        norm_mix=norm_mix_w[:, None, :], norm_ffn=norm_ffn_w[:, None, :], final=final_norm_w[None, :],
        w_in=gdn_w_in.astype(BF16), n_main=main, w_gate=w_gate, alog=alog, dtb=dtb, cw=cw,
        gnorm=gdn_norm_w[:, None, :], w_out=gdn_w_out.astype(BF16), pool_w=pool_w.astype(BF16),
        pool_scale=pool_scale[:, None, :], w_gu=ffn_w_gu.astype(BF16), w_down=ffn_w_down.astype(BF16),
    )


def _gdn_layer(x, conv_hist, s_hist, wts, i, j, *, b, t):
    m, d = x.shape
    nv = s_hist.shape[1]
    nk = nv // 2
    qkv_dim = conv_hist.shape[-1]
    nh = qkv_dim // LANES
    if t % GDN_TILE == 0:
        tp, tt, heads = t, GDN_TILE, GDN_HEADS
        xp = x
    else:
        tp = -(-t // CHUNK) * CHUNK
        tt, heads = tp, nk
        xp = jnp.pad(x.reshape(b, t, d), ((0, 0), (0, tp - t), (0, 0))).reshape(b * tp, d)
    mp = b * tp
    ph, ba = _norm_proj(xp, wts["norm_mix"][i], wts["w_in"][j], wts["w_gate"][j], n=wts["n_main"],
                        tm=min(1024, mp), tn=512)
    gates = _gates(ba, wts["alog"][j], wts["dtb"][j], c=CHUNK, tt=min(mp, 2048), valid=t, period=tp)
    hist = conv_hist.reshape(b, CONV_W - 1, nh, LANES).transpose(2, 0, 1, 3)
    hist = jnp.pad(hist, ((0, 0), (0, 0), (HIST_ROWS - (CONV_W - 1), 0), (0, 0)))
    o, s_new = _gdn_core(ph, hist, wts["cw"][j], gates, s_hist.astype(F32), wts["gnorm"][j], b=b, t=tp, nk=nk,
                         nv=nv, tt=tt, heads=heads, unroll=min(GDN_UNROLL, heads * tt // CHUNK))
    if tp != t:
        o = o.reshape(b, tp, o.shape[-1])[:, :t].reshape(m, o.shape[-1])
    x = _mm_residual(o, wts["w_out"][j], x, tm=min(1024, m), tn=512)
    tail = ph.reshape(ph.shape[0], b, tp, LANES)[:nh, :, t - (CONV_W - 1):t, :]
    return x, tail.transpose(1, 2, 0, 3).reshape(b, CONV_W - 1, qkv_dim), s_new


def _trunk(x3, conv_hist, s_hist, pool_hist, wts, depth):
    b, t, d = x3.shape
    m = b * t
    x = x3.reshape(m, d)
    past = pool_hist.shape[2]
    assert t >= CONV_W - 1 and t >= POOL_PAD - 1
    new_conv, new_s, new_pool = [], [], []
    for i in range(depth):
        j = i // 2
        if i % 2 == 0:
            x, conv_new, s_new = _gdn_layer(x, conv_hist[j], s_hist[j], wts, i, j, b=b, t=t)
            new_conv.append(conv_new)
            new_s.append(s_new)
        else:
            hist = jnp.pad(pool_hist[j], ((0, 0), (POOL_PAD - past, 0), (0, 0)))
            x, hs = _pool_mixer(x, wts["norm_mix"][i], hist, wts["pool_w"][j], wts["pool_scale"][j],
                                b=b, t=t, tt=min(t, 256), past=past)
            new_pool.append(hs[:, 1:, :])
        x = _ffn(x, wts["norm_ffn"][i], wts["w_gu"][i], wts["w_down"][i], wts["final"], tm=min(512, m), tf=512,
                 final=(i == depth - 1))
    return x.reshape(b, t, d), jnp.stack(new_conv), jnp.stack(new_s), jnp.stack(new_pool)


def kernel(x_prompt, x_sample, state_gdn_conv, state_gdn_S, state_pool, norm_mix_w, norm_ffn_w, final_norm_w,
           gdn_w_in, gdn_conv_w, gdn_A_log, gdn_dt_bias, gdn_norm_w, gdn_w_out, pool_w, pool_scale, ffn_w_gu,
           ffn_w_down):
    depth = norm_mix_w.shape[0]
    n_a, n_b = state_gdn_conv.shape[0], state_pool.shape[0]
    bp = x_prompt.shape[0]
    wts = _prep_weights(norm_mix_w, norm_ffn_w, final_norm_w, gdn_w_in, gdn_conv_w, gdn_A_log, gdn_dt_bias,
                        gdn_norm_w, gdn_w_out, pool_w, pool_scale, ffn_w_gu, ffn_w_down)
    conv0 = jnp.zeros((n_a, bp) + state_gdn_conv.shape[2:], x_prompt.dtype)
    s00 = jnp.zeros((n_a, bp) + state_gdn_S.shape[2:], F32)
    pool0 = jnp.zeros((n_b, bp, 0, x_prompt.shape[-1]), x_prompt.dtype)
    y_p, conv_p, s_p, pool_p = _trunk(x_prompt, conv0, s00, pool0, wts, depth)
    y_s, conv_s, s_s, pool_s = _trunk(x_sample, state_gdn_conv, state_gdn_S, state_pool, wts, depth)
    return (y_p, y_s, conv_p, s_p.astype(state_gdn_S.dtype), pool_p, conv_s, s_s.astype(state_gdn_S.dtype), pool_s)
```

```python
import functools

import jax
import jax.numpy as jnp
from jax import lax
from jax.experimental import pallas as pl
from jax.experimental.pallas import tpu as pltpu

F32 = jnp.float32
BF16 = jnp.bfloat16

EPS = 1e-6
LANES = 128
DK = 128
DV = 128
CONV_W = 4
HIST_ROWS = 8
CHUNK = 64
INV_BLOCK = 16
GDN_TILE = 512
GDN_HEADS = 4
GDN_UNROLL = 16
POOL_WINDOWS = (2, 4, 8, 16)
POOL_PAD = 16
GATE_SLOTS = 4
VMEM_LIMIT = 56 * 1024 * 1024


def _cparams(sem):
    return pltpu.CompilerParams(dimension_semantics=sem, vmem_limit_bytes=VMEM_LIMIT)


def _rms(x, w):
    ms = jnp.mean(x * x, axis=-1, keepdims=True)
    return x * lax.rsqrt(ms + EPS) * w


def _mm(a, b):
    return jnp.dot(a.astype(BF16), b.astype(BF16), preferred_element_type=F32)


def _norm_proj_kernel(x_ref, nw_ref, w_ref, wg_ref, o_ref, g_ref, h_ref, *, tn):
    @pl.when(pl.program_id(1) == 0)
    def _():
        h = _rms(x_ref[...], nw_ref[...]).astype(BF16)
        h_ref[...] = h
        g_ref[...] = jnp.dot(h, wg_ref[...], preferred_element_type=F32)

    acc = jnp.dot(h_ref[...], w_ref[...], preferred_element_type=F32)
    for c in range(tn // LANES):
        o_ref[c] = acc[:, c * LANES:(c + 1) * LANES]


def _norm_proj(x, nw, w, wg, *, n, tm, tn):
    m, d = x.shape
    assert n % tn == 0 and n <= w.shape[1] and m % tm == 0
    return pl.pallas_call(
        functools.partial(_norm_proj_kernel, tn=tn),
        grid=(m // tm, n // tn),
        in_specs=[
            pl.BlockSpec((tm, d), lambda i, j: (i, 0)),
            pl.BlockSpec((1, d), lambda i, j: (0, 0)),
            pl.BlockSpec((d, tn), lambda i, j: (0, j)),
            pl.BlockSpec((d, LANES), lambda i, j: (0, 0)),
        ],
        out_specs=[
            pl.BlockSpec((tn // LANES, tm, LANES), lambda i, j: (j, i, 0)),
            pl.BlockSpec((tm, LANES), lambda i, j: (i, 0)),
        ],
        out_shape=[
            jax.ShapeDtypeStruct((n // LANES, m, LANES), F32),
            jax.ShapeDtypeStruct((m, LANES), F32),
        ],
        scratch_shapes=[pltpu.VMEM((tm, d), BF16)],
        compiler_params=_cparams(("parallel", "arbitrary")),
        name="norm_proj",
    )(x, nw, w, wg)


def _gates_kernel(ba_ref, alog_ref, dtb_ref, tri_ref, o_ref, *, c, valid, period):
    x = ba_ref[...]
    beta = jax.nn.sigmoid(x)
    y = x + dtb_ref[...]
    softplus = jnp.maximum(y, 0.0) + jnp.log1p(jnp.exp(-jnp.abs(y)))
    g = -jnp.exp(alog_ref[...]) * softplus
    if valid != period:
        assert period & (period - 1) == 0 and x.shape[0] % period == 0
        row = lax.broadcasted_iota(jnp.int32, x.shape, 0) & (period - 1)
        beta = jnp.where(row < valid, beta, 0.0)
        g = jnp.where(row < valid, g, 0.0)
    lane = lax.broadcasted_iota(jnp.int32, (c, x.shape[1]), 1)
    is_beta = (lane & (GATE_SLOTS - 1)) == 0
    tri = tri_ref[...]
    for n in range(x.shape[0] // c):
        sl = slice(n * c, (n + 1) * c)
        gc = jnp.dot(tri, g[sl], preferred_element_type=F32, precision=lax.Precision.HIGHEST)
        o_ref[sl, :] = jnp.where(is_beta, beta[sl], gc)


def _gates(ba, alog, dtb, *, c, tt, valid, period):
    m = ba.shape[0]
    tri = jnp.tril(jnp.ones((c, c), F32))
    return pl.pallas_call(
        functools.partial(_gates_kernel, c=c, valid=valid, period=period),
        grid=(m // tt,),
        in_specs=[
            pl.BlockSpec((tt, LANES), lambda i: (i, 0)),
            pl.BlockSpec((1, LANES), lambda i: (0, 0)),
            pl.BlockSpec((1, LANES), lambda i: (0, 0)),
            pl.BlockSpec((c, c), lambda i: (0, 0)),
        ],
        out_specs=pl.BlockSpec((tt, LANES), lambda i: (i, 0)),
        out_shape=jax.ShapeDtypeStruct((m, LANES), F32),
        compiler_params=_cparams(("parallel",)),
        name="gates",
    )(ba, alog, dtb, tri)


def _inv_unit_lower(l_mats):
    c, w = l_mats[0].shape
    npack = w // c
    r = lax.broadcasted_iota(jnp.int32, (c, w), 0)
    lane = lax.broadcasted_iota(jnp.int32, (c, w), 1)
    s = lane & (c - 1)
    shift = INV_BLOCK.bit_length() - 1
    same = lax.shift_right_logical(r, shift) == lax.shift_right_logical(s, shift)
    eye = jnp.where(r == s, 1.0, 0.0).astype(F32)
    part = [lax.shift_right_logical(lane, c.bit_length() - 1) == p for p in range(npack)]

    def mm(xs, ys):
        if npack > 1:
            ys = [jnp.concatenate([jnp.where(part[p], y, 0.0) for p in range(npack)], axis=0) for y in ys]
        return [_mm(x, y) for x, y in zip(xs, ys)]

    def stack(xs, ys):
        return [jnp.concatenate([x, y], axis=0) for x, y in zip(xs, ys)]

    d1 = [jnp.where(same, l, 0.0) for l in l_mats]
    d2 = mm(d1, d1)
    t = mm(stack(d2, d1), d2)
    d4 = [x[:c] for x in t]
    y = [eye - a + b - x[c:] for a, b, x in zip(d1, d2, t)]
    t = mm(stack(d4, y), d4)
    y = [a + x[c:] for a, x in zip(y, t)]
    x0 = [a + b for a, b in zip(y, mm(y, [x[:c] for x in t]))]
    nblk = c // INV_BLOCK
    if nblk == 1:
        return x0
    e = [jnp.where(same, 0.0, l) for l in l_mats]
    nmat = mm(x0, e)
    z = [eye - a for a in nmat]
    if nblk > 2:
        assert nblk == 4
        n2 = mm(nmat, nmat)
        n3 = mm(nmat, n2)
        z = [a + b - x for a, b, x in zip(z, n2, n3)]
    return mm(z, x0)


def _gdn_kernel(q_ref, k_ref, v_ref, z_ref, hq_ref, hk_ref, hv_ref, cq_ref, ck_ref, cv_ref, g_ref, s0_ref,
                nw_ref, o_ref, s_ref, xc_s, hist_s, gcol_s, bcol_s, begcol_s, gl_s, qk_s, kk_s, kbd_s, vbd_s,
                kd_s, wq_s, u_s, a_s, st_s, *, tt, c, heads, unroll):
    n = tt // c
    hg = pl.program_id(1)
    half = LANES // 2
    assert c == half

    @pl.when(pl.program_id(2) == 0)
    def _():
        for g in range(heads):
            hist_s[g, 0] = hq_ref[g, 0]
            hist_s[g, 1] = hk_ref[g, 0]
            for h in range(2):
                hist_s[g, 2 + h] = hv_ref[2 * g + h, 0]
                st_s[g, :, h * DV:(h + 1) * DV] = s0_ref[0, 2 * g + h]

    def l2n(x):
        return x * lax.rsqrt(jnp.sum(x * x, axis=-1, keepdims=True) + EPS)

    left = lax.broadcasted_iota(jnp.int32, (c, LANES), 1) < half
    full = (c, LANES)
    zero_k = jnp.zeros((c, LANES), BF16)
    base = HIST_ROWS - (CONV_W - 1)

    def prep(g):
        xin = (q_ref[g], k_ref[g], v_ref[2 * g], v_ref[2 * g + 1])
        cws = (cq_ref[g], ck_ref[g], cv_ref[2 * g], cv_ref[2 * g + 1])
        for slot in range(4):
            xc_s[slot, 0:HIST_ROWS, :] = hist_s[g, slot]
            xc_s[slot, HIST_ROWS:HIST_ROWS + tt, :] = xin[slot]
            hist_s[g, slot] = xc_s[slot, tt:tt + HIST_ROWS, :]
        shift = (LANES - 2 * GATE_SLOTS * (hg * heads + g)) & (LANES - 1)

        def conv_silu(slot, r0):
            acc = xc_s[slot, r0 + base:r0 + base + c, :] * cws[slot][0:1, :]
            for j in range(1, CONV_W):
                acc = acc + xc_s[slot, r0 + base + j:r0 + base + j + c, :] * cws[slot][j:j + 1, :]
            return acc * jax.nn.sigmoid(acc)

        for nn in range(n):
            r0 = nn * c
            job = nn * heads + g
            rows = slice(job * c, (job + 1) * c)
            gt = pltpu.roll(g_ref[r0:r0 + c, :], shift, 1)
            gc = [jnp.broadcast_to(gt[:, GATE_SLOTS * h + 1:GATE_SLOTS * h + 2], full) for h in range(2)]
            beta = [jnp.broadcast_to(gt[:, GATE_SLOTS * h:GATE_SLOTS * h + 1], full) for h in range(2)]
            eg = [jnp.exp(x) for x in gc]
            gcol_s[rows, :] = jnp.where(left, gc[0], gc[1])
            bcol = jnp.where(left, beta[0], beta[1])
            bcol_s[rows, :] = bcol
            begcol_s[rows, :] = bcol * jnp.where(left, eg[0], eg[1])
            gl = [x[c - 1:c, :] for x in gc]
            for h in range(2):
                gl_s[job, :, h * LANES:(h + 1) * LANES] = jnp.exp(gl[h])

            q = l2n(conv_silu(0, r0)) * (DK ** -0.5)
            k = l2n(conv_silu(1, r0))
            k16 = k.astype(BF16)
            qk_s[job, 0:c, :] = q.astype(BF16)
            qk_s[job, c:2 * c, :] = k16
            kk_s[job, 0:c, :] = k16
            kk_s[job, c:2 * c, :] = k16
            for h in range(2):
                wq_s[job, (2 * h + 1) * c:(2 * h + 2) * c, :] = (q * eg[h]).astype(BF16)
                kd_s[job, h * c:(h + 1) * c, :] = (k * jnp.exp(gl[h] - gc[h])).astype(BF16)
                kbd_s[job, h * c:(h + 1) * c, h * LANES:(h + 1) * LANES] = k16
                kbd_s[job, h * c:(h + 1) * c, (1 - h) * LANES:(2 - h) * LANES] = zero_k
                vbd_s[job, h * c:(h + 1) * c, h * LANES:(h + 1) * LANES] = conv_silu(2 + h, r0).astype(BF16)
                vbd_s[job, h * c:(h + 1) * c, (1 - h) * LANES:(2 - h) * LANES] = zero_k

    for g in range(heads):
        prep(g)

    r = lax.broadcasted_iota(jnp.int32, (c, LANES), 0)
    sloc = lax.broadcasted_iota(jnp.int32, (c, LANES), 1) & (c - 1)
    causal = r >= sloc
    strict = r > sloc
    diag = r == sloc

    def as_row(x):
        return jnp.sum(jnp.where(diag, x, 0.0), axis=0, keepdims=True)

    def intra(i):
        js = [i * unroll + j for j in range(unroll)]
        sls = [slice(j * c, (j + 1) * c) for j in js]
        gcol = [gcol_s[sl, :] for sl in sls]
        bcl = [bcol_s[sl, :] for sl in sls]
        begcol = [begcol_s[sl, :] for sl in sls]
        decay = [jnp.where(causal, jnp.exp(jnp.where(causal, x - as_row(x), 0.0)), 0.0) for x in gcol]
        qkk = [lax.dot_general(qk_s[j], kk_s[j], (((1,), (1,)), ((), ())), preferred_element_type=F32)
               for j in js]
        for j, x, d in zip(js, qkk, decay):
            a_s[j] = (x[:c] * d).astype(BF16)
        tm = _inv_unit_lower([jnp.where(strict, x[c:] * d * b, 0.0) for x, d, b in zip(qkk, decay, bcl)])
        w = [jnp.dot((x * as_row(gg)).astype(BF16), kbd_s[j], preferred_element_type=F32)
             for j, x, gg in zip(js, tm, begcol)]
        u = [jnp.dot((x * as_row(b)).astype(BF16), vbd_s[j], preferred_element_type=F32)
             for j, x, b in zip(js, tm, bcl)]
        for j, wn, un in zip(js, w, u):
            u_s[j] = un
            for h in range(2):
                wq_s[j, 2 * h * c:(2 * h + 1) * c, :] = wn[:, h * DK:(h + 1) * DK].astype(BF16)

    for i in range(heads * n // unroll):
        intra(i)

    zero_c = jnp.zeros((c, DV), F32)

    def scan(nn, carry):
        js = [nn * heads + g for g in range(heads)]
        sp = [st_s[g] for g in range(heads)]
        r1 = [jnp.dot(wq_s[j], x.astype(BF16), preferred_element_type=F32) for j, x in zip(js, sp)]
        u = [u_s[j] for j in js]
        vnbd = [jnp.concatenate([jnp.concatenate([un[:, :DV] - x[0:c, :DV], zero_c], axis=1),
                                 jnp.concatenate([zero_c, un[:, DV:] - x[2 * c:3 * c, DV:]], axis=1)],
                                axis=0).astype(BF16) for un, x in zip(u, r1)]
        av = [jnp.dot(a_s[j], v, preferred_element_type=F32) for j, v in zip(js, vnbd)]
        upd = [lax.dot_general(kd_s[j], v, (((0,), (0,)), ((), ())), preferred_element_type=F32)
               for j, v in zip(js, vnbd)]
        rows = pl.ds(pl.multiple_of(nn * c, c), c)
        for g, j in enumerate(js):
            st_s[g] = sp[g] * gl_s[j] + upd[g]
            for h in range(2):
                hv = 2 * g + h
                o = r1[g][(2 * h + 1) * c:(2 * h + 2) * c, h * DV:(h + 1) * DV] + av[g][:, h * DV:(h + 1) * DV]
                zf = z_ref[hv, rows, :]
                o = (o * lax.rsqrt(jnp.mean(o * o, axis=-1, keepdims=True) + EPS) * nw_ref[...]
                     * (zf * jax.nn.sigmoid(zf)))
                o_ref[rows, hv * DV:(hv + 1) * DV] = o.astype(o_ref.dtype)
        return carry

    if n == 1:
        scan(0, 0)
    else:
        lax.fori_loop(0, n, scan, 0)

    for g in range(heads):
        for h in range(2):
            s_ref[0, 2 * g + h] = st_s[g, :, h * DV:(h + 1) * DV]


def _gdn_core(ph, hist, cw, gates, s0, nw, *, b, t, nk, nv, tt, heads, unroll):
    c = CHUNK
    assert t % tt == 0 and tt % c == 0 and nk % heads == 0 and nv == 2 * nk and DK == LANES and DV == LANES
    assert (heads * tt // c) % unroll == 0
    m = b * t
    nt = t // tt
    jobs = heads * tt // c
    vh = 2 * heads
    off_k, off_v, off_z = nk // heads, 2 * nk // vh, (2 * nk + nv) // vh
    seq = lambda off: (lambda bi, h, ti: (off + h, bi * nt + ti, 0))
    hst = lambda off: (lambda bi, h, ti: (off + h, bi, 0, 0))
    cwt = lambda off: (lambda bi, h, ti: (off + h, 0, 0))
    f32 = lambda *shape: pltpu.VMEM(shape, F32)
    bf16 = lambda *shape: pltpu.VMEM(shape, BF16)
    return pl.pallas_call(
        functools.partial(_gdn_kernel, tt=tt, c=c, heads=heads, unroll=unroll),
        grid=(b, nk // heads, nt),
        in_specs=[
            pl.BlockSpec((heads, tt, LANES), seq(0)), pl.BlockSpec((heads, tt, LANES), seq(off_k)),
            pl.BlockSpec((vh, tt, LANES), seq(off_v)), pl.BlockSpec((vh, tt, LANES), seq(off_z)),
            pl.BlockSpec((heads, 1, HIST_ROWS, LANES), hst(0)), pl.BlockSpec((heads, 1, HIST_ROWS, LANES), hst(off_k)),
            pl.BlockSpec((vh, 1, HIST_ROWS, LANES), hst(off_v)),
            pl.BlockSpec((heads, HIST_ROWS, LANES), cwt(0)), pl.BlockSpec((heads, HIST_ROWS, LANES), cwt(off_k)),
            pl.BlockSpec((vh, HIST_ROWS, LANES), cwt(off_v)),
            pl.BlockSpec((tt, LANES), lambda bi, h, ti: (bi * nt + ti, 0)),
            pl.BlockSpec((1, vh, DK, DV), lambda bi, h, ti: (bi, h, 0, 0)),
            pl.BlockSpec((1, DV), lambda bi, h, ti: (0, 0)),
        ],
        out_specs=[
            pl.BlockSpec((tt, vh * DV), lambda bi, h, ti: (bi * nt + ti, h)),
            pl.BlockSpec((1, vh, DK, DV), lambda bi, h, ti: (bi, h, 0, 0)),
        ],
        out_shape=[
            jax.ShapeDtypeStruct((m, nv * DV), BF16),
            jax.ShapeDtypeStruct((b, nv, DK, DV), F32),
        ],
        scratch_shapes=[
            f32(4, tt + HIST_ROWS, LANES), f32(heads, 4, HIST_ROWS, LANES),
            f32(heads * tt, LANES), f32(heads * tt, LANES), f32(heads * tt, LANES), f32(jobs, 1, 2 * LANES),
            bf16(jobs, 2 * c, LANES), bf16(jobs, 2 * c, LANES), bf16(jobs, 2 * c, 2 * LANES),
            bf16(jobs, 2 * c, 2 * LANES), bf16(jobs, 2 * c, LANES), bf16(jobs, 4 * c, LANES),
            f32(jobs, c, 2 * DV), bf16(jobs, c, LANES), f32(heads, DK, 2 * DV),
        ],
        compiler_params=_cparams(("parallel", "parallel", "arbitrary")),
        name="gdn_core",
    )(ph, ph, ph, ph, hist, hist, hist, cw, cw, cw, gates, s0, nw)


def _mm_res_kernel(a_ref, w_ref, r_ref, o_ref):
    o_ref[...] = r_ref[...] + jnp.dot(a_ref[...], w_ref[...], preferred_element_type=F32)


def _mm_residual(a, w, res, *, tm, tn):
    m, k = a.shape
    n = w.shape[1]
    return pl.pallas_call(
        _mm_res_kernel,
        grid=(m // tm, n // tn),
        in_specs=[
            pl.BlockSpec((tm, k), lambda i, j: (i, 0)),
            pl.BlockSpec((k, tn), lambda i, j: (0, j)),
            pl.BlockSpec((tm, tn), lambda i, j: (i, j)),
        ],
        out_specs=pl.BlockSpec((tm, tn), lambda i, j: (i, j)),
        out_shape=jax.ShapeDtypeStruct((m, n), F32),
        compiler_params=_cparams(("parallel", "arbitrary")),
        name="mm_residual",
    )(a, w, res)


def _ffn_kernel(x_ref, nw_ref, wg_ref, wu_ref, wd_ref, fw_ref, o_ref, h_ref, *, final):
    j = pl.program_id(1)

    @pl.when(j == 0)
    def _():
        x = x_ref[...]
        h_ref[...] = _rms(x, nw_ref[...]).astype(BF16)
        o_ref[...] = x

    h = h_ref[...]
    g = jnp.dot(h, wg_ref[...], preferred_element_type=F32)
    u = jnp.dot(h, wu_ref[...], preferred_element_type=F32)
    act = (g * jax.nn.sigmoid(g) * u).astype(BF16)
    o_ref[...] += jnp.dot(act, wd_ref[...], preferred_element_type=F32)

    if final:
        @pl.when(j == pl.num_programs(1) - 1)
        def _():
            o_ref[...] = _rms(o_ref[...], fw_ref[...])


def _ffn(x, nw, wgu, wd, fw, *, tm, tf, final):
    m, d = x.shape
    f = wd.shape[0]
    nf = f // tf
    return pl.pallas_call(
        functools.partial(_ffn_kernel, final=final),
        grid=(m // tm, nf),
        in_specs=[
            pl.BlockSpec((tm, d), lambda i, j: (i, 0)),
            pl.BlockSpec((1, d), lambda i, j: (0, 0)),
            pl.BlockSpec((d, tf), lambda i, j: (0, j)),
            pl.BlockSpec((d, tf), lambda i, j: (0, nf + j)),
            pl.BlockSpec((tf, d), lambda i, j: (j, 0)),
            pl.BlockSpec((1, d), lambda i, j: (0, 0)),
        ],
        out_specs=pl.BlockSpec((tm, d), lambda i, j: (i, 0)),
        out_shape=jax.ShapeDtypeStruct((m, d), F32),
        scratch_shapes=[pltpu.VMEM((tm, d), BF16)],
        compiler_params=_cparams(("parallel", "arbitrary")),
        name="ffn",
    )(x, nw, wgu, wgu, wd, fw)


def _pool_kernel(x_ref, nw_ref, hist_ref, pw_ref, sc_ref, o_ref, hs_ref, hbuf, *, tt, past):
    ti = pl.program_id(1)

    @pl.when(ti == 0)
    def _():
        hbuf[0:POOL_PAD, :] = hist_ref[0]

    @pl.when(ti > 0)
    def _():
        hbuf[0:POOL_PAD, :] = hbuf[tt:tt + POOL_PAD, :]

    x = x_ref[...]
    h = _rms(x, nw_ref[...])
    hbuf[POOL_PAD:POOL_PAD + tt, :] = h
    tg = ti * tt + lax.broadcasted_iota(jnp.int32, (tt, 1), 0)
    gcw = x.shape[1] // len(POOL_WINDOWS)
    for gi, wlen in enumerate(POOL_WINDOWS):
        cols = slice(gi * gcw, (gi + 1) * gcw)
        acc = h[:, cols]
        for i in range(1, wlen):
            acc = acc + hbuf[POOL_PAD - i:POOL_PAD - i + tt, cols]
        cnt = jnp.minimum(wlen, past + tg + 1).astype(F32)
        dlt = acc / cnt - h[:, cols]
        y = jnp.dot(dlt.astype(BF16), pw_ref[gi], preferred_element_type=F32)
        o_ref[:, cols] = x[:, cols] + y * sc_ref[:, cols]
    hs_ref[0] = hbuf[tt:tt + POOL_PAD, :]


def _pool_mixer(x, nw, hist, pw, sc, *, b, t, tt, past):
    m, d = x.shape
    nt = t // tt
    g, gcw, _ = pw.shape
    return pl.pallas_call(
        functools.partial(_pool_kernel, tt=tt, past=past),
        grid=(b, nt),
        in_specs=[
            pl.BlockSpec((tt, d), lambda bi, ti: (bi * nt + ti, 0)),
            pl.BlockSpec((1, d), lambda bi, ti: (0, 0)),
            pl.BlockSpec((1, POOL_PAD, d), lambda bi, ti: (bi, 0, 0)),
            pl.BlockSpec((g, gcw, gcw), lambda bi, ti: (0, 0, 0)),
            pl.BlockSpec((1, d), lambda bi, ti: (0, 0)),
        ],
        out_specs=[
            pl.BlockSpec((tt, d), lambda bi, ti: (bi * nt + ti, 0)),
            pl.BlockSpec((1, POOL_PAD, d), lambda bi, ti: (bi, 0, 0)),
        ],
        out_shape=[
            jax.ShapeDtypeStruct((m, d), F32),
            jax.ShapeDtypeStruct((b, POOL_PAD, d), F32),
        ],
        scratch_shapes=[pltpu.VMEM((tt + POOL_PAD, d), F32)],
        compiler_params=_cparams(("parallel", "arbitrary")),
        name="pool_mixer",
    )(x, nw, hist, pw, sc)


def _prep_weights(norm_mix_w, norm_ffn_w, final_norm_w, gdn_w_in, gdn_conv_w, gdn_A_log, gdn_dt_bias, gdn_norm_w,
                  gdn_w_out, pool_w, pool_scale, ffn_w_gu, ffn_w_down):
    n_a, d, _ = gdn_w_in.shape
    nv = gdn_A_log.shape[1]
    val_dim = nv * DV
    qkv_dim = gdn_conv_w.shape[2]
    main = qkv_dim + val_dim
    zeros = jnp.zeros((n_a, d, nv), F32)
    w_gate = jnp.stack([gdn_w_in[:, :, main:main + nv], gdn_w_in[:, :, main + nv:main + 2 * nv], zeros, zeros],
                       axis=-1).reshape(n_a, d, nv * GATE_SLOTS).astype(BF16)
    zv = jnp.zeros((n_a, nv), F32)
    alog = jnp.stack([zv, gdn_A_log.astype(F32), zv, zv], axis=-1).reshape(n_a, 1, nv * GATE_SLOTS)
    dtb = jnp.stack([zv, gdn_dt_bias.astype(F32), zv, zv], axis=-1).reshape(n_a, 1, nv * GATE_SLOTS)
    nh = qkv_dim // LANES
    cw = gdn_conv_w.reshape(n_a, CONV_W, nh, LANES).transpose(0, 2, 1, 3)
    cw = jnp.pad(cw, ((0, 0), (0, 0), (0, HIST_ROWS - CONV_W), (0, 0)))
    per_layer = lambda w: [w[l].astype(BF16) for l in range(w.shape[0])]
    return dict(
        norm_mix=norm_mix_w[:, None, :], norm_ffn=norm_ffn_w[:, None, :], final=final_norm_w[None, :],
        w_in=per_layer(gdn_w_in), n_main=main, w_gate=w_gate, alog=alog, dtb=dtb, cw=cw,
        gnorm=gdn_norm_w[:, None, :], w_out=per_layer(gdn_w_out), pool_w=pool_w.astype(BF16),
        pool_scale=pool_scale[:, None, :], w_gu=per_layer(ffn_w_gu), w_down=per_layer(ffn_w_down),
    )


def _gdn_layer(x, conv_hist, s_hist, wts, i, j, *, b, t):
    m, d = x.shape
    nv = s_hist.shape[1]
    nk = nv // 2
    qkv_dim = conv_hist.shape[-1]
    nh = qkv_dim // LANES
    if t % GDN_TILE == 0:
        tp, tt, heads = t, GDN_TILE, GDN_HEADS
        xp = x
    else:
        tp = -(-t // CHUNK) * CHUNK
        tt, heads = tp, nk
        xp = jnp.pad(x.reshape(b, t, d), ((0, 0), (0, tp - t), (0, 0))).reshape(b * tp, d)
    mp = b * tp
    ph, ba = _norm_proj(xp, wts["norm_mix"][i], wts["w_in"][j], wts["w_gate"][j], n=wts["n_main"],
                        tm=min(1024, mp), tn=512)
    gates = _gates(ba, wts["alog"][j], wts["dtb"][j], c=CHUNK, tt=min(mp, 2048), valid=t, period=tp)
    hist = conv_hist.reshape(b, CONV_W - 1, nh, LANES).transpose(2, 0, 1, 3)
    hist = jnp.pad(hist, ((0, 0), (0, 0), (HIST_ROWS - (CONV_W - 1), 0), (0, 0)))
    o, s_new = _gdn_core(ph, hist, wts["cw"][j], gates, s_hist.astype(F32), wts["gnorm"][j], b=b, t=tp, nk=nk,
                         nv=nv, tt=tt, heads=heads, unroll=min(GDN_UNROLL, heads * tt // CHUNK))
    if tp != t:
        o = o.reshape(b, tp, o.shape[-1])[:, :t].reshape(m, o.shape[-1])
    x = _mm_residual(o, wts["w_out"][j], x, tm=min(1024, m), tn=512)
    tail = ph.reshape(ph.shape[0], b, tp, LANES)[:nh, :, t - (CONV_W - 1):t, :]
    return x, tail.transpose(1, 2, 0, 3).reshape(b, CONV_W - 1, qkv_dim), s_new


def _trunk(x3, conv_hist, s_hist, pool_hist, wts, depth):
    b, t, d = x3.shape
    m = b * t
    x = x3.reshape(m, d)
    past = pool_hist.shape[2]
    assert t >= CONV_W - 1 and t >= POOL_PAD - 1
    new_conv, new_s, new_pool = [], [], []
    for i in range(depth):
        j = i // 2
        if i % 2 == 0:
            x, conv_new, s_new = _gdn_layer(x, conv_hist[j], s_hist[j], wts, i, j, b=b, t=t)
            new_conv.append(conv_new)
            new_s.append(s_new)
        else:
            hist = jnp.pad(pool_hist[j], ((0, 0), (POOL_PAD - past, 0), (0, 0)))
            x, hs = _pool_mixer(x, wts["norm_mix"][i], hist, wts["pool_w"][j], wts["pool_scale"][j],
                                b=b, t=t, tt=min(t, 256), past=past)
            new_pool.append(hs[:, 1:, :])
        x = _ffn(x, wts["norm_ffn"][i], wts["w_gu"][i], wts["w_down"][i], wts["final"], tm=min(512, m), tf=512,
                 final=(i == depth - 1))
    return x.reshape(b, t, d), jnp.stack(new_conv), jnp.stack(new_s), jnp.stack(new_pool)


def kernel(x_prompt, x_sample, state_gdn_conv, state_gdn_S, state_pool, norm_mix_w, norm_ffn_w, final_norm_w,
           gdn_w_in, gdn_conv_w, gdn_A_log, gdn_dt_bias, gdn_norm_w, gdn_w_out, pool_w, pool_scale, ffn_w_gu,
           ffn_w_down):
    depth = norm_mix_w.shape[0]
    n_a, n_b = state_gdn_conv.shape[0], state_pool.shape[0]
    bp = x_prompt.shape[0]
    wts = _prep_weights(norm_mix_w, norm_ffn_w, final_norm_w, gdn_w_in, gdn_conv_w, gdn_A_log, gdn_dt_bias,
                        gdn_norm_w, gdn_w_out, pool_w, pool_scale, ffn_w_gu, ffn_w_down)
    conv0 = jnp.zeros((n_a, bp) + state_gdn_conv.shape[2:], x_prompt.dtype)
    s00 = jnp.zeros((n_a, bp) + state_gdn_S.shape[2:], F32)
    pool0 = jnp.zeros((n_b, bp, 0, x_prompt.shape[-1]), x_prompt.dtype)
    y_p, conv_p, s_p, pool_p = _trunk(x_prompt, conv0, s00, pool0, wts, depth)
    y_s, conv_s, s_s, pool_s = _trunk(x_sample, state_gdn_conv, state_gdn_S, state_pool, wts, depth)
    return (y_p, y_s, conv_p, s_p.astype(state_gdn_S.dtype), pool_p, conv_s, s_s.astype(state_gdn_S.dtype), pool_s)
```

```python
import functools

import jax
import jax.numpy as jnp
from jax import lax
from jax.experimental import pallas as pl
from jax.experimental.pallas import tpu as pltpu

F32 = jnp.float32
BF16 = jnp.bfloat16

EPS = 1e-6
LANES = 128
DK = 128
DV = 128
CONV_W = 4
HIST_ROWS = 8
CHUNK = 64
INV_BLOCK = 16
GDN_TILE = 256
GDN_HEADS = 8
GDN_UNROLL = 16
POOL_WINDOWS = (2, 4, 8, 16)
POOL_PAD = 16
GATE_SLOTS = 4
VMEM_LIMIT = 56 * 1024 * 1024


def _cparams(sem):
    return pltpu.CompilerParams(dimension_semantics=sem, vmem_limit_bytes=VMEM_LIMIT)


def _rms(x, w):
    ms = jnp.mean(x * x, axis=-1, keepdims=True)
    return x * lax.rsqrt(ms + EPS) * w


def _mm(a, b):
    return jnp.dot(a.astype(BF16), b.astype(BF16), preferred_element_type=F32)


def _norm_proj_kernel(x_ref, nw_ref, w_ref, wg_ref, o_ref, g_ref, h_ref, *, tn):
    @pl.when(pl.program_id(1) == 0)
    def _():
        h = _rms(x_ref[...], nw_ref[...]).astype(BF16)
        h_ref[...] = h
        g_ref[...] = jnp.dot(h, wg_ref[...], preferred_element_type=F32)

    acc = jnp.dot(h_ref[...], w_ref[...], preferred_element_type=F32)
    for c in range(tn // LANES):
        o_ref[c] = acc[:, c * LANES:(c + 1) * LANES]


def _norm_proj(x, nw, w, wg, *, layer, n, tm, tn):
    m, d = x.shape
    assert n % tn == 0 and n <= w.shape[2] and m % tm == 0
    return pl.pallas_call(
        functools.partial(_norm_proj_kernel, tn=tn),
        grid=(m // tm, n // tn),
        in_specs=[
            pl.BlockSpec((tm, d), lambda i, j: (i, 0)),
            pl.BlockSpec((1, d), lambda i, j: (0, 0)),
            pl.BlockSpec((None, d, tn), lambda i, j: (layer, 0, j)),
            pl.BlockSpec((d, LANES), lambda i, j: (0, 0)),
        ],
        out_specs=[
            pl.BlockSpec((tn // LANES, tm, LANES), lambda i, j: (j, i, 0)),
            pl.BlockSpec((tm, LANES), lambda i, j: (i, 0)),
        ],
        out_shape=[
            jax.ShapeDtypeStruct((n // LANES, m, LANES), F32),
            jax.ShapeDtypeStruct((m, LANES), F32),
        ],
        scratch_shapes=[pltpu.VMEM((tm, d), BF16)],
        compiler_params=_cparams(("parallel", "arbitrary")),
        name="norm_proj",
    )(x, nw, w, wg)


def _gates_kernel(ba_ref, alog_ref, dtb_ref, tri_ref, o_ref, *, c, valid, period):
    x = ba_ref[...]
    beta = jax.nn.sigmoid(x)
    y = x + dtb_ref[...]
    softplus = jnp.maximum(y, 0.0) + jnp.log1p(jnp.exp(-jnp.abs(y)))
    g = -jnp.exp(alog_ref[...]) * softplus
    if valid != period:
        assert period & (period - 1) == 0 and x.shape[0] % period == 0
        row = lax.broadcasted_iota(jnp.int32, x.shape, 0) & (period - 1)
        beta = jnp.where(row < valid, beta, 0.0)
        g = jnp.where(row < valid, g, 0.0)
    lane = lax.broadcasted_iota(jnp.int32, (c, x.shape[1]), 1)
    is_beta = (lane & (GATE_SLOTS - 1)) == 0
    tri = tri_ref[...]
    for n in range(x.shape[0] // c):
        sl = slice(n * c, (n + 1) * c)
        gc = jnp.dot(tri, g[sl], preferred_element_type=F32, precision=lax.Precision.HIGHEST)
        o_ref[sl, :] = jnp.where(is_beta, beta[sl], gc)


def _gates(ba, alog, dtb, *, c, tt, valid, period):
    m = ba.shape[0]
    tri = jnp.tril(jnp.ones((c, c), F32))
    return pl.pallas_call(
        functools.partial(_gates_kernel, c=c, valid=valid, period=period),
        grid=(m // tt,),
        in_specs=[
            pl.BlockSpec((tt, LANES), lambda i: (i, 0)),
            pl.BlockSpec((1, LANES), lambda i: (0, 0)),
            pl.BlockSpec((1, LANES), lambda i: (0, 0)),
            pl.BlockSpec((c, c), lambda i: (0, 0)),
        ],
        out_specs=pl.BlockSpec((tt, LANES), lambda i: (i, 0)),
        out_shape=jax.ShapeDtypeStruct((m, LANES), F32),
        compiler_params=_cparams(("parallel",)),
        name="gates",
    )(ba, alog, dtb, tri)


def _inv_unit_lower(l_mats):
    c, w = l_mats[0].shape
    npack = w // c
    r = lax.broadcasted_iota(jnp.int32, (c, w), 0)
    lane = lax.broadcasted_iota(jnp.int32, (c, w), 1)
    s = lane & (c - 1)
    shift = INV_BLOCK.bit_length() - 1
    same = lax.shift_right_logical(r, shift) == lax.shift_right_logical(s, shift)
    eye = jnp.where(r == s, 1.0, 0.0).astype(F32)
    part = [lax.shift_right_logical(lane, c.bit_length() - 1) == p for p in range(npack)]

    def mm(xs, ys):
        if npack > 1:
            ys = [jnp.concatenate([jnp.where(part[p], y, 0.0) for p in range(npack)], axis=0) for y in ys]
        return [_mm(x, y) for x, y in zip(xs, ys)]

    def stack(xs, ys):
        return [jnp.concatenate([x, y], axis=0) for x, y in zip(xs, ys)]

    d1 = [jnp.where(same, l, 0.0) for l in l_mats]
    d2 = mm(d1, d1)
    t = mm(stack(d2, d1), d2)
    d4 = [x[:c] for x in t]
    y = [eye - a + b - x[c:] for a, b, x in zip(d1, d2, t)]
    t = mm(stack(d4, y), d4)
    y = [a + x[c:] for a, x in zip(y, t)]
    x0 = [a + b for a, b in zip(y, mm(y, [x[:c] for x in t]))]
    nblk = c // INV_BLOCK
    if nblk == 1:
        return x0
    e = [jnp.where(same, 0.0, l) for l in l_mats]
    nmat = mm(x0, e)
    z = [eye - a for a in nmat]
    if nblk > 2:
        assert nblk == 4
        n2 = mm(nmat, nmat)
        n3 = mm(nmat, n2)
        z = [a + b - x for a, b, x in zip(z, n2, n3)]
    return mm(z, x0)


def _gdn_kernel(q_ref, k_ref, v_ref, z_ref, hq_ref, hk_ref, hv_ref, cq_ref, ck_ref, cv_ref, g_ref, s0_ref,
                nw_ref, o_ref, s_ref, xc_s, hist_s, gcol_s, bcol_s, begcol_s, gl_s, qk_s, kk_s, kbd_s, vbd_s,
                kd_s, wq_s, u_s, a_s, st_s, *, tt, c, heads, unroll):
    n = tt // c
    hg = pl.program_id(1)
    half = LANES // 2
    assert c == half

    @pl.when(pl.program_id(2) == 0)
    def _():
        for g in range(heads):
            hist_s[g, 0] = hq_ref[g, 0]
            hist_s[g, 1] = hk_ref[g, 0]
            for h in range(2):
                hist_s[g, 2 + h] = hv_ref[2 * g + h, 0]
                st_s[g, :, h * DV:(h + 1) * DV] = s0_ref[0, 2 * g + h]

    def l2n(x):
        return x * lax.rsqrt(jnp.sum(x * x, axis=-1, keepdims=True) + EPS)

    left = lax.broadcasted_iota(jnp.int32, (c, LANES), 1) < half
    full = (c, LANES)
    zero_k = jnp.zeros((c, LANES), BF16)
    base = HIST_ROWS - (CONV_W - 1)

    def prep(g):
        xin = (q_ref[g], k_ref[g], v_ref[2 * g], v_ref[2 * g + 1])
        cws = (cq_ref[g], ck_ref[g], cv_ref[2 * g], cv_ref[2 * g + 1])
        for slot in range(4):
            xc_s[slot, 0:HIST_ROWS, :] = hist_s[g, slot]
            xc_s[slot, HIST_ROWS:HIST_ROWS + tt, :] = xin[slot]
            hist_s[g, slot] = xc_s[slot, tt:tt + HIST_ROWS, :]
        shift = (LANES - 2 * GATE_SLOTS * (hg * heads + g)) & (LANES - 1)

        def conv_silu(slot, r0):
            acc = xc_s[slot, r0 + base:r0 + base + c, :] * cws[slot][0:1, :]
            for j in range(1, CONV_W):
                acc = acc + xc_s[slot, r0 + base + j:r0 + base + j + c, :] * cws[slot][j:j + 1, :]
            return acc * jax.nn.sigmoid(acc)

        for nn in range(n):
            r0 = nn * c
            job = nn * heads + g
            rows = slice(job * c, (job + 1) * c)
            gt = pltpu.roll(g_ref[r0:r0 + c, :], shift, 1)
            gc = [jnp.broadcast_to(gt[:, GATE_SLOTS * h + 1:GATE_SLOTS * h + 2], full) for h in range(2)]
            beta = [jnp.broadcast_to(gt[:, GATE_SLOTS * h:GATE_SLOTS * h + 1], full) for h in range(2)]
            eg = [jnp.exp(x) for x in gc]
            gcol_s[rows, :] = jnp.where(left, gc[0], gc[1])
            bcol = jnp.where(left, beta[0], beta[1])
            bcol_s[rows, :] = bcol
            begcol_s[rows, :] = bcol * jnp.where(left, eg[0], eg[1])
            gl = [x[c - 1:c, :] for x in gc]
            for h in range(2):
                gl_s[job, :, h * LANES:(h + 1) * LANES] = jnp.exp(gl[h])

            q = l2n(conv_silu(0, r0)) * (DK ** -0.5)
            k = l2n(conv_silu(1, r0))
            k16 = k.astype(BF16)
            qk_s[job, 0:c, :] = q.astype(BF16)
            qk_s[job, c:2 * c, :] = k16
            kk_s[job, 0:c, :] = k16
            kk_s[job, c:2 * c, :] = k16
            for h in range(2):
                wq_s[job, c:2 * c, h * DK:(h + 1) * DK] = (q * eg[h]).astype(BF16)
                kd_s[job, h * c:(h + 1) * c, :] = (k * jnp.exp(gl[h] - gc[h])).astype(BF16)
                kbd_s[job, h * c:(h + 1) * c, h * LANES:(h + 1) * LANES] = k16
                kbd_s[job, h * c:(h + 1) * c, (1 - h) * LANES:(2 - h) * LANES] = zero_k
                vbd_s[job, h * c:(h + 1) * c, h * LANES:(h + 1) * LANES] = conv_silu(2 + h, r0).astype(BF16)
                vbd_s[job, h * c:(h + 1) * c, (1 - h) * LANES:(2 - h) * LANES] = zero_k

    for g in range(heads):
        prep(g)

    r = lax.broadcasted_iota(jnp.int32, (c, LANES), 0)
    sloc = lax.broadcasted_iota(jnp.int32, (c, LANES), 1) & (c - 1)
    causal = r >= sloc
    strict = r > sloc
    diag = r == sloc

    def as_row(x):
        return jnp.sum(jnp.where(diag, x, 0.0), axis=0, keepdims=True)

    def intra(i):
        js = [i * unroll + j for j in range(unroll)]
        sls = [slice(j * c, (j + 1) * c) for j in js]
        gcol = [gcol_s[sl, :] for sl in sls]
        bcl = [bcol_s[sl, :] for sl in sls]
        begcol = [begcol_s[sl, :] for sl in sls]
        decay = [jnp.where(causal, jnp.exp(jnp.where(causal, x - as_row(x), 0.0)), 0.0) for x in gcol]
        qkk = [lax.dot_general(qk_s[j], kk_s[j], (((1,), (1,)), ((), ())), preferred_element_type=F32)
               for j in js]
        for j, x, d in zip(js, qkk, decay):
            a_s[j] = (x[:c] * d).astype(BF16)
        tm = _inv_unit_lower([jnp.where(strict, x[c:] * d * b, 0.0) for x, d, b in zip(qkk, decay, bcl)])
        w = [jnp.dot((x * as_row(gg)).astype(BF16), kbd_s[j], preferred_element_type=F32)
             for j, x, gg in zip(js, tm, begcol)]
        u = [jnp.dot((x * as_row(b)).astype(BF16), vbd_s[j], preferred_element_type=F32)
             for j, x, b in zip(js, tm, bcl)]
        for j, wn, un in zip(js, w, u):
            u_s[j] = un
            wq_s[j, 0:c, :] = wn.astype(BF16)

    for i in range(heads * n // unroll):
        intra(i)

    zero_c = jnp.zeros((c, DV), BF16)
    zero_s = jnp.zeros((DK, DV), BF16)

    def blockdiag(x, zero):
        return jnp.concatenate([jnp.concatenate([x[:, :DV], zero], axis=1),
                                jnp.concatenate([zero, x[:, DV:]], axis=1)], axis=0)

    def scan(nn, carry):
        js = [nn * heads + g for g in range(heads)]
        sp = [st_s[g] for g in range(heads)]
        r1 = [jnp.dot(wq_s[j], blockdiag(x.astype(BF16), zero_s), preferred_element_type=F32)
              for j, x in zip(js, sp)]
        vnbd = [blockdiag((u_s[j] - x[0:c]).astype(BF16), zero_c) for j, x in zip(js, r1)]
        av = [jnp.dot(a_s[j], v, preferred_element_type=F32) for j, v in zip(js, vnbd)]
        upd = [lax.dot_general(kd_s[j], v, (((0,), (0,)), ((), ())), preferred_element_type=F32)
               for j, v in zip(js, vnbd)]
        rows = pl.ds(pl.multiple_of(nn * c, c), c)
        for g, j in enumerate(js):
            st_s[g] = sp[g] * gl_s[j] + upd[g]
            for h in range(2):
                hv = 2 * g + h
                o = r1[g][c:2 * c, h * DV:(h + 1) * DV] + av[g][:, h * DV:(h + 1) * DV]
                zf = z_ref[hv, rows, :]
                o = (o * lax.rsqrt(jnp.mean(o * o, axis=-1, keepdims=True) + EPS) * nw_ref[...]
                     * (zf * jax.nn.sigmoid(zf)))
                o_ref[rows, hv * DV:(hv + 1) * DV] = o.astype(o_ref.dtype)
        return carry

    if n == 1:
        scan(0, 0)
    else:
        lax.fori_loop(0, n, scan, 0)

    for g in range(heads):
        for h in range(2):
            s_ref[0, 2 * g + h] = st_s[g, :, h * DV:(h + 1) * DV]


def _gdn_core(ph, hist, cw, gates, s0, nw, *, b, t, nk, nv, tt, heads, unroll):
    c = CHUNK
    assert t % tt == 0 and tt % c == 0 and nk % heads == 0 and nv == 2 * nk and DK == LANES and DV == LANES
    assert (heads * tt // c) % unroll == 0
    m = b * t
    nt = t // tt
    jobs = heads * tt // c
    vh = 2 * heads
    off_k, off_v, off_z = nk // heads, 2 * nk // vh, (2 * nk + nv) // vh
    seq = lambda off: (lambda bi, h, ti: (off + h, bi * nt + ti, 0))
    hst = lambda off: (lambda bi, h, ti: (off + h, bi, 0, 0))
    cwt = lambda off: (lambda bi, h, ti: (off + h, 0, 0))
    f32 = lambda *shape: pltpu.VMEM(shape, F32)
    bf16 = lambda *shape: pltpu.VMEM(shape, BF16)
    return pl.pallas_call(
        functools.partial(_gdn_kernel, tt=tt, c=c, heads=heads, unroll=unroll),
        grid=(b, nk // heads, nt),
        in_specs=[
            pl.BlockSpec((heads, tt, LANES), seq(0)), pl.BlockSpec((heads, tt, LANES), seq(off_k)),
            pl.BlockSpec((vh, tt, LANES), seq(off_v)), pl.BlockSpec((vh, tt, LANES), seq(off_z)),
            pl.BlockSpec((heads, 1, HIST_ROWS, LANES), hst(0)), pl.BlockSpec((heads, 1, HIST_ROWS, LANES), hst(off_k)),
            pl.BlockSpec((vh, 1, HIST_ROWS, LANES), hst(off_v)),
            pl.BlockSpec((heads, HIST_ROWS, LANES), cwt(0)), pl.BlockSpec((heads, HIST_ROWS, LANES), cwt(off_k)),
            pl.BlockSpec((vh, HIST_ROWS, LANES), cwt(off_v)),
            pl.BlockSpec((tt, LANES), lambda bi, h, ti: (bi * nt + ti, 0)),
            pl.BlockSpec((1, vh, DK, DV), lambda bi, h, ti: (bi, h, 0, 0)),
            pl.BlockSpec((1, DV), lambda bi, h, ti: (0, 0)),
        ],
        out_specs=[
            pl.BlockSpec((tt, vh * DV), lambda bi, h, ti: (bi * nt + ti, h)),
            pl.BlockSpec((1, vh, DK, DV), lambda bi, h, ti: (bi, h, 0, 0)),
        ],
        out_shape=[
            jax.ShapeDtypeStruct((m, nv * DV), BF16),
            jax.ShapeDtypeStruct((b, nv, DK, DV), F32),
        ],
        scratch_shapes=[
            f32(4, tt + HIST_ROWS, LANES), f32(heads, 4, HIST_ROWS, LANES),
            f32(heads * tt, LANES), f32(heads * tt, LANES), f32(heads * tt, LANES), f32(jobs, 1, 2 * LANES),
            bf16(jobs, 2 * c, LANES), bf16(jobs, 2 * c, LANES), bf16(jobs, 2 * c, 2 * LANES),
            bf16(jobs, 2 * c, 2 * LANES), bf16(jobs, 2 * c, LANES), bf16(jobs, 2 * c, 2 * LANES),
            f32(jobs, c, 2 * DV), bf16(jobs, c, LANES), f32(heads, DK, 2 * DV),
        ],
        compiler_params=_cparams(("parallel", "parallel", "arbitrary")),
        name="gdn_core",
    )(ph, ph, ph, ph, hist, hist, hist, cw, cw, cw, gates, s0, nw)


def _mm_res_kernel(a_ref, w_ref, r_ref, o_ref):
    o_ref[...] = r_ref[...] + jnp.dot(a_ref[...], w_ref[...], preferred_element_type=F32)


def _mm_residual(a, w, res, *, layer, tm, tn):
    m, k = a.shape
    n = w.shape[2]
    return pl.pallas_call(
        _mm_res_kernel,
        grid=(m // tm, n // tn),
        in_specs=[
            pl.BlockSpec((tm, k), lambda i, j: (i, 0)),
            pl.BlockSpec((None, k, tn), lambda i, j: (layer, 0, j)),
            pl.BlockSpec((tm, tn), lambda i, j: (i, j)),
        ],
        out_specs=pl.BlockSpec((tm, tn), lambda i, j: (i, j)),
        out_shape=jax.ShapeDtypeStruct((m, n), F32),
        compiler_params=_cparams(("parallel", "arbitrary")),
        name="mm_residual",
    )(a, w, res)


def _ffn_kernel(x_ref, nw_ref, wg_ref, wu_ref, wd_ref, fw_ref, o_ref, h_ref, *, final):
    j = pl.program_id(1)

    @pl.when(j == 0)
    def _():
        x = x_ref[...]
        h_ref[...] = _rms(x, nw_ref[...]).astype(BF16)
        o_ref[...] = x

    h = h_ref[...]
    g = jnp.dot(h, wg_ref[...], preferred_element_type=F32)
    u = jnp.dot(h, wu_ref[...], preferred_element_type=F32)
    act = (g * jax.nn.sigmoid(g) * u).astype(BF16)
    o_ref[...] += jnp.dot(act, wd_ref[...], preferred_element_type=F32)

    if final:
        @pl.when(j == pl.num_programs(1) - 1)
        def _():
            o_ref[...] = _rms(o_ref[...], fw_ref[...])


def _ffn(x, nw, wgu, wd, fw, *, layer, tm, tf, final):
    m, d = x.shape
    f = wd.shape[1]
    nf = f // tf
    return pl.pallas_call(
        functools.partial(_ffn_kernel, final=final),
        grid=(m // tm, nf),
        in_specs=[
            pl.BlockSpec((tm, d), lambda i, j: (i, 0)),
            pl.BlockSpec((1, d), lambda i, j: (0, 0)),
            pl.BlockSpec((None, d, tf), lambda i, j: (layer, 0, j)),
            pl.BlockSpec((None, d, tf), lambda i, j: (layer, 0, nf + j)),
            pl.BlockSpec((None, tf, d), lambda i, j: (layer, j, 0)),
            pl.BlockSpec((1, d), lambda i, j: (0, 0)),
        ],
        out_specs=pl.BlockSpec((tm, d), lambda i, j: (i, 0)),
        out_shape=jax.ShapeDtypeStruct((m, d), F32),
        scratch_shapes=[pltpu.VMEM((tm, d), BF16)],
        compiler_params=_cparams(("parallel", "arbitrary")),
        name="ffn",
    )(x, nw, wgu, wgu, wd, fw)


def _pool_kernel(x_ref, nw_ref, hist_ref, pw_ref, sc_ref, o_ref, hs_ref, hbuf, *, tt, past):
    ti = pl.program_id(1)

    @pl.when(ti == 0)
    def _():
        hbuf[0:POOL_PAD, :] = hist_ref[0]

    @pl.when(ti > 0)
    def _():
        hbuf[0:POOL_PAD, :] = hbuf[tt:tt + POOL_PAD, :]

    x = x_ref[...]
    h = _rms(x, nw_ref[...])
    hbuf[POOL_PAD:POOL_PAD + tt, :] = h
    tg = ti * tt + lax.broadcasted_iota(jnp.int32, (tt, 1), 0)
    gcw = x.shape[1] // len(POOL_WINDOWS)
    for gi, wlen in enumerate(POOL_WINDOWS):
        cols = slice(gi * gcw, (gi + 1) * gcw)
        acc = h[:, cols]
        for i in range(1, wlen):
            acc = acc + hbuf[POOL_PAD - i:POOL_PAD - i + tt, cols]
        cnt = jnp.minimum(wlen, past + tg + 1).astype(F32)
        dlt = acc / cnt - h[:, cols]
        y = jnp.dot(dlt.astype(BF16), pw_ref[gi], preferred_element_type=F32)
        o_ref[:, cols] = x[:, cols] + y * sc_ref[:, cols]
    hs_ref[0] = hbuf[tt:tt + POOL_PAD, :]


def _pool_mixer(x, nw, hist, pw, sc, *, b, t, tt, past):
    m, d = x.shape
    nt = t // tt
    g, gcw, _ = pw.shape
    return pl.pallas_call(
        functools.partial(_pool_kernel, tt=tt, past=past),
        grid=(b, nt),
        in_specs=[
            pl.BlockSpec((tt, d), lambda bi, ti: (bi * nt + ti, 0)),
            pl.BlockSpec((1, d), lambda bi, ti: (0, 0)),
            pl.BlockSpec((1, POOL_PAD, d), lambda bi, ti: (bi, 0, 0)),
            pl.BlockSpec((g, gcw, gcw), lambda bi, ti: (0, 0, 0)),
            pl.BlockSpec((1, d), lambda bi, ti: (0, 0)),
        ],
        out_specs=[
            pl.BlockSpec((tt, d), lambda bi, ti: (bi * nt + ti, 0)),
            pl.BlockSpec((1, POOL_PAD, d), lambda bi, ti: (bi, 0, 0)),
        ],
        out_shape=[
            jax.ShapeDtypeStruct((m, d), F32),
            jax.ShapeDtypeStruct((b, POOL_PAD, d), F32),
        ],
        scratch_shapes=[pltpu.VMEM((tt + POOL_PAD, d), F32)],
        compiler_params=_cparams(("parallel", "arbitrary")),
        name="pool_mixer",
    )(x, nw, hist, pw, sc)


def _prep_weights(norm_mix_w, norm_ffn_w, final_norm_w, gdn_w_in, gdn_conv_w, gdn_A_log, gdn_dt_bias, gdn_norm_w,
                  gdn_w_out, pool_w, pool_scale, ffn_w_gu, ffn_w_down):
    n_a, d, _ = gdn_w_in.shape
    nv = gdn_A_log.shape[1]
    val_dim = nv * DV
    qkv_dim = gdn_conv_w.shape[2]
    main = qkv_dim + val_dim
    zeros = jnp.zeros((n_a, d, nv), F32)
    w_gate = jnp.stack([gdn_w_in[:, :, main:main + nv], gdn_w_in[:, :, main + nv:main + 2 * nv], zeros, zeros],
                       axis=-1).reshape(n_a, d, nv * GATE_SLOTS).astype(BF16)
    zv = jnp.zeros((n_a, nv), F32)
    alog = jnp.stack([zv, gdn_A_log.astype(F32), zv, zv], axis=-1).reshape(n_a, 1, nv * GATE_SLOTS)
    dtb = jnp.stack([zv, gdn_dt_bias.astype(F32), zv, zv], axis=-1).reshape(n_a, 1, nv * GATE_SLOTS)
    nh = qkv_dim // LANES
    cw = gdn_conv_w.reshape(n_a, CONV_W, nh, LANES).transpose(0, 2, 1, 3)
    cw = jnp.pad(cw, ((0, 0), (0, 0), (0, HIST_ROWS - CONV_W), (0, 0)))
    return dict(
        norm_mix=norm_mix_w[:, None, :], norm_ffn=norm_ffn_w[:, None, :], final=final_norm_w[None, :],
        w_in=gdn_w_in.astype(BF16), n_main=main, w_gate=w_gate, alog=alog, dtb=dtb, cw=cw,
        gnorm=gdn_norm_w[:, None, :], w_out=gdn_w_out.astype(BF16), pool_w=pool_w.astype(BF16),
        pool_scale=pool_scale[:, None, :], w_gu=ffn_w_gu.astype(BF16), w_down=ffn_w_down.astype(BF16),
    )


def _gdn_layer(x, conv_hist, s_hist, wts, i, j, *, b, t):
    m, d = x.shape
    nv = s_hist.shape[1]
    nk = nv // 2
    qkv_dim = conv_hist.shape[-1]
    nh = qkv_dim // LANES
    if t % GDN_TILE == 0:
        tp, tt, heads = t, GDN_TILE, GDN_HEADS
        xp = x
    else:
        tp = -(-t // CHUNK) * CHUNK
        tt, heads = tp, nk
        xp = jnp.pad(x.reshape(b, t, d), ((0, 0), (0, tp - t), (0, 0))).reshape(b * tp, d)
    mp = b * tp
    ph, ba = _norm_proj(xp, wts["norm_mix"][i], wts["w_in"], wts["w_gate"][j], layer=j, n=wts["n_main"],
                        tm=min(1024, mp), tn=512)
    gates = _gates(ba, wts["alog"][j], wts["dtb"][j], c=CHUNK, tt=min(mp, 2048), valid=t, period=tp)
    hist = conv_hist.reshape(b, CONV_W - 1, nh, LANES).transpose(2, 0, 1, 3)
    hist = jnp.pad(hist, ((0, 0), (0, 0), (HIST_ROWS - (CONV_W - 1), 0), (0, 0)))
    o, s_new = _gdn_core(ph, hist, wts["cw"][j], gates, s_hist.astype(F32), wts["gnorm"][j], b=b, t=tp, nk=nk,
                         nv=nv, tt=tt, heads=heads, unroll=min(GDN_UNROLL, heads * tt // CHUNK))
    if tp != t:
        o = o.reshape(b, tp, o.shape[-1])[:, :t].reshape(m, o.shape[-1])
    x = _mm_residual(o, wts["w_out"], x, layer=j, tm=min(1024, m), tn=512)
    tail = ph.reshape(ph.shape[0], b, tp, LANES)[:nh, :, t - (CONV_W - 1):t, :]
    return x, tail.transpose(1, 2, 0, 3).reshape(b, CONV_W - 1, qkv_dim), s_new


def _trunk(x3, conv_hist, s_hist, pool_hist, wts, depth):
    b, t, d = x3.shape
    m = b * t
    x = x3.reshape(m, d)
    past = pool_hist.shape[2]
    assert t >= CONV_W - 1 and t >= POOL_PAD - 1
    new_conv, new_s, new_pool = [], [], []
    for i in range(depth):
        j = i // 2
        if i % 2 == 0:
            x, conv_new, s_new = _gdn_layer(x, conv_hist[j], s_hist[j], wts, i, j, b=b, t=t)
            new_conv.append(conv_new)
            new_s.append(s_new)
        else:
            hist = jnp.pad(pool_hist[j], ((0, 0), (POOL_PAD - past, 0), (0, 0)))
            x, hs = _pool_mixer(x, wts["norm_mix"][i], hist, wts["pool_w"][j], wts["pool_scale"][j],
                                b=b, t=t, tt=min(t, 256), past=past)
            new_pool.append(hs[:, 1:, :])
        x = _ffn(x, wts["norm_ffn"][i], wts["w_gu"], wts["w_down"], wts["final"], layer=i, tm=min(512, m), tf=512,
                 final=(i == depth - 1))
    return x.reshape(b, t, d), jnp.stack(new_conv), jnp.stack(new_s), jnp.stack(new_pool)


def kernel(x_prompt, x_sample, state_gdn_conv, state_gdn_S, state_pool, norm_mix_w, norm_ffn_w, final_norm_w,
           gdn_w_in, gdn_conv_w, gdn_A_log, gdn_dt_bias, gdn_norm_w, gdn_w_out, pool_w, pool_scale, ffn_w_gu,
           ffn_w_down):
    depth = norm_mix_w.shape[0]
    n_a, n_b = state_gdn_conv.shape[0], state_pool.shape[0]
    bp = x_prompt.shape[0]
    wts = _prep_weights(norm_mix_w, norm_ffn_w, final_norm_w, gdn_w_in, gdn_conv_w, gdn_A_log, gdn_dt_bias,
                        gdn_norm_w, gdn_w_out, pool_w, pool_scale, ffn_w_gu, ffn_w_down)
    conv0 = jnp.zeros((n_a, bp) + state_gdn_conv.shape[2:], x_prompt.dtype)
    s00 = jnp.zeros((n_a, bp) + state_gdn_S.shape[2:], F32)
    pool0 = jnp.zeros((n_b, bp, 0, x_prompt.shape[-1]), x_prompt.dtype)
    y_p, conv_p, s_p, pool_p = _trunk(x_prompt, conv0, s00, pool0, wts, depth)
    y_s, conv_s, s_s, pool_s = _trunk(x_sample, state_gdn_conv, state_gdn_S, state_pool, wts, depth)
    return (y_p, y_s, conv_p, s_p.astype(state_gdn_S.dtype), pool_p, conv_s, s_s.astype(state_gdn_S.dtype), pool_s)
```

```python
import functools

import jax
import jax.numpy as jnp
from jax import lax
from jax.experimental import pallas as pl
from jax.experimental.pallas import tpu as pltpu

F32 = jnp.float32
BF16 = jnp.bfloat16

EPS = 1e-6
LANES = 128
DK = 128
DV = 128
CONV_W = 4
HIST_ROWS = 8
CHUNK = 64
INV_BLOCK = 16
GDN_TILE = 256
GDN_HEADS = 8
GDN_UNROLL = 16
POOL_WINDOWS = (2, 4, 8, 16)
POOL_PAD = 16
GATE_SLOTS = 4
VMEM_LIMIT = 56 * 1024 * 1024


def _cparams(sem):
    return pltpu.CompilerParams(dimension_semantics=sem, vmem_limit_bytes=VMEM_LIMIT)


def _rms(x, w):
    ms = jnp.mean(x * x, axis=-1, keepdims=True)
    return x * lax.rsqrt(ms + EPS) * w


def _mm(a, b):
    return jnp.dot(a.astype(BF16), b.astype(BF16), preferred_element_type=F32)


def _norm_proj_kernel(x_ref, nw_ref, w_ref, wg_ref, o_ref, g_ref, h_ref, *, tn):
    @pl.when(pl.program_id(1) == 0)
    def _():
        h = _rms(x_ref[...], nw_ref[...]).astype(BF16)
        h_ref[...] = h
        g_ref[...] = jnp.dot(h, wg_ref[...], preferred_element_type=F32)

    acc = jnp.dot(h_ref[...], w_ref[...], preferred_element_type=F32)
    for c in range(tn // LANES):
        o_ref[c] = acc[:, c * LANES:(c + 1) * LANES]


def _norm_proj(x, nw, w, wg, *, layer, n, tm, tn):
    m, d = x.shape
    assert n % tn == 0 and n <= w.shape[2] and m % tm == 0
    return pl.pallas_call(
        functools.partial(_norm_proj_kernel, tn=tn),
        grid=(m // tm, n // tn),
        in_specs=[
            pl.BlockSpec((tm, d), lambda i, j: (i, 0)),
            pl.BlockSpec((1, d), lambda i, j: (0, 0)),
            pl.BlockSpec((None, d, tn), lambda i, j: (layer, 0, j)),
            pl.BlockSpec((d, LANES), lambda i, j: (0, 0)),
        ],
        out_specs=[
            pl.BlockSpec((tn // LANES, tm, LANES), lambda i, j: (j, i, 0)),
            pl.BlockSpec((tm, LANES), lambda i, j: (i, 0)),
        ],
        out_shape=[
            jax.ShapeDtypeStruct((n // LANES, m, LANES), F32),
            jax.ShapeDtypeStruct((m, LANES), F32),
        ],
        scratch_shapes=[pltpu.VMEM((tm, d), BF16)],
        compiler_params=_cparams(("parallel", "arbitrary")),
        name="norm_proj",
    )(x, nw, w, wg)


def _gates_kernel(ba_ref, alog_ref, dtb_ref, tri_ref, o_ref, *, c, valid, period):
    x = ba_ref[...]
    beta = jax.nn.sigmoid(x)
    y = x + dtb_ref[...]
    softplus = jnp.maximum(y, 0.0) + jnp.log1p(jnp.exp(-jnp.abs(y)))
    g = -jnp.exp(alog_ref[...]) * softplus
    if valid != period:
        assert period & (period - 1) == 0 and x.shape[0] % period == 0
        row = lax.broadcasted_iota(jnp.int32, x.shape, 0) & (period - 1)
        beta = jnp.where(row < valid, beta, 0.0)
        g = jnp.where(row < valid, g, 0.0)
    lane = lax.broadcasted_iota(jnp.int32, (c, x.shape[1]), 1)
    is_beta = (lane & (GATE_SLOTS - 1)) == 0
    tri = tri_ref[...]
    for n in range(x.shape[0] // c):
        sl = slice(n * c, (n + 1) * c)
        gc = jnp.dot(tri, g[sl], preferred_element_type=F32, precision=lax.Precision.HIGHEST)
        o_ref[sl, :] = jnp.where(is_beta, beta[sl], gc)


def _gates(ba, alog, dtb, *, c, tt, valid, period):
    m = ba.shape[0]
    tri = jnp.tril(jnp.ones((c, c), F32))
    return pl.pallas_call(
        functools.partial(_gates_kernel, c=c, valid=valid, period=period),
        grid=(m // tt,),
        in_specs=[
            pl.BlockSpec((tt, LANES), lambda i: (i, 0)),
            pl.BlockSpec((1, LANES), lambda i: (0, 0)),
            pl.BlockSpec((1, LANES), lambda i: (0, 0)),
            pl.BlockSpec((c, c), lambda i: (0, 0)),
        ],
        out_specs=pl.BlockSpec((tt, LANES), lambda i: (i, 0)),
        out_shape=jax.ShapeDtypeStruct((m, LANES), F32),
        compiler_params=_cparams(("parallel",)),
        name="gates",
    )(ba, alog, dtb, tri)


def _inv_unit_lower(l_mats):
    c, w = l_mats[0].shape
    npack = w // c
    r = lax.broadcasted_iota(jnp.int32, (c, w), 0)
    lane = lax.broadcasted_iota(jnp.int32, (c, w), 1)
    s = lane & (c - 1)
    shift = INV_BLOCK.bit_length() - 1
    same = lax.shift_right_logical(r, shift) == lax.shift_right_logical(s, shift)
    eye = jnp.where(r == s, 1.0, 0.0).astype(BF16)
    zero = jnp.zeros((c, w), BF16)
    part = [lax.shift_right_logical(lane, c.bit_length() - 1) == p for p in range(npack)]

    def mm(xs, ys):
        if npack > 1:
            ys = [jnp.concatenate([jnp.where(part[p], y, zero) for p in range(npack)], axis=0) for y in ys]
        return [jnp.dot(x, y, preferred_element_type=F32).astype(BF16) for x, y in zip(xs, ys)]

    def stack(xs, ys):
        return [jnp.concatenate([x, y], axis=0) for x, y in zip(xs, ys)]

    d1 = [jnp.where(same, l, 0.0).astype(BF16) for l in l_mats]
    d2 = mm(d1, d1)
    t = mm(stack(d2, d1), d2)
    d4 = [x[:c] for x in t]
    y = [eye - a + b - x[c:] for a, b, x in zip(d1, d2, t)]
    t = mm(stack(d4, y), d4)
    y = [a + x[c:] for a, x in zip(y, t)]
    x0 = [a + b for a, b in zip(y, mm(y, [x[:c] for x in t]))]
    nblk = c // INV_BLOCK
    if nblk == 1:
        return x0
    e = [jnp.where(same, 0.0, l).astype(BF16) for l in l_mats]
    nmat = mm(x0, e)
    z = [eye - a for a in nmat]
    if nblk > 2:
        assert nblk == 4
        n2 = mm(nmat, nmat)
        n3 = mm(nmat, n2)
        z = [a + b - x for a, b, x in zip(z, n2, n3)]
    return mm(z, x0)


def _gdn_kernel(q_ref, k_ref, v_ref, z_ref, hq_ref, hk_ref, hv_ref, cq_ref, ck_ref, cv_ref, g_ref, e_ref, s0_ref,
                nw_ref, o_ref, s_ref, xc_s, hist_s, gcol_s, bcol_s, begcol_s, gl_s, qk_s, kk_s, kbd_s, vbd_s,
                kd_s, wq_s, u_s, a_s, st_s, *, tt, c, heads, unroll):
    n = tt // c
    assert 2 * c == LANES

    @pl.when(pl.program_id(2) == 0)
    def _():
        for g in range(heads):
            hist_s[g, 0] = hq_ref[g, 0]
            hist_s[g, 1] = hk_ref[g, 0]
            for h in range(2):
                hist_s[g, 2 + h] = hv_ref[2 * g + h, 0]
                st_s[g, :, h * DV:(h + 1) * DV] = s0_ref[0, 2 * g + h]

    def l2n(x):
        return x * lax.rsqrt(jnp.sum(x * x, axis=-1, keepdims=True) + EPS)

    zero_k = jnp.zeros((c, LANES), BF16)
    base = HIST_ROWS - (CONV_W - 1)
    ghl = []
    for nn in range(n):
        graw = g_ref[nn * c:(nn + 1) * c, :]
        hi = graw.astype(BF16)
        ghl.append(jnp.concatenate([hi, (graw - hi.astype(F32)).astype(BF16)], axis=1))

    for g in range(heads):
        xin = (q_ref[g], k_ref[g], v_ref[2 * g], v_ref[2 * g + 1])
        for slot in range(4):
            xc_s[g, slot, 0:HIST_ROWS, :] = hist_s[g, slot]
            xc_s[g, slot, HIST_ROWS:HIST_ROWS + tt, :] = xin[slot]
            hist_s[g, slot] = xc_s[g, slot, tt:tt + HIST_ROWS, :]

    ex_all = [jnp.dot(ghl[nn], e_ref[g], preferred_element_type=F32) for nn in range(n) for g in range(heads)]

    def prep(job):
        nn, g = divmod(job, heads)
        r0 = nn * c
        rows = slice(job * c, (job + 1) * c)
        cws = (cq_ref[g], ck_ref[g], cv_ref[2 * g], cv_ref[2 * g + 1])

        def conv_silu(slot):
            acc = xc_s[g, slot, r0 + base:r0 + base + c, :] * cws[slot][0:1, :]
            for j in range(1, CONV_W):
                acc = acc + xc_s[g, slot, r0 + base + j:r0 + base + j + c, :] * cws[slot][j:j + 1, :]
            return acc * jax.nn.sigmoid(acc)

        ex = ex_all[job]
        gcol, bcol = ex[:, 0:LANES], ex[:, LANES:2 * LANES]
        gc = [ex[:, (2 + h) * LANES:(3 + h) * LANES] for h in range(2)]
        eg = [jnp.exp(x) for x in gc]
        gcol_s[rows, :] = gcol
        bcol_s[rows, :] = bcol
        begcol_s[rows, :] = bcol * jnp.exp(gcol)
        gl = [x[c - 1:c, :] for x in gc]
        for h in range(2):
            gl_s[job, :, h * LANES:(h + 1) * LANES] = jnp.exp(gl[h])

        q = l2n(conv_silu(0)) * (DK ** -0.5)
        k = l2n(conv_silu(1))
        k16 = k.astype(BF16)
        qk_s[job, 0:c, :] = q.astype(BF16)
        qk_s[job, c:2 * c, :] = k16
        kk_s[job, 0:c, :] = k16
        kk_s[job, c:2 * c, :] = k16
        for h in range(2):
            wq_s[job, c:2 * c, h * DK:(h + 1) * DK] = (q * eg[h]).astype(BF16)
            kd_s[job, h * c:(h + 1) * c, :] = (k * jnp.exp(gl[h] - gc[h])).astype(BF16)
            kbd_s[job, h * c:(h + 1) * c, h * LANES:(h + 1) * LANES] = k16
            kbd_s[job, h * c:(h + 1) * c, (1 - h) * LANES:(2 - h) * LANES] = zero_k
            vbd_s[job, h * c:(h + 1) * c, h * LANES:(h + 1) * LANES] = conv_silu(2 + h).astype(BF16)
            vbd_s[job, h * c:(h + 1) * c, (1 - h) * LANES:(2 - h) * LANES] = zero_k

    r = lax.broadcasted_iota(jnp.int32, (c, LANES), 0)
    sloc = lax.broadcasted_iota(jnp.int32, (c, LANES), 1) & (c - 1)
    causal = r >= sloc
    strict = r > sloc
    diag = r == sloc

    def as_row(x):
        return jnp.sum(jnp.where(diag, x, 0.0), axis=0, keepdims=True)

    def intra(i):
        js = [i * unroll + j for j in range(unroll)]
        sls = [slice(j * c, (j + 1) * c) for j in js]
        gcol = [gcol_s[sl, :] for sl in sls]
        bcl = [bcol_s[sl, :] for sl in sls]
        begcol = [begcol_s[sl, :] for sl in sls]
        decay = [jnp.where(causal, jnp.exp(jnp.where(causal, x - as_row(x), 0.0)), 0.0) for x in gcol]
        qkk = [lax.dot_general(qk_s[j], kk_s[j], (((1,), (1,)), ((), ())), preferred_element_type=F32)
               for j in js]
        for j, x, d in zip(js, qkk, decay):
            a_s[j] = (x[:c] * d).astype(BF16)
        tm = _inv_unit_lower([jnp.where(strict, x[c:] * d * b, 0.0) for x, d, b in zip(qkk, decay, bcl)])
        w = [jnp.dot(x * as_row(gg).astype(BF16), kbd_s[j], preferred_element_type=F32)
             for j, x, gg in zip(js, tm, begcol)]
        u = [jnp.dot(x * as_row(b).astype(BF16), vbd_s[j], preferred_element_type=F32)
             for j, x, b in zip(js, tm, bcl)]
        for j, wn, un in zip(js, w, u):
            u_s[j] = un
            wq_s[j, 0:c, :] = wn.astype(BF16)

    for i in range(heads * n // unroll):
        for job in range(i * unroll, (i + 1) * unroll):
            prep(job)
        intra(i)

    zero_c = jnp.zeros((c, DV), BF16)
    zero_s = jnp.zeros((DK, DV), BF16)

    def blockdiag(x, zero):
        return jnp.concatenate([jnp.concatenate([x[:, :DV], zero], axis=1),
                                jnp.concatenate([zero, x[:, DV:]], axis=1)], axis=0)

    def scan(nn):
        js = [nn * heads + g for g in range(heads)]
        sp = [st_s[g] for g in range(heads)]
        r1 = [jnp.dot(wq_s[j], blockdiag(x.astype(BF16), zero_s), preferred_element_type=F32)
              for j, x in zip(js, sp)]
        vnbd = [blockdiag((u_s[j] - x[0:c]).astype(BF16), zero_c) for j, x in zip(js, r1)]
        av = [jnp.dot(a_s[j], v, preferred_element_type=F32) for j, v in zip(js, vnbd)]
        upd = [lax.dot_general(kd_s[j], v, (((0,), (0,)), ((), ())), preferred_element_type=F32)
               for j, v in zip(js, vnbd)]
        rows = slice(nn * c, (nn + 1) * c)
        for g, j in enumerate(js):
            st_s[g] = sp[g] * gl_s[j] + upd[g]
            for h in range(2):
                hv = 2 * g + h
                o = r1[g][c:2 * c, h * DV:(h + 1) * DV] + av[g][:, h * DV:(h + 1) * DV]
                zf = z_ref[hv, rows, :]
                o = (o * lax.rsqrt(jnp.mean(o * o, axis=-1, keepdims=True) + EPS) * nw_ref[...]
                     * (zf * jax.nn.sigmoid(zf)))
                o_ref[rows, hv * DV:(hv + 1) * DV] = o.astype(o_ref.dtype)

    for nn in range(n):
        scan(nn)

    for g in range(heads):
        for h in range(2):
            s_ref[0, 2 * g + h] = st_s[g, :, h * DV:(h + 1) * DV]


def _gate_selectors(nk):
    kh = jnp.arange(nk, dtype=jnp.int32)[:, None, None]
    row = jnp.arange(2 * LANES, dtype=jnp.int32)[None, :, None] & (LANES - 1)
    col = jnp.arange(4 * LANES, dtype=jnp.int32)[None, None, :]
    blk, second = col // LANES, (col % LANES) >= LANES // 2
    head = jnp.where(blk < 2, second.astype(jnp.int32), blk - 2)
    slot = jnp.where(blk == 1, 0, 1)
    return (row == GATE_SLOTS * (2 * kh + head) + slot).astype(BF16)


def _gdn_core(ph, hist, cw, gates, s0, nw, *, b, t, nk, nv, tt, heads, unroll):
    c = CHUNK
    assert t % tt == 0 and tt % c == 0 and nk % heads == 0 and nv == 2 * nk and DK == LANES and DV == LANES
    assert (heads * tt // c) % unroll == 0
    m = b * t
    nt = t // tt
    jobs = heads * tt // c
    vh = 2 * heads
    off_k, off_v, off_z = nk // heads, 2 * nk // vh, (2 * nk + nv) // vh
    seq = lambda off: (lambda bi, h, ti: (off + h, bi * nt + ti, 0))
    hst = lambda off: (lambda bi, h, ti: (off + h, bi, 0, 0))
    cwt = lambda off: (lambda bi, h, ti: (off + h, 0, 0))
    f32 = lambda *shape: pltpu.VMEM(shape, F32)
    bf16 = lambda *shape: pltpu.VMEM(shape, BF16)
    return pl.pallas_call(
        functools.partial(_gdn_kernel, tt=tt, c=c, heads=heads, unroll=unroll),
        grid=(b, nk // heads, nt),
        in_specs=[
            pl.BlockSpec((heads, tt, LANES), seq(0)), pl.BlockSpec((heads, tt, LANES), seq(off_k)),
            pl.BlockSpec((vh, tt, LANES), seq(off_v)), pl.BlockSpec((vh, tt, LANES), seq(off_z)),
            pl.BlockSpec((heads, 1, HIST_ROWS, LANES), hst(0)), pl.BlockSpec((heads, 1, HIST_ROWS, LANES), hst(off_k)),
            pl.BlockSpec((vh, 1, HIST_ROWS, LANES), hst(off_v)),
            pl.BlockSpec((heads, HIST_ROWS, LANES), cwt(0)), pl.BlockSpec((heads, HIST_ROWS, LANES), cwt(off_k)),
            pl.BlockSpec((vh, HIST_ROWS, LANES), cwt(off_v)),
            pl.BlockSpec((tt, LANES), lambda bi, h, ti: (bi * nt + ti, 0)),
            pl.BlockSpec((heads, 2 * LANES, 4 * LANES), lambda bi, h, ti: (h, 0, 0)),
            pl.BlockSpec((1, vh, DK, DV), lambda bi, h, ti: (bi, h, 0, 0)),
            pl.BlockSpec((1, DV), lambda bi, h, ti: (0, 0)),
        ],
        out_specs=[
            pl.BlockSpec((tt, vh * DV), lambda bi, h, ti: (bi * nt + ti, h)),
            pl.BlockSpec((1, vh, DK, DV), lambda bi, h, ti: (bi, h, 0, 0)),
        ],
        out_shape=[
            jax.ShapeDtypeStruct((m, nv * DV), BF16),
            jax.ShapeDtypeStruct((b, nv, DK, DV), F32),
        ],
        scratch_shapes=[
            f32(heads, 4, tt + HIST_ROWS, LANES), f32(heads, 4, HIST_ROWS, LANES),
            f32(heads * tt, LANES), f32(heads * tt, LANES), f32(heads * tt, LANES), f32(jobs, 1, 2 * LANES),
            bf16(jobs, 2 * c, LANES), bf16(jobs, 2 * c, LANES), bf16(jobs, 2 * c, 2 * LANES),
            bf16(jobs, 2 * c, 2 * LANES), bf16(jobs, 2 * c, LANES), bf16(jobs, 2 * c, 2 * LANES),
            f32(jobs, c, 2 * DV), bf16(jobs, c, LANES), f32(heads, DK, 2 * DV),
        ],
        compiler_params=_cparams(("parallel", "parallel", "arbitrary")),
        name="gdn_core",
    )(ph, ph, ph, ph, hist, hist, hist, cw, cw, cw, gates, _gate_selectors(nk), s0, nw)


def _mm_res_kernel(a_ref, w_ref, r_ref, o_ref):
    o_ref[...] = r_ref[...] + jnp.dot(a_ref[...], w_ref[...], preferred_element_type=F32)


def _mm_residual(a, w, res, *, layer, tm, tn):
    m, k = a.shape
    n = w.shape[2]
    return pl.pallas_call(
        _mm_res_kernel,
        grid=(m // tm, n // tn),
        in_specs=[
            pl.BlockSpec((tm, k), lambda i, j: (i, 0)),
            pl.BlockSpec((None, k, tn), lambda i, j: (layer, 0, j)),
            pl.BlockSpec((tm, tn), lambda i, j: (i, j)),
        ],
        out_specs=pl.BlockSpec((tm, tn), lambda i, j: (i, j)),
        out_shape=jax.ShapeDtypeStruct((m, n), F32),
        compiler_params=_cparams(("parallel", "arbitrary")),
        name="mm_residual",
    )(a, w, res)


def _ffn_kernel(x_ref, nw_ref, wg_ref, wu_ref, wd_ref, fw_ref, o_ref, h_ref, *, final):
    j = pl.program_id(1)

    @pl.when(j == 0)
    def _():
        x = x_ref[...]
        h_ref[...] = _rms(x, nw_ref[...]).astype(BF16)
        o_ref[...] = x

    h = h_ref[...]
    g = jnp.dot(h, wg_ref[...], preferred_element_type=F32)
    u = jnp.dot(h, wu_ref[...], preferred_element_type=F32)
    act = (g * jax.nn.sigmoid(g) * u).astype(BF16)
    o_ref[...] += jnp.dot(act, wd_ref[...], preferred_element_type=F32)

    if final:
        @pl.when(j == pl.num_programs(1) - 1)
        def _():
            o_ref[...] = _rms(o_ref[...], fw_ref[...])


def _ffn(x, nw, wgu, wd, fw, *, layer, tm, tf, final):
    m, d = x.shape
    f = wd.shape[1]
    nf = f // tf
    return pl.pallas_call(
        functools.partial(_ffn_kernel, final=final),
        grid=(m // tm, nf),
        in_specs=[
            pl.BlockSpec((tm, d), lambda i, j: (i, 0)),
            pl.BlockSpec((1, d), lambda i, j: (0, 0)),
            pl.BlockSpec((None, d, tf), lambda i, j: (layer, 0, j)),
            pl.BlockSpec((None, d, tf), lambda i, j: (layer, 0, nf + j)),
            pl.BlockSpec((None, tf, d), lambda i, j: (layer, j, 0)),
            pl.BlockSpec((1, d), lambda i, j: (0, 0)),
        ],
        out_specs=pl.BlockSpec((tm, d), lambda i, j: (i, 0)),
        out_shape=jax.ShapeDtypeStruct((m, d), F32),
        scratch_shapes=[pltpu.VMEM((tm, d), BF16)],
        compiler_params=_cparams(("parallel", "arbitrary")),
        name="ffn",
    )(x, nw, wgu, wgu, wd, fw)


def _pool_kernel(x_ref, nw_ref, hist_ref, pw_ref, sc_ref, o_ref, hs_ref, hbuf, *, tt, past):
    ti = pl.program_id(1)

    @pl.when(ti == 0)
    def _():
        hbuf[0:POOL_PAD, :] = hist_ref[0]

    @pl.when(ti > 0)
    def _():
        hbuf[0:POOL_PAD, :] = hbuf[tt:tt + POOL_PAD, :]

    x = x_ref[...]
    h = _rms(x, nw_ref[...])
    hbuf[POOL_PAD:POOL_PAD + tt, :] = h
    tg = ti * tt + lax.broadcasted_iota(jnp.int32, (tt, 1), 0)
    gcw = x.shape[1] // len(POOL_WINDOWS)
    for gi, wlen in enumerate(POOL_WINDOWS):
        cols = slice(gi * gcw, (gi + 1) * gcw)
        acc = h[:, cols]
        for i in range(1, wlen):
            acc = acc + hbuf[POOL_PAD - i:POOL_PAD - i + tt, cols]
        cnt = jnp.minimum(wlen, past + tg + 1).astype(F32)
        dlt = acc / cnt - h[:, cols]
        y = jnp.dot(dlt.astype(BF16), pw_ref[gi], preferred_element_type=F32)
        o_ref[:, cols] = x[:, cols] + y * sc_ref[:, cols]
    hs_ref[0] = hbuf[tt:tt + POOL_PAD, :]


def _pool_mixer(x, nw, hist, pw, sc, *, b, t, tt, past):
    m, d = x.shape
    nt = t // tt
    g, gcw, _ = pw.shape
    return pl.pallas_call(
        functools.partial(_pool_kernel, tt=tt, past=past),
        grid=(b, nt),
        in_specs=[
            pl.BlockSpec((tt, d), lambda bi, ti: (bi * nt + ti, 0)),
            pl.BlockSpec((1, d), lambda bi, ti: (0, 0)),
            pl.BlockSpec((1, POOL_PAD, d), lambda bi, ti: (bi, 0, 0)),
            pl.BlockSpec((g, gcw, gcw), lambda bi, ti: (0, 0, 0)),
            pl.BlockSpec((1, d), lambda bi, ti: (0, 0)),
        ],
        out_specs=[
            pl.BlockSpec((tt, d), lambda bi, ti: (bi * nt + ti, 0)),
            pl.BlockSpec((1, POOL_PAD, d), lambda bi, ti: (bi, 0, 0)),
        ],
        out_shape=[
            jax.ShapeDtypeStruct((m, d), F32),
            jax.ShapeDtypeStruct((b, POOL_PAD, d), F32),
        ],
        scratch_shapes=[pltpu.VMEM((tt + POOL_PAD, d), F32)],
        compiler_params=_cparams(("parallel", "arbitrary")),
        name="pool_mixer",
    )(x, nw, hist, pw, sc)


def _prep_weights(norm_mix_w, norm_ffn_w, final_norm_w, gdn_w_in, gdn_conv_w, gdn_A_log, gdn_dt_bias, gdn_norm_w,
                  gdn_w_out, pool_w, pool_scale, ffn_w_gu, ffn_w_down):
    n_a, d, _ = gdn_w_in.shape
    nv = gdn_A_log.shape[1]
    val_dim = nv * DV
    qkv_dim = gdn_conv_w.shape[2]
    main = qkv_dim + val_dim
    zeros = jnp.zeros((n_a, d, nv), F32)
    w_gate = jnp.stack([gdn_w_in[:, :, main:main + nv], gdn_w_in[:, :, main + nv:main + 2 * nv], zeros, zeros],
                       axis=-1).reshape(n_a, d, nv * GATE_SLOTS).astype(BF16)
    zv = jnp.zeros((n_a, nv), F32)
    alog = jnp.stack([zv, gdn_A_log.astype(F32), zv, zv], axis=-1).reshape(n_a, 1, nv * GATE_SLOTS)
    dtb = jnp.stack([zv, gdn_dt_bias.astype(F32), zv, zv], axis=-1).reshape(n_a, 1, nv * GATE_SLOTS)
    nh = qkv_dim // LANES
    cw = gdn_conv_w.reshape(n_a, CONV_W, nh, LANES).transpose(0, 2, 1, 3)
    cw = jnp.pad(cw, ((0, 0), (0, 0), (0, HIST_ROWS - CONV_W), (0, 0)))
    return dict(
        norm_mix=norm_mix_w[:, None, :], norm_ffn=norm_ffn_w[:, None, :], final=final_norm_w[None, :],
        w_in=gdn_w_in.astype(BF16), n_main=main, w_gate=w_gate, alog=alog, dtb=dtb, cw=cw,
        gnorm=gdn_norm_w[:, None, :], w_out=gdn_w_out.astype(BF16), pool_w=pool_w.astype(BF16),
        pool_scale=pool_scale[:, None, :], w_gu=ffn_w_gu.astype(BF16), w_down=ffn_w_down.astype(BF16),
    )


def _gdn_layer(x, conv_hist, s_hist, wts, i, j, *, b, t):
    m, d = x.shape
    nv = s_hist.shape[1]
    nk = nv // 2
    qkv_dim = conv_hist.shape[-1]
    nh = qkv_dim // LANES
    if t % GDN_TILE == 0:
        tp, tt, heads = t, GDN_TILE, GDN_HEADS
        xp = x
    else:
        tp = -(-t // CHUNK) * CHUNK
        tt, heads = tp, nk
        xp = jnp.pad(x.reshape(b, t, d), ((0, 0), (0, tp - t), (0, 0))).reshape(b * tp, d)
    mp = b * tp
    ph, ba = _norm_proj(xp, wts["norm_mix"][i], wts["w_in"], wts["w_gate"][j], layer=j, n=wts["n_main"],
                        tm=min(1024, mp), tn=512)
    gates = _gates(ba, wts["alog"][j], wts["dtb"][j], c=CHUNK, tt=min(mp, 2048), valid=t, period=tp)
    hist = conv_hist.reshape(b, CONV_W - 1, nh, LANES).transpose(2, 0, 1, 3)
    hist = jnp.pad(hist, ((0, 0), (0, 0), (HIST_ROWS - (CONV_W - 1), 0), (0, 0)))
    o, s_new = _gdn_core(ph, hist, wts["cw"][j], gates, s_hist.astype(F32), wts["gnorm"][j], b=b, t=tp, nk=nk,
                         nv=nv, tt=tt, heads=heads, unroll=min(GDN_UNROLL, heads * tt // CHUNK))
    if tp != t:
        o = o.reshape(b, tp, o.shape[-1])[:, :t].reshape(m, o.shape[-1])
    x = _mm_residual(o, wts["w_out"], x, layer=j, tm=min(1024, m), tn=512)
    tail = ph.reshape(ph.shape[0], b, tp, LANES)[:nh, :, t - (CONV_W - 1):t, :]
    return x, tail.transpose(1, 2, 0, 3).reshape(b, CONV_W - 1, qkv_dim), s_new


def _trunk(x3, conv_hist, s_hist, pool_hist, wts, depth):
    b, t, d = x3.shape
    m = b * t
    x = x3.reshape(m, d)
    past = pool_hist.shape[2]
    assert t >= CONV_W - 1 and t >= POOL_PAD - 1
    new_conv, new_s, new_pool = [], [], []
    for i in range(depth):
        j = i // 2
        if i % 2 == 0:
            x, conv_new, s_new = _gdn_layer(x, conv_hist[j], s_hist[j], wts, i, j, b=b, t=t)
            new_conv.append(conv_new)
            new_s.append(s_new)
        else:
            hist = jnp.pad(pool_hist[j], ((0, 0), (POOL_PAD - past, 0), (0, 0)))
            x, hs = _pool_mixer(x, wts["norm_mix"][i], hist, wts["pool_w"][j], wts["pool_scale"][j],
                                b=b, t=t, tt=min(t, 256), past=past)
            new_pool.append(hs[:, 1:, :])
        x = _ffn(x, wts["norm_ffn"][i], wts["w_gu"], wts["w_down"], wts["final"], layer=i, tm=min(512, m), tf=512,
                 final=(i == depth - 1))
    return x.reshape(b, t, d), jnp.stack(new_conv), jnp.stack(new_s), jnp.stack(new_pool)


def kernel(x_prompt, x_sample, state_gdn_conv, state_gdn_S, state_pool, norm_mix_w, norm_ffn_w, final_norm_w,
           gdn_w_in, gdn_conv_w, gdn_A_log, gdn_dt_bias, gdn_norm_w, gdn_w_out, pool_w, pool_scale, ffn_w_gu,
           ffn_w_down):
    depth = norm_mix_w.shape[0]
    n_a, n_b = state_gdn_conv.shape[0], state_pool.shape[0]
    bp = x_prompt.shape[0]
    wts = _prep_weights(norm_mix_w, norm_ffn_w, final_norm_w, gdn_w_in, gdn_conv_w, gdn_A_log, gdn_dt_bias,
                        gdn_norm_w, gdn_w_out, pool_w, pool_scale, ffn_w_gu, ffn_w_down)
    conv0 = jnp.zeros((n_a, bp) + state_gdn_conv.shape[2:], x_prompt.dtype)
    s00 = jnp.zeros((n_a, bp) + state_gdn_S.shape[2:], F32)
    pool0 = jnp.zeros((n_b, bp, 0, x_prompt.shape[-1]), x_prompt.dtype)
    y_p, conv_p, s_p, pool_p = _trunk(x_prompt, conv0, s00, pool0, wts, depth)
    y_s, conv_s, s_s, pool_s = _trunk(x_sample, state_gdn_conv, state_gdn_S, state_pool, wts, depth)
    return (y_p, y_s, conv_p, s_p.astype(state_gdn_S.dtype), pool_p, conv_s, s_s.astype(state_gdn_S.dtype), pool_s)
```

```python
import functools

import jax
import jax.numpy as jnp
from jax import lax
from jax.experimental import pallas as pl
from jax.experimental.pallas import tpu as pltpu

F32 = jnp.float32
BF16 = jnp.bfloat16

EPS = 1e-6
LANES = 128
DK = 128
DV = 128
CONV_W = 4
HIST_ROWS = 8
CHUNK = 64
INV_BLOCK = 16
GDN_TILE = 128
GDN_HEADS = 16
GDN_UNROLL = 16
POOL_WINDOWS = (2, 4, 8, 16)
POOL_PAD = 16
GATE_SLOTS = 4
VMEM_LIMIT = 56 * 1024 * 1024


def _cparams(sem):
    return pltpu.CompilerParams(dimension_semantics=sem, vmem_limit_bytes=VMEM_LIMIT)


def _rms(x, w):
    ms = jnp.mean(x * x, axis=-1, keepdims=True)
    return x * lax.rsqrt(ms + EPS) * w


def _mm(a, b):
    return jnp.dot(a.astype(BF16), b.astype(BF16), preferred_element_type=F32)


def _norm_proj_kernel(x_ref, nw_ref, w_ref, wg_ref, o_ref, g_ref, h_ref, *, tn):
    @pl.when(pl.program_id(1) == 0)
    def _():
        h = _rms(x_ref[...], nw_ref[...]).astype(BF16)
        h_ref[...] = h
        g_ref[...] = jnp.dot(h, wg_ref[...], preferred_element_type=F32)

    acc = jnp.dot(h_ref[...], w_ref[...], preferred_element_type=F32)
    for c in range(tn // LANES):
        o_ref[c] = acc[:, c * LANES:(c + 1) * LANES]


def _norm_proj(x, nw, w, wg, *, layer, n, tm, tn):
    m, d = x.shape
    assert n % tn == 0 and n <= w.shape[2] and m % tm == 0
    return pl.pallas_call(
        functools.partial(_norm_proj_kernel, tn=tn),
        grid=(m // tm, n // tn),
        in_specs=[
            pl.BlockSpec((tm, d), lambda i, j: (i, 0)),
            pl.BlockSpec((1, d), lambda i, j: (0, 0)),
            pl.BlockSpec((None, d, tn), lambda i, j: (layer, 0, j)),
            pl.BlockSpec((d, LANES), lambda i, j: (0, 0)),
        ],
        out_specs=[
            pl.BlockSpec((tn // LANES, tm, LANES), lambda i, j: (j, i, 0)),
            pl.BlockSpec((tm, LANES), lambda i, j: (i, 0)),
        ],
        out_shape=[
            jax.ShapeDtypeStruct((n // LANES, m, LANES), F32),
            jax.ShapeDtypeStruct((m, LANES), F32),
        ],
        scratch_shapes=[pltpu.VMEM((tm, d), BF16)],
        compiler_params=_cparams(("parallel", "arbitrary")),
        name="norm_proj",
    )(x, nw, w, wg)


def _gates_kernel(ba_ref, alog_ref, dtb_ref, tri_ref, o_ref, *, c, valid, period):
    x = ba_ref[...]
    beta = jax.nn.sigmoid(x)
    y = x + dtb_ref[...]
    softplus = jnp.maximum(y, 0.0) + jnp.log1p(jnp.exp(-jnp.abs(y)))
    g = -jnp.exp(alog_ref[...]) * softplus
    if valid != period:
        assert period & (period - 1) == 0 and x.shape[0] % period == 0
        row = lax.broadcasted_iota(jnp.int32, x.shape, 0) & (period - 1)
        beta = jnp.where(row < valid, beta, 0.0)
        g = jnp.where(row < valid, g, 0.0)
    lane = lax.broadcasted_iota(jnp.int32, (c, x.shape[1]), 1)
    is_beta = (lane & (GATE_SLOTS - 1)) == 0
    tri = tri_ref[...]
    for n in range(x.shape[0] // c):
        sl = slice(n * c, (n + 1) * c)
        gc = jnp.dot(tri, g[sl], preferred_element_type=F32, precision=lax.Precision.HIGHEST)
        o_ref[sl, :] = jnp.where(is_beta, beta[sl], gc)


def _gates(ba, alog, dtb, *, c, tt, valid, period):
    m = ba.shape[0]
    tri = jnp.tril(jnp.ones((c, c), F32))
    return pl.pallas_call(
        functools.partial(_gates_kernel, c=c, valid=valid, period=period),
        grid=(m // tt,),
        in_specs=[
            pl.BlockSpec((tt, LANES), lambda i: (i, 0)),
            pl.BlockSpec((1, LANES), lambda i: (0, 0)),
            pl.BlockSpec((1, LANES), lambda i: (0, 0)),
            pl.BlockSpec((c, c), lambda i: (0, 0)),
        ],
        out_specs=pl.BlockSpec((tt, LANES), lambda i: (i, 0)),
        out_shape=jax.ShapeDtypeStruct((m, LANES), F32),
        compiler_params=_cparams(("parallel",)),
        name="gates",
    )(ba, alog, dtb, tri)


def _inv_unit_lower(l_mats):
    c, w = l_mats[0].shape
    npack = w // c
    r = lax.broadcasted_iota(jnp.int32, (c, w), 0)
    lane = lax.broadcasted_iota(jnp.int32, (c, w), 1)
    s = lane & (c - 1)
    shift = INV_BLOCK.bit_length() - 1
    same = lax.shift_right_logical(r, shift) == lax.shift_right_logical(s, shift)
    eye = jnp.where(r == s, 1.0, 0.0).astype(BF16)
    zero = jnp.zeros((c, w), BF16)
    part = [lax.shift_right_logical(lane, c.bit_length() - 1) == p for p in range(npack)]

    def mm(xs, ys):
        if npack > 1:
            ys = [jnp.concatenate([jnp.where(part[p], y, zero) for p in range(npack)], axis=0) for y in ys]
        return [jnp.dot(x, y, preferred_element_type=F32).astype(BF16) for x, y in zip(xs, ys)]

    def stack(xs, ys):
        return [jnp.concatenate([x, y], axis=0) for x, y in zip(xs, ys)]

    d1 = [jnp.where(same, l, 0.0).astype(BF16) for l in l_mats]
    d2 = mm(d1, d1)
    t = mm(stack(d2, d1), d2)
    d4 = [x[:c] for x in t]
    y = [eye - a + b - x[c:] for a, b, x in zip(d1, d2, t)]
    t = mm(stack(d4, y), d4)
    y = [a + x[c:] for a, x in zip(y, t)]
    x0 = [a + b for a, b in zip(y, mm(y, [x[:c] for x in t]))]
    nblk = c // INV_BLOCK
    if nblk == 1:
        return x0
    e = [jnp.where(same, 0.0, l).astype(BF16) for l in l_mats]
    nmat = mm(x0, e)
    z = [eye - a for a in nmat]
    if nblk > 2:
        assert nblk == 4
        n2 = mm(nmat, nmat)
        n3 = mm(nmat, n2)
        z = [a + b - x for a, b, x in zip(z, n2, n3)]
    return mm(z, x0)


def _gdn_kernel(q_ref, k_ref, v_ref, z_ref, hq_ref, hk_ref, hv_ref, cq_ref, ck_ref, cv_ref, g_ref, e_ref, s0_ref,
                nw_ref, o_ref, s_ref, xc_s, hist_s, gcol_s, bcol_s, begcol_s, gl_s, qk_s, kk_s, kbd_s, vbd_s,
                kd_s, wq_s, u_s, a_s, st_s, *, tt, c, heads, unroll):
    n = tt // c
    assert 2 * c == LANES

    @pl.when(pl.program_id(2) == 0)
    def _():
        for g in range(heads):
            hist_s[g, 0] = hq_ref[g, 0]
            hist_s[g, 1] = hk_ref[g, 0]
            for h in range(2):
                hist_s[g, 2 + h] = hv_ref[2 * g + h, 0]
                st_s[g, :, h * DV:(h + 1) * DV] = s0_ref[0, 2 * g + h]

    def l2n(x):
        return x * lax.rsqrt(jnp.sum(x * x, axis=-1, keepdims=True) + EPS)

    zero_k = jnp.zeros((c, LANES), BF16)
    base = HIST_ROWS - (CONV_W - 1)
    ghl = []
    for nn in range(n):
        graw = g_ref[nn * c:(nn + 1) * c, :]
        hi = graw.astype(BF16)
        ghl.append(jnp.concatenate([hi, (graw - hi.astype(F32)).astype(BF16)], axis=1))

    for g in range(heads):
        xin = (q_ref[g], k_ref[g], v_ref[2 * g], v_ref[2 * g + 1])
        for slot in range(4):
            xc_s[g, slot, 0:HIST_ROWS, :] = hist_s[g, slot]
            xc_s[g, slot, HIST_ROWS:HIST_ROWS + tt, :] = xin[slot]
            hist_s[g, slot] = xc_s[g, slot, tt:tt + HIST_ROWS, :]

    ex_all = [jnp.dot(ghl[nn], e_ref[g], preferred_element_type=F32) for nn in range(n) for g in range(heads)]

    def prep(job):
        nn, g = divmod(job, heads)
        r0 = nn * c
        rows = slice(job * c, (job + 1) * c)
        cws = (cq_ref[g], ck_ref[g], cv_ref[2 * g], cv_ref[2 * g + 1])

        def conv_silu(slot):
            acc = xc_s[g, slot, r0 + base:r0 + base + c, :] * cws[slot][0:1, :]
            for j in range(1, CONV_W):
                acc = acc + xc_s[g, slot, r0 + base + j:r0 + base + j + c, :] * cws[slot][j:j + 1, :]
            return acc * jax.nn.sigmoid(acc)

        ex = ex_all[job]
        gcol, bcol = ex[:, 0:LANES], ex[:, LANES:2 * LANES]
        gc = [ex[:, (2 + h) * LANES:(3 + h) * LANES] for h in range(2)]
        eg = [jnp.exp(x) for x in gc]
        gcol_s[rows, :] = gcol
        bcol_s[rows, :] = bcol
        begcol_s[rows, :] = bcol * jnp.exp(gcol)
        gl = [x[c - 1:c, :] for x in gc]
        for h in range(2):
            gl_s[job, :, h * LANES:(h + 1) * LANES] = jnp.exp(gl[h])

        q = l2n(conv_silu(0)) * (DK ** -0.5)
        k = l2n(conv_silu(1))
        k16 = k.astype(BF16)
        qk_s[job, 0:c, :] = q.astype(BF16)
        qk_s[job, c:2 * c, :] = k16
        kk_s[job, 0:c, :] = k16
        kk_s[job, c:2 * c, :] = k16
        for h in range(2):
            wq_s[job, c:2 * c, h * DK:(h + 1) * DK] = (q * eg[h]).astype(BF16)
            kd_s[job, h * c:(h + 1) * c, :] = (k * jnp.exp(gl[h] - gc[h])).astype(BF16)
            kbd_s[job, h * c:(h + 1) * c, h * LANES:(h + 1) * LANES] = k16
            kbd_s[job, h * c:(h + 1) * c, (1 - h) * LANES:(2 - h) * LANES] = zero_k
            vbd_s[job, h * c:(h + 1) * c, h * LANES:(h + 1) * LANES] = conv_silu(2 + h).astype(BF16)
            vbd_s[job, h * c:(h + 1) * c, (1 - h) * LANES:(2 - h) * LANES] = zero_k

    r = lax.broadcasted_iota(jnp.int32, (c, LANES), 0)
    sloc = lax.broadcasted_iota(jnp.int32, (c, LANES), 1) & (c - 1)
    causal = r >= sloc
    strict = r > sloc
    diag = r == sloc

    def as_row(x):
        return jnp.sum(jnp.where(diag, x, 0.0), axis=0, keepdims=True)

    def intra(i):
        js = [i * unroll + j for j in range(unroll)]
        sls = [slice(j * c, (j + 1) * c) for j in js]
        gcol = [gcol_s[sl, :] for sl in sls]
        bcl = [bcol_s[sl, :] for sl in sls]
        begcol = [begcol_s[sl, :] for sl in sls]
        decay = [jnp.where(causal, jnp.exp(jnp.where(causal, x - as_row(x), 0.0)), 0.0) for x in gcol]
        qkk = [lax.dot_general(qk_s[j], kk_s[j], (((1,), (1,)), ((), ())), preferred_element_type=F32)
               for j in js]
        for j, x, d in zip(js, qkk, decay):
            a_s[j] = (x[:c] * d).astype(BF16)
        tm = _inv_unit_lower([jnp.where(strict, x[c:] * d * b, 0.0) for x, d, b in zip(qkk, decay, bcl)])
        w = [jnp.dot(x * as_row(gg).astype(BF16), kbd_s[j], preferred_element_type=F32)
             for j, x, gg in zip(js, tm, begcol)]
        u = [jnp.dot(x * as_row(b).astype(BF16), vbd_s[j], preferred_element_type=F32)
             for j, x, b in zip(js, tm, bcl)]
        for j, wn, un in zip(js, w, u):
            u_s[j] = un
            wq_s[j, 0:c, :] = wn.astype(BF16)

    for i in range(heads * n // unroll):
        for job in range(i * unroll, (i + 1) * unroll):
            prep(job)
        intra(i)

    zero_c = jnp.zeros((c, DV), BF16)
    zero_s = jnp.zeros((DK, DV), BF16)

    def blockdiag(x, zero):
        return jnp.concatenate([jnp.concatenate([x[:, :DV], zero], axis=1),
                                jnp.concatenate([zero, x[:, DV:]], axis=1)], axis=0)

    def scan(nn):
        js = [nn * heads + g for g in range(heads)]
        sp = [st_s[g] for g in range(heads)]
        r1 = [jnp.dot(wq_s[j], blockdiag(x.astype(BF16), zero_s), preferred_element_type=F32)
              for j, x in zip(js, sp)]
        vnbd = [blockdiag((u_s[j] - x[0:c]).astype(BF16), zero_c) for j, x in zip(js, r1)]
        av = [jnp.dot(a_s[j], v, preferred_element_type=F32) for j, v in zip(js, vnbd)]
        upd = [lax.dot_general(kd_s[j], v, (((0,), (0,)), ((), ())), preferred_element_type=F32)
               for j, v in zip(js, vnbd)]
        rows = slice(nn * c, (nn + 1) * c)
        for g, j in enumerate(js):
            st_s[g] = sp[g] * gl_s[j] + upd[g]
            for h in range(2):
                hv = 2 * g + h
                o = r1[g][c:2 * c, h * DV:(h + 1) * DV] + av[g][:, h * DV:(h + 1) * DV]
                zf = z_ref[hv, rows, :]
                o = (o * lax.rsqrt(jnp.mean(o * o, axis=-1, keepdims=True) + EPS) * nw_ref[...]
                     * (zf * jax.nn.sigmoid(zf)))
                o_ref[rows, hv * DV:(hv + 1) * DV] = o.astype(o_ref.dtype)

    for nn in range(n):
        scan(nn)

    for g in range(heads):
        for h in range(2):
            s_ref[0, 2 * g + h] = st_s[g, :, h * DV:(h + 1) * DV]


def _gate_selectors(nk):
    kh = jnp.arange(nk, dtype=jnp.int32)[:, None, None]
    row = jnp.arange(2 * LANES, dtype=jnp.int32)[None, :, None] & (LANES - 1)
    col = jnp.arange(4 * LANES, dtype=jnp.int32)[None, None, :]
    blk, second = col // LANES, (col % LANES) >= LANES // 2
    head = jnp.where(blk < 2, second.astype(jnp.int32), blk - 2)
    slot = jnp.where(blk == 1, 0, 1)
    return (row == GATE_SLOTS * (2 * kh + head) + slot).astype(BF16)


def _gdn_core(ph, hist, cw, gates, s0, nw, *, b, t, nk, nv, tt, heads, unroll):
    c = CHUNK
    assert t % tt == 0 and tt % c == 0 and nk % heads == 0 and nv == 2 * nk and DK == LANES and DV == LANES
    assert (heads * tt // c) % unroll == 0
    m = b * t
    nt = t // tt
    jobs = heads * tt // c
    vh = 2 * heads
    off_k, off_v, off_z = nk // heads, 2 * nk // vh, (2 * nk + nv) // vh
    seq = lambda off: (lambda bi, h, ti: (off + h, bi * nt + ti, 0))
    hst = lambda off: (lambda bi, h, ti: (off + h, bi, 0, 0))
    cwt = lambda off: (lambda bi, h, ti: (off + h, 0, 0))
    f32 = lambda *shape: pltpu.VMEM(shape, F32)
    bf16 = lambda *shape: pltpu.VMEM(shape, BF16)
    return pl.pallas_call(
        functools.partial(_gdn_kernel, tt=tt, c=c, heads=heads, unroll=unroll),
        grid=(b, nk // heads, nt),
        in_specs=[
            pl.BlockSpec((heads, tt, LANES), seq(0)), pl.BlockSpec((heads, tt, LANES), seq(off_k)),
            pl.BlockSpec((vh, tt, LANES), seq(off_v)), pl.BlockSpec((vh, tt, LANES), seq(off_z)),
            pl.BlockSpec((heads, 1, HIST_ROWS, LANES), hst(0)), pl.BlockSpec((heads, 1, HIST_ROWS, LANES), hst(off_k)),
            pl.BlockSpec((vh, 1, HIST_ROWS, LANES), hst(off_v)),
            pl.BlockSpec((heads, HIST_ROWS, LANES), cwt(0)), pl.BlockSpec((heads, HIST_ROWS, LANES), cwt(off_k)),
            pl.BlockSpec((vh, HIST_ROWS, LANES), cwt(off_v)),
            pl.BlockSpec((tt, LANES), lambda bi, h, ti: (bi * nt + ti, 0)),
            pl.BlockSpec((heads, 2 * LANES, 4 * LANES), lambda bi, h, ti: (h, 0, 0)),
            pl.BlockSpec((1, vh, DK, DV), lambda bi, h, ti: (bi, h, 0, 0)),
            pl.BlockSpec((1, DV), lambda bi, h, ti: (0, 0)),
        ],
        out_specs=[
            pl.BlockSpec((tt, vh * DV), lambda bi, h, ti: (bi * nt + ti, h)),
            pl.BlockSpec((1, vh, DK, DV), lambda bi, h, ti: (bi, h, 0, 0)),
        ],
        out_shape=[
            jax.ShapeDtypeStruct((m, nv * DV), BF16),
            jax.ShapeDtypeStruct((b, nv, DK, DV), F32),
        ],
        scratch_shapes=[
            f32(heads, 4, tt + HIST_ROWS, LANES), f32(heads, 4, HIST_ROWS, LANES),
            f32(heads * tt, LANES), f32(heads * tt, LANES), f32(heads * tt, LANES), f32(jobs, 1, 2 * LANES),
            bf16(jobs, 2 * c, LANES), bf16(jobs, 2 * c, LANES), bf16(jobs, 2 * c, 2 * LANES),
            bf16(jobs, 2 * c, 2 * LANES), bf16(jobs, 2 * c, LANES), bf16(jobs, 2 * c, 2 * LANES),
            f32(jobs, c, 2 * DV), bf16(jobs, c, LANES), f32(heads, DK, 2 * DV),
        ],
        compiler_params=_cparams(("parallel", "parallel", "arbitrary")),
        name="gdn_core",
    )(ph, ph, ph, ph, hist, hist, hist, cw, cw, cw, gates, _gate_selectors(nk), s0, nw)


def _mm_res_kernel(a_ref, w_ref, r_ref, o_ref):
    o_ref[...] = r_ref[...] + jnp.dot(a_ref[...], w_ref[...], preferred_element_type=F32)


def _mm_residual(a, w, res, *, layer, tm, tn):
    m, k = a.shape
    n = w.shape[2]
    return pl.pallas_call(
        _mm_res_kernel,
        grid=(m // tm, n // tn),
        in_specs=[
            pl.BlockSpec((tm, k), lambda i, j: (i, 0)),
            pl.BlockSpec((None, k, tn), lambda i, j: (layer, 0, j)),
            pl.BlockSpec((tm, tn), lambda i, j: (i, j)),
        ],
        out_specs=pl.BlockSpec((tm, tn), lambda i, j: (i, j)),
        out_shape=jax.ShapeDtypeStruct((m, n), F32),
        compiler_params=_cparams(("parallel", "arbitrary")),
        name="mm_residual",
    )(a, w, res)


def _ffn_kernel(x_ref, nw_ref, wg_ref, wu_ref, wd_ref, fw_ref, o_ref, h_ref, *, final):
    j = pl.program_id(1)

    @pl.when(j == 0)
    def _():
        x = x_ref[...]
        h_ref[...] = _rms(x, nw_ref[...]).astype(BF16)
        o_ref[...] = x

    h = h_ref[...]
    g = jnp.dot(h, wg_ref[...], preferred_element_type=F32)
    u = jnp.dot(h, wu_ref[...], preferred_element_type=F32)
    act = (g * jax.nn.sigmoid(g) * u).astype(BF16)
    o_ref[...] += jnp.dot(act, wd_ref[...], preferred_element_type=F32)

    if final:
        @pl.when(j == pl.num_programs(1) - 1)
        def _():
            o_ref[...] = _rms(o_ref[...], fw_ref[...])


def _ffn(x, nw, wgu, wd, fw, *, layer, tm, tf, final):
    m, d = x.shape
    f = wd.shape[1]
    nf = f // tf
    return pl.pallas_call(
        functools.partial(_ffn_kernel, final=final),
        grid=(m // tm, nf),
        in_specs=[
            pl.BlockSpec((tm, d), lambda i, j: (i, 0)),
            pl.BlockSpec((1, d), lambda i, j: (0, 0)),
            pl.BlockSpec((None, d, tf), lambda i, j: (layer, 0, j)),
            pl.BlockSpec((None, d, tf), lambda i, j: (layer, 0, nf + j)),
            pl.BlockSpec((None, tf, d), lambda i, j: (layer, j, 0)),
            pl.BlockSpec((1, d), lambda i, j: (0, 0)),
        ],
        out_specs=pl.BlockSpec((tm, d), lambda i, j: (i, 0)),
        out_shape=jax.ShapeDtypeStruct((m, d), F32),
        scratch_shapes=[pltpu.VMEM((tm, d), BF16)],
        compiler_params=_cparams(("parallel", "arbitrary")),
        name="ffn",
    )(x, nw, wgu, wgu, wd, fw)


def _pool_kernel(x_ref, nw_ref, hist_ref, pw_ref, sc_ref, o_ref, hs_ref, hbuf, *, tt, past):
    ti = pl.program_id(1)

    @pl.when(ti == 0)
    def _():
        hbuf[0:POOL_PAD, :] = hist_ref[0]

    @pl.when(ti > 0)
    def _():
        hbuf[0:POOL_PAD, :] = hbuf[tt:tt + POOL_PAD, :]

    x = x_ref[...]
    h = _rms(x, nw_ref[...])
    hbuf[POOL_PAD:POOL_PAD + tt, :] = h
    tg = ti * tt + lax.broadcasted_iota(jnp.int32, (tt, 1), 0)
    gcw = x.shape[1] // len(POOL_WINDOWS)
    for gi, wlen in enumerate(POOL_WINDOWS):
        cols = slice(gi * gcw, (gi + 1) * gcw)
        acc = h[:, cols]
        for i in range(1, wlen):
            acc = acc + hbuf[POOL_PAD - i:POOL_PAD - i + tt, cols]
        cnt = jnp.minimum(wlen, past + tg + 1).astype(F32)
        dlt = acc / cnt - h[:, cols]
        y = jnp.dot(dlt.astype(BF16), pw_ref[gi], preferred_element_type=F32)
        o_ref[:, cols] = x[:, cols] + y * sc_ref[:, cols]
    hs_ref[0] = hbuf[tt:tt + POOL_PAD, :]


def _pool_mixer(x, nw, hist, pw, sc, *, b, t, tt, past):
    m, d = x.shape
    nt = t // tt
    g, gcw, _ = pw.shape
    return pl.pallas_call(
        functools.partial(_pool_kernel, tt=tt, past=past),
        grid=(b, nt),
        in_specs=[
            pl.BlockSpec((tt, d), lambda bi, ti: (bi * nt + ti, 0)),
            pl.BlockSpec((1, d), lambda bi, ti: (0, 0)),
            pl.BlockSpec((1, POOL_PAD, d), lambda bi, ti: (bi, 0, 0)),
            pl.BlockSpec((g, gcw, gcw), lambda bi, ti: (0, 0, 0)),
            pl.BlockSpec((1, d), lambda bi, ti: (0, 0)),
        ],
        out_specs=[
            pl.BlockSpec((tt, d), lambda bi, ti: (bi * nt + ti, 0)),
            pl.BlockSpec((1, POOL_PAD, d), lambda bi, ti: (bi, 0, 0)),
        ],
        out_shape=[
            jax.ShapeDtypeStruct((m, d), F32),
            jax.ShapeDtypeStruct((b, POOL_PAD, d), F32),
        ],
        scratch_shapes=[pltpu.VMEM((tt + POOL_PAD, d), F32)],
        compiler_params=_cparams(("parallel", "arbitrary")),
        name="pool_mixer",
    )(x, nw, hist, pw, sc)


def _prep_weights(norm_mix_w, norm_ffn_w, final_norm_w, gdn_w_in, gdn_conv_w, gdn_A_log, gdn_dt_bias, gdn_norm_w,
                  gdn_w_out, pool_w, pool_scale, ffn_w_gu, ffn_w_down):
    n_a, d, _ = gdn_w_in.shape
    nv = gdn_A_log.shape[1]
    val_dim = nv * DV
    qkv_dim = gdn_conv_w.shape[2]
    main = qkv_dim + val_dim
    zeros = jnp.zeros((n_a, d, nv), F32)
    w_gate = jnp.stack([gdn_w_in[:, :, main:main + nv], gdn_w_in[:, :, main + nv:main + 2 * nv], zeros, zeros],
                       axis=-1).reshape(n_a, d, nv * GATE_SLOTS).astype(BF16)
    zv = jnp.zeros((n_a, nv), F32)
    alog = jnp.stack([zv, gdn_A_log.astype(F32), zv, zv], axis=-1).reshape(n_a, 1, nv * GATE_SLOTS)
    dtb = jnp.stack([zv, gdn_dt_bias.astype(F32), zv, zv], axis=-1).reshape(n_a, 1, nv * GATE_SLOTS)
    nh = qkv_dim // LANES
    cw = gdn_conv_w.reshape(n_a, CONV_W, nh, LANES).transpose(0, 2, 1, 3)
    cw = jnp.pad(cw, ((0, 0), (0, 0), (0, HIST_ROWS - CONV_W), (0, 0)))
    return dict(
        norm_mix=norm_mix_w[:, None, :], norm_ffn=norm_ffn_w[:, None, :], final=final_norm_w[None, :],
        w_in=gdn_w_in.astype(BF16), n_main=main, w_gate=w_gate, alog=alog, dtb=dtb, cw=cw,
        gnorm=gdn_norm_w[:, None, :], w_out=gdn_w_out.astype(BF16), pool_w=pool_w.astype(BF16),
        pool_scale=pool_scale[:, None, :], w_gu=ffn_w_gu.astype(BF16), w_down=ffn_w_down.astype(BF16),
    )


def _gdn_layer(x, conv_hist, s_hist, wts, i, j, *, b, t):
    m, d = x.shape
    nv = s_hist.shape[1]
    nk = nv // 2
    qkv_dim = conv_hist.shape[-1]
    nh = qkv_dim // LANES
    if t % GDN_TILE == 0:
        tp, tt, heads = t, GDN_TILE, GDN_HEADS
        xp = x
    else:
        tp = -(-t // CHUNK) * CHUNK
        tt, heads = tp, nk
        xp = jnp.pad(x.reshape(b, t, d), ((0, 0), (0, tp - t), (0, 0))).reshape(b * tp, d)
    mp = b * tp
    ph, ba = _norm_proj(xp, wts["norm_mix"][i], wts["w_in"], wts["w_gate"][j], layer=j, n=wts["n_main"],
                        tm=min(1024, mp), tn=512)
    gates = _gates(ba, wts["alog"][j], wts["dtb"][j], c=CHUNK, tt=min(mp, 2048), valid=t, period=tp)
    hist = conv_hist.reshape(b, CONV_W - 1, nh, LANES).transpose(2, 0, 1, 3)
    hist = jnp.pad(hist, ((0, 0), (0, 0), (HIST_ROWS - (CONV_W - 1), 0), (0, 0)))
    o, s_new = _gdn_core(ph, hist, wts["cw"][j], gates, s_hist.astype(F32), wts["gnorm"][j], b=b, t=tp, nk=nk,
                         nv=nv, tt=tt, heads=heads, unroll=min(GDN_UNROLL, heads * tt // CHUNK))
    if tp != t:
        o = o.reshape(b, tp, o.shape[-1])[:, :t].reshape(m, o.shape[-1])
    x = _mm_residual(o, wts["w_out"], x, layer=j, tm=min(1024, m), tn=512)
    tail = ph.reshape(ph.shape[0], b, tp, LANES)[:nh, :, t - (CONV_W - 1):t, :]
    return x, tail.transpose(1, 2, 0, 3).reshape(b, CONV_W - 1, qkv_dim), s_new


def _trunk(x3, conv_hist, s_hist, pool_hist, wts, depth):
    b, t, d = x3.shape
    m = b * t
    x = x3.reshape(m, d)
    past = pool_hist.shape[2]
    assert t >= CONV_W - 1 and t >= POOL_PAD - 1
    new_conv, new_s, new_pool = [], [], []
    for i in range(depth):
        j = i // 2
        if i % 2 == 0:
            x, conv_new, s_new = _gdn_layer(x, conv_hist[j], s_hist[j], wts, i, j, b=b, t=t)
            new_conv.append(conv_new)
            new_s.append(s_new)
        else:
            hist = jnp.pad(pool_hist[j], ((0, 0), (POOL_PAD - past, 0), (0, 0)))
            x, hs = _pool_mixer(x, wts["norm_mix"][i], hist, wts["pool_w"][j], wts["pool_scale"][j],
                                b=b, t=t, tt=min(t, 256), past=past)
            new_pool.append(hs[:, 1:, :])
        x = _ffn(x, wts["norm_ffn"][i], wts["w_gu"], wts["w_down"], wts["final"], layer=i, tm=min(1024, m), tf=512,
                 final=(i == depth - 1))
    return x.reshape(b, t, d), jnp.stack(new_conv), jnp.stack(new_s), jnp.stack(new_pool)


def kernel(x_prompt, x_sample, state_gdn_conv, state_gdn_S, state_pool, norm_mix_w, norm_ffn_w, final_norm_w,
           gdn_w_in, gdn_conv_w, gdn_A_log, gdn_dt_bias, gdn_norm_w, gdn_w_out, pool_w, pool_scale, ffn_w_gu,
           ffn_w_down):
    depth = norm_mix_w.shape[0]
    n_a, n_b = state_gdn_conv.shape[0], state_pool.shape[0]
    bp = x_prompt.shape[0]
    wts = _prep_weights(norm_mix_w, norm_ffn_w, final_norm_w, gdn_w_in, gdn_conv_w, gdn_A_log, gdn_dt_bias,
                        gdn_norm_w, gdn_w_out, pool_w, pool_scale, ffn_w_gu, ffn_w_down)
    conv0 = jnp.zeros((n_a, bp) + state_gdn_conv.shape[2:], x_prompt.dtype)
    s00 = jnp.zeros((n_a, bp) + state_gdn_S.shape[2:], F32)
    pool0 = jnp.zeros((n_b, bp, 0, x_prompt.shape[-1]), x_prompt.dtype)
    y_p, conv_p, s_p, pool_p = _trunk(x_prompt, conv0, s00, pool0, wts, depth)
    y_s, conv_s, s_s, pool_s = _trunk(x_sample, state_gdn_conv, state_gdn_S, state_pool, wts, depth)
    return (y_p, y_s, conv_p, s_p.astype(state_gdn_S.dtype), pool_p, conv_s, s_s.astype(state_gdn_S.dtype), pool_s)
```

```python
import functools

import jax
import jax.numpy as jnp
from jax import lax
from jax.experimental import pallas as pl
from jax.experimental.pallas import tpu as pltpu

F32 = jnp.float32
BF16 = jnp.bfloat16

EPS = 1e-6
LANES = 128
DK = 128
DV = 128
CONV_W = 4
HIST_ROWS = 8
CHUNK = 64
INV_BLOCK = 16
GDN_TILE = 128
GDN_HEADS = 16
GDN_UNROLL = 32
POOL_WINDOWS = (2, 4, 8, 16)
POOL_PAD = 16
GATE_SLOTS = 4
VMEM_LIMIT = 56 * 1024 * 1024


def _cparams(sem):
    return pltpu.CompilerParams(dimension_semantics=sem, vmem_limit_bytes=VMEM_LIMIT)


def _rms(x, w):
    ms = jnp.mean(x * x, axis=-1, keepdims=True)
    return x * lax.rsqrt(ms + EPS) * w


def _mm(a, b):
    return jnp.dot(a.astype(BF16), b.astype(BF16), preferred_element_type=F32)


def _norm_proj_kernel(x_ref, nw_ref, w_ref, wg_ref, o_ref, g_ref, h_ref, *, tn):
    @pl.when(pl.program_id(1) == 0)
    def _():
        h = _rms(x_ref[...], nw_ref[...]).astype(BF16)
        h_ref[...] = h
        g_ref[...] = jnp.dot(h, wg_ref[...], preferred_element_type=F32)

    acc = jnp.dot(h_ref[...], w_ref[...], preferred_element_type=F32)
    for c in range(tn // LANES):
        o_ref[c] = acc[:, c * LANES:(c + 1) * LANES]


def _norm_proj(x, nw, w, wg, *, layer, n, tm, tn):
    m, d = x.shape
    assert n % tn == 0 and n <= w.shape[2] and m % tm == 0
    return pl.pallas_call(
        functools.partial(_norm_proj_kernel, tn=tn),
        grid=(m // tm, n // tn),
        in_specs=[
            pl.BlockSpec((tm, d), lambda i, j: (i, 0)),
            pl.BlockSpec((1, d), lambda i, j: (0, 0)),
            pl.BlockSpec((None, d, tn), lambda i, j: (layer, 0, j)),
            pl.BlockSpec((d, LANES), lambda i, j: (0, 0)),
        ],
        out_specs=[
            pl.BlockSpec((tn // LANES, tm, LANES), lambda i, j: (j, i, 0)),
            pl.BlockSpec((tm, LANES), lambda i, j: (i, 0)),
        ],
        out_shape=[
            jax.ShapeDtypeStruct((n // LANES, m, LANES), F32),
            jax.ShapeDtypeStruct((m, LANES), F32),
        ],
        scratch_shapes=[pltpu.VMEM((tm, d), BF16)],
        compiler_params=_cparams(("parallel", "arbitrary")),
        name="norm_proj",
    )(x, nw, w, wg)


def _gates_kernel(ba_ref, alog_ref, dtb_ref, tri_ref, o_ref, *, c, valid, period):
    x = ba_ref[...]
    beta = jax.nn.sigmoid(x)
    y = x + dtb_ref[...]
    softplus = jnp.maximum(y, 0.0) + jnp.log1p(jnp.exp(-jnp.abs(y)))
    g = -jnp.exp(alog_ref[...]) * softplus
    if valid != period:
        assert period & (period - 1) == 0 and x.shape[0] % period == 0
        row = lax.broadcasted_iota(jnp.int32, x.shape, 0) & (period - 1)
        beta = jnp.where(row < valid, beta, 0.0)
        g = jnp.where(row < valid, g, 0.0)
    lane = lax.broadcasted_iota(jnp.int32, (c, x.shape[1]), 1)
    is_beta = (lane & (GATE_SLOTS - 1)) == 0
    tri = tri_ref[...]
    for n in range(x.shape[0] // c):
        sl = slice(n * c, (n + 1) * c)
        gc = jnp.dot(tri, g[sl], preferred_element_type=F32, precision=lax.Precision.HIGHEST)
        o_ref[sl, :] = jnp.where(is_beta, beta[sl], gc)


def _gates(ba, alog, dtb, *, c, tt, valid, period):
    m = ba.shape[0]
    tri = jnp.tril(jnp.ones((c, c), F32))
    return pl.pallas_call(
        functools.partial(_gates_kernel, c=c, valid=valid, period=period),
        grid=(m // tt,),
        in_specs=[
            pl.BlockSpec((tt, LANES), lambda i: (i, 0)),
            pl.BlockSpec((1, LANES), lambda i: (0, 0)),
            pl.BlockSpec((1, LANES), lambda i: (0, 0)),
            pl.BlockSpec((c, c), lambda i: (0, 0)),
        ],
        out_specs=pl.BlockSpec((tt, LANES), lambda i: (i, 0)),
        out_shape=jax.ShapeDtypeStruct((m, LANES), F32),
        compiler_params=_cparams(("parallel",)),
        name="gates",
    )(ba, alog, dtb, tri)


def _inv_unit_lower(l_mats):
    c, w = l_mats[0].shape
    npack = w // c
    r = lax.broadcasted_iota(jnp.int32, (c, w), 0)
    lane = lax.broadcasted_iota(jnp.int32, (c, w), 1)
    s = lane & (c - 1)
    shift = INV_BLOCK.bit_length() - 1
    same = lax.shift_right_logical(r, shift) == lax.shift_right_logical(s, shift)
    eye = jnp.where(r == s, 1.0, 0.0).astype(BF16)
    zero = jnp.zeros((c, w), BF16)
    part = [lax.shift_right_logical(lane, c.bit_length() - 1) == p for p in range(npack)]

    def mm(xs, ys):
        if npack > 1:
            ys = [jnp.concatenate([jnp.where(part[p], y, zero) for p in range(npack)], axis=0) for y in ys]
        return [jnp.dot(x, y, preferred_element_type=F32).astype(BF16) for x, y in zip(xs, ys)]

    def stack(xs, ys):
        return [jnp.concatenate([x, y], axis=0) for x, y in zip(xs, ys)]

    d1 = [jnp.where(same, l, 0.0).astype(BF16) for l in l_mats]
    d2 = mm(d1, d1)
    t = mm(stack(d2, d1), d2)
    d4 = [x[:c] for x in t]
    y = [eye - a + b - x[c:] for a, b, x in zip(d1, d2, t)]
    t = mm(stack(d4, y), d4)
    y = [a + x[c:] for a, x in zip(y, t)]
    x0 = [a + b for a, b in zip(y, mm(y, [x[:c] for x in t]))]
    nblk = c // INV_BLOCK
    if nblk == 1:
        return x0
    e = [jnp.where(same, 0.0, l).astype(BF16) for l in l_mats]
    nmat = mm(x0, e)
    z = [eye - a for a in nmat]
    if nblk > 2:
        assert nblk == 4
        n2 = mm(nmat, nmat)
        n3 = mm(nmat, n2)
        z = [a + b - x for a, b, x in zip(z, n2, n3)]
    return mm(z, x0)


def _gdn_kernel(q_ref, k_ref, v_ref, z_ref, hq_ref, hk_ref, hv_ref, cq_ref, ck_ref, cv_ref, g_ref, e_ref, s0_ref,
                nw_ref, o_ref, s_ref, xc_s, hist_s, gl_s, kd_s, wq_s, u_s, a_s, st_s, *, tt, c, heads, unroll):
    n = tt // c
    assert 2 * c == LANES

    @pl.when(pl.program_id(2) == 0)
    def _():
        for g in range(heads):
            hist_s[g, 0] = hq_ref[g, 0]
            hist_s[g, 1] = hk_ref[g, 0]
            for h in range(2):
                hist_s[g, 2 + h] = hv_ref[2 * g + h, 0]
                st_s[g, :, h * DV:(h + 1) * DV] = s0_ref[0, 2 * g + h]

    def l2n(x):
        return x * lax.rsqrt(jnp.sum(x * x, axis=-1, keepdims=True) + EPS)

    zero_k = jnp.zeros((c, LANES), BF16)
    base = HIST_ROWS - (CONV_W - 1)
    ghl = []
    for nn in range(n):
        graw = g_ref[nn * c:(nn + 1) * c, :]
        hi = graw.astype(BF16)
        ghl.append(jnp.concatenate([hi, (graw - hi.astype(F32)).astype(BF16)], axis=1))

    for g in range(heads):
        xin = (q_ref[g], k_ref[g], v_ref[2 * g], v_ref[2 * g + 1])
        for slot in range(4):
            xc_s[g, slot, 0:HIST_ROWS, :] = hist_s[g, slot]
            xc_s[g, slot, HIST_ROWS:HIST_ROWS + tt, :] = xin[slot]
            hist_s[g, slot] = xc_s[g, slot, tt:tt + HIST_ROWS, :]

    ex_all = [jnp.dot(ghl[nn], e_ref[g], preferred_element_type=F32) for nn in range(n) for g in range(heads)]

    def prep(job):
        nn, g = divmod(job, heads)
        r0 = nn * c
        cws =(cq_ref[g], ck_ref[g], cv_ref[2 * g], cv_ref[2 * g + 1])

        def conv_silu(slot):
            acc = xc_s[g, slot, r0 + base:r0 + base + c, :] * cws[slot][0:1, :]
            for j in range(1, CONV_W):
                acc = acc + xc_s[g, slot, r0 + base + j:r0 + base + j + c, :] * cws[slot][j:j + 1, :]
            return acc * jax.nn.sigmoid(acc)

        ex = ex_all[job]
        gcol, bcol = ex[:, 0:LANES], ex[:, LANES:2 * LANES]
        gc = [ex[:, (2 + h) * LANES:(3 + h) * LANES] for h in range(2)]
        eg = [jnp.exp(x) for x in gc]
        gl = [x[c - 1:c, :] for x in gc]
        for h in range(2):
            gl_s[job, :, h * LANES:(h + 1) * LANES] = jnp.exp(gl[h])

        q = l2n(conv_silu(0)) * (DK ** -0.5)
        k = l2n(conv_silu(1))
        k16 = k.astype(BF16)
        v16 = [conv_silu(2 + h).astype(BF16) for h in range(2)]
        for h in range(2):
            wq_s[job, c:2 * c, h * DK:(h + 1) * DK] = (q * eg[h]).astype(BF16)
            kd_s[job, h * c:(h + 1) * c, :] = (k * jnp.exp(gl[h] - gc[h])).astype(BF16)
        return dict(gcol=gcol, bcol=bcol, begcol=bcol * jnp.exp(gcol),
                    qk=jnp.concatenate([q.astype(BF16), k16], axis=0), kk=jnp.concatenate([k16, k16], axis=0),
                    kbd=jnp.concatenate([jnp.concatenate([k16, zero_k], axis=1),
                                         jnp.concatenate([zero_k, k16], axis=1)], axis=0),
                    vbd=jnp.concatenate([jnp.concatenate([v16[0], zero_k], axis=1),
                                         jnp.concatenate([zero_k, v16[1]], axis=1)], axis=0))

    r = lax.broadcasted_iota(jnp.int32, (c, LANES), 0)
    sloc = lax.broadcasted_iota(jnp.int32, (c, LANES), 1) & (c - 1)
    causal = r >= sloc
    strict = r > sloc
    diag = r == sloc

    def as_row(x):
        return jnp.sum(jnp.where(diag, x, 0.0), axis=0, keepdims=True)

    def intra(js, ops):
        gcol = [p["gcol"] for p in ops]
        bcl = [p["bcol"] for p in ops]
        begcol = [p["begcol"] for p in ops]
        decay = [jnp.where(causal, jnp.exp(jnp.where(causal, x - as_row(x), 0.0)), 0.0) for x in gcol]
        qkk = [lax.dot_general(p["qk"], p["kk"], (((1,), (1,)), ((), ())), preferred_element_type=F32)
               for p in ops]
        for j, x, d in zip(js, qkk, decay):
            a_s[j] = (x[:c] * d).astype(BF16)
        tm = _inv_unit_lower([jnp.where(strict, x[c:] * d * b, 0.0) for x, d, b in zip(qkk, decay, bcl)])
        w = [jnp.dot(x * as_row(gg).astype(BF16), p["kbd"], preferred_element_type=F32)
             for p, x, gg in zip(ops, tm, begcol)]
        u = [jnp.dot(x * as_row(b).astype(BF16), p["vbd"], preferred_element_type=F32)
             for p, x, b in zip(ops, tm, bcl)]
        for j, wn, un in zip(js, w, u):
            u_s[j] = un
            wq_s[j, 0:c, :] = wn.astype(BF16)

    for i in range(heads * n // unroll):
        js = list(range(i * unroll, (i + 1) * unroll))
        intra(js, [prep(job) for job in js])

    zero_c = jnp.zeros((c, DV), BF16)
    zero_s = jnp.zeros((DK, DV), BF16)

    def blockdiag(x, zero):
        return jnp.concatenate([jnp.concatenate([x[:, :DV], zero], axis=1),
                                jnp.concatenate([zero, x[:, DV:]], axis=1)], axis=0)

    def scan(nn):
        js = [nn * heads + g for g in range(heads)]
        sp = [st_s[g] for g in range(heads)]
        r1 = [jnp.dot(wq_s[j], blockdiag(x.astype(BF16), zero_s), preferred_element_type=F32)
              for j, x in zip(js, sp)]
        vnbd = [blockdiag((u_s[j] - x[0:c]).astype(BF16), zero_c) for j, x in zip(js, r1)]
        av = [jnp.dot(a_s[j], v, preferred_element_type=F32) for j, v in zip(js, vnbd)]
        upd = [lax.dot_general(kd_s[j], v, (((0,), (0,)), ((), ())), preferred_element_type=F32)
               for j, v in zip(js, vnbd)]
        rows = slice(nn * c, (nn + 1) * c)
        for g, j in enumerate(js):
            st_s[g] = sp[g] * gl_s[j] + upd[g]
            for h in range(2):
                hv = 2 * g + h
                o = r1[g][c:2 * c, h * DV:(h + 1) * DV] + av[g][:, h * DV:(h + 1) * DV]
                zf = z_ref[hv, rows, :]
                o = (o * lax.rsqrt(jnp.mean(o * o, axis=-1, keepdims=True) + EPS) * nw_ref[...]
                     * (zf * jax.nn.sigmoid(zf)))
                o_ref[rows, hv * DV:(hv + 1) * DV] = o.astype(o_ref.dtype)

    for nn in range(n):
        scan(nn)

    for g in range(heads):
        for h in range(2):
            s_ref[0, 2 * g + h] = st_s[g, :, h * DV:(h + 1) * DV]


def _gate_selectors(nk):
    kh = jnp.arange(nk, dtype=jnp.int32)[:, None, None]
    row = jnp.arange(2 * LANES, dtype=jnp.int32)[None, :, None] & (LANES - 1)
    col = jnp.arange(4 * LANES, dtype=jnp.int32)[None, None, :]
    blk, second = col // LANES, (col % LANES) >= LANES // 2
    head = jnp.where(blk < 2, second.astype(jnp.int32), blk - 2)
    slot = jnp.where(blk == 1, 0, 1)
    return (row == GATE_SLOTS * (2 * kh + head) + slot).astype(BF16)


def _gdn_core(ph, hist, cw, gates, s0, nw, *, b, t, nk, nv, tt, heads, unroll):
    c = CHUNK
    assert t % tt == 0 and tt % c == 0 and nk % heads == 0 and nv == 2 * nk and DK == LANES and DV == LANES
    assert (heads * tt // c) % unroll == 0
    m = b * t
    nt = t // tt
    jobs = heads * tt // c
    vh = 2 * heads
    off_k, off_v, off_z = nk // heads, 2 * nk // vh, (2 * nk + nv) // vh
    seq = lambda off: (lambda bi, h, ti: (off + h, bi * nt + ti, 0))
    hst = lambda off: (lambda bi, h, ti: (off + h, bi, 0, 0))
    cwt = lambda off: (lambda bi, h, ti: (off + h, 0, 0))
    f32 = lambda *shape: pltpu.VMEM(shape, F32)
    bf16 = lambda *shape: pltpu.VMEM(shape, BF16)
    return pl.pallas_call(
        functools.partial(_gdn_kernel, tt=tt, c=c, heads=heads, unroll=unroll),
        grid=(b, nk // heads, nt),
        in_specs=[
            pl.BlockSpec((heads, tt, LANES), seq(0)), pl.BlockSpec((heads, tt, LANES), seq(off_k)),
            pl.BlockSpec((vh, tt, LANES), seq(off_v)), pl.BlockSpec((vh, tt, LANES), seq(off_z)),
            pl.BlockSpec((heads, 1, HIST_ROWS, LANES), hst(0)), pl.BlockSpec((heads, 1, HIST_ROWS, LANES), hst(off_k)),
            pl.BlockSpec((vh, 1, HIST_ROWS, LANES), hst(off_v)),
            pl.BlockSpec((heads, HIST_ROWS, LANES), cwt(0)), pl.BlockSpec((heads, HIST_ROWS, LANES), cwt(off_k)),
            pl.BlockSpec((vh, HIST_ROWS, LANES), cwt(off_v)),
            pl.BlockSpec((tt, LANES), lambda bi, h, ti: (bi * nt + ti, 0)),
            pl.BlockSpec((heads, 2 * LANES, 4 * LANES), lambda bi, h, ti: (h, 0, 0)),
            pl.BlockSpec((1, vh, DK, DV), lambda bi, h, ti: (bi, h, 0, 0)),
            pl.BlockSpec((1, DV), lambda bi, h, ti: (0, 0)),
        ],
        out_specs=[
            pl.BlockSpec((tt, vh * DV), lambda bi, h, ti: (bi * nt + ti, h)),
            pl.BlockSpec((1, vh, DK, DV), lambda bi, h, ti: (bi, h, 0, 0)),
        ],
        out_shape=[
            jax.ShapeDtypeStruct((m, nv * DV), BF16),
            jax.ShapeDtypeStruct((b, nv, DK, DV), F32),
        ],
        scratch_shapes=[
            f32(heads, 4, tt + HIST_ROWS, LANES), f32(heads, 4, HIST_ROWS, LANES), f32(jobs, 1, 2 * LANES),
            bf16(jobs, 2 * c, LANES), bf16(jobs, 2 * c, 2 * LANES),
            f32(jobs, c, 2 * DV), bf16(jobs, c, LANES), f32(heads, DK, 2 * DV),
        ],
        compiler_params=_cparams(("parallel", "parallel", "arbitrary")),
        name="gdn_core",
    )(ph, ph, ph, ph, hist, hist, hist, cw, cw, cw, gates, _gate_selectors(nk), s0, nw)


def _mm_res_kernel(a_ref, w_ref, r_ref, o_ref):
    o_ref[...] = r_ref[...] + jnp.dot(a_ref[...], w_ref[...], preferred_element_type=F32)


def _mm_residual(a, w, res, *, layer, tm, tn):
    m, k = a.shape
    n = w.shape[2]
    return pl.pallas_call(
        _mm_res_kernel,
        grid=(m // tm, n // tn),
        in_specs=[
            pl.BlockSpec((tm, k), lambda i, j: (i, 0)),
            pl.BlockSpec((None, k, tn), lambda i, j: (layer, 0, j)),
            pl.BlockSpec((tm, tn), lambda i, j: (i, j)),
        ],
        out_specs=pl.BlockSpec((tm, tn), lambda i, j: (i, j)),
        out_shape=jax.ShapeDtypeStruct((m, n), F32),
        compiler_params=_cparams(("parallel", "arbitrary")),
        name="mm_residual",
    )(a, w, res)


def _ffn_kernel(x_ref, nw_ref, wg_ref, wu_ref, wd_ref, fw_ref, o_ref, h_ref, *, final):
    j = pl.program_id(1)

    @pl.when(j == 0)
    def _():
        x = x_ref[...]
        h_ref[...] = _rms(x, nw_ref[...]).astype(BF16)
        o_ref[...] = x

    h = h_ref[...]
    g = jnp.dot(h, wg_ref[...], preferred_element_type=F32)
    u = jnp.dot(h, wu_ref[...], preferred_element_type=F32)
    act = (g * jax.nn.sigmoid(g) * u).astype(BF16)
    o_ref[...] += jnp.dot(act, wd_ref[...], preferred_element_type=F32)

    if final:
        @pl.when(j == pl.num_programs(1) - 1)
        def _():
            o_ref[...] = _rms(o_ref[...], fw_ref[...])


def _ffn(x, nw, wgu, wd, fw, *, layer, tm, tf, final):
    m, d = x.shape
    f = wd.shape[1]
    nf = f // tf
    return pl.pallas_call(
        functools.partial(_ffn_kernel, final=final),
        grid=(m // tm, nf),
        in_specs=[
            pl.BlockSpec((tm, d), lambda i, j: (i, 0)),
            pl.BlockSpec((1, d), lambda i, j: (0, 0)),
            pl.BlockSpec((None, d, tf), lambda i, j: (layer, 0, j)),
            pl.BlockSpec((None, d, tf), lambda i, j: (layer, 0, nf + j)),
            pl.BlockSpec((None, tf, d), lambda i, j: (layer, j, 0)),
            pl.BlockSpec((1, d), lambda i, j: (0, 0)),
        ],
        out_specs=pl.BlockSpec((tm, d), lambda i, j: (i, 0)),
        out_shape=jax.ShapeDtypeStruct((m, d), F32),
        scratch_shapes=[pltpu.VMEM((tm, d), BF16)],
        compiler_params=_cparams(("parallel", "arbitrary")),
        name="ffn",
    )(x, nw, wgu, wgu, wd, fw)


def _pool_kernel(x_ref, nw_ref, hist_ref, pw_ref, sc_ref, o_ref, hs_ref, hbuf, *, tt, past):
    ti = pl.program_id(1)

    @pl.when(ti == 0)
    def _():
        hbuf[0:POOL_PAD, :] = hist_ref[0]

    @pl.when(ti > 0)
    def _():
        hbuf[0:POOL_PAD, :] = hbuf[tt:tt + POOL_PAD, :]

    x = x_ref[...]
    h = _rms(x, nw_ref[...])
    hbuf[POOL_PAD:POOL_PAD + tt, :] = h
    tg = ti * tt + lax.broadcasted_iota(jnp.int32, (tt, 1), 0)
    gcw = x.shape[1] // len(POOL_WINDOWS)
    for gi, wlen in enumerate(POOL_WINDOWS):
        cols = slice(gi * gcw, (gi + 1) * gcw)
        acc = h[:, cols]
        for i in range(1, wlen):
            acc = acc + hbuf[POOL_PAD - i:POOL_PAD - i + tt, cols]
        cnt = jnp.minimum(wlen, past + tg + 1).astype(F32)
        dlt = acc / cnt - h[:, cols]
        y = jnp.dot(dlt.astype(BF16), pw_ref[gi], preferred_element_type=F32)
        o_ref[:, cols] = x[:, cols] + y * sc_ref[:, cols]
    hs_ref[0] = hbuf[tt:tt + POOL_PAD, :]


def _pool_mixer(x, nw, hist, pw, sc, *, b, t, tt, past):
    m, d = x.shape
    nt = t // tt
    g, gcw, _ = pw.shape
    return pl.pallas_call(
        functools.partial(_pool_kernel, tt=tt, past=past),
        grid=(b, nt),
        in_specs=[
            pl.BlockSpec((tt, d), lambda bi, ti: (bi * nt + ti, 0)),
            pl.BlockSpec((1, d), lambda bi, ti: (0, 0)),
            pl.BlockSpec((1, POOL_PAD, d), lambda bi, ti: (bi, 0, 0)),
            pl.BlockSpec((g, gcw, gcw), lambda bi, ti: (0, 0, 0)),
            pl.BlockSpec((1, d), lambda bi, ti: (0, 0)),
        ],
        out_specs=[
            pl.BlockSpec((tt, d), lambda bi, ti: (bi * nt + ti, 0)),
            pl.BlockSpec((1, POOL_PAD, d), lambda bi, ti: (bi, 0, 0)),
        ],
        out_shape=[
            jax.ShapeDtypeStruct((m, d), F32),
            jax.ShapeDtypeStruct((b, POOL_PAD, d), F32),
        ],
        scratch_shapes=[pltpu.VMEM((tt + POOL_PAD, d), F32)],
        compiler_params=_cparams(("parallel", "arbitrary")),
        name="pool_mixer",
    )(x, nw, hist, pw, sc)


def _prep_weights(norm_mix_w, norm_ffn_w, final_norm_w, gdn_w_in, gdn_conv_w, gdn_A_log, gdn_dt_bias, gdn_norm_w,
                  gdn_w_out, pool_w, pool_scale, ffn_w_gu, ffn_w_down):
    n_a, d, _ = gdn_w_in.shape
    nv = gdn_A_log.shape[1]
    val_dim = nv * DV
    qkv_dim = gdn_conv_w.shape[2]
    main = qkv_dim + val_dim
    zeros = jnp.zeros((n_a, d, nv), F32)
    w_gate = jnp.stack([gdn_w_in[:, :, main:main + nv], gdn_w_in[:, :, main + nv:main + 2 * nv], zeros, zeros],
                       axis=-1).reshape(n_a, d, nv * GATE_SLOTS).astype(BF16)
    zv = jnp.zeros((n_a, nv), F32)
    alog = jnp.stack([zv, gdn_A_log.astype(F32), zv, zv], axis=-1).reshape(n_a, 1, nv * GATE_SLOTS)
    dtb = jnp.stack([zv, gdn_dt_bias.astype(F32), zv, zv], axis=-1).reshape(n_a, 1, nv * GATE_SLOTS)
    nh = qkv_dim // LANES
    cw = gdn_conv_w.reshape(n_a, CONV_W, nh, LANES).transpose(0, 2, 1, 3)
    cw = jnp.pad(cw, ((0, 0), (0, 0), (0, HIST_ROWS - CONV_W), (0, 0)))
    return dict(
        norm_mix=norm_mix_w[:, None, :], norm_ffn=norm_ffn_w[:, None, :], final=final_norm_w[None, :],
        w_in=gdn_w_in.astype(BF16), n_main=main, w_gate=w_gate, alog=alog, dtb=dtb, cw=cw,
        gnorm=gdn_norm_w[:, None, :], w_out=gdn_w_out.astype(BF16), pool_w=pool_w.astype(BF16),
        pool_scale=pool_scale[:, None, :], w_gu=ffn_w_gu.astype(BF16), w_down=ffn_w_down.astype(BF16),
    )


def _gdn_layer(x, conv_hist, s_hist, wts, i, j, *, b, t):
    m, d = x.shape
    nv = s_hist.shape[1]
    nk = nv // 2
    qkv_dim = conv_hist.shape[-1]
    nh = qkv_dim // LANES
    if t % GDN_TILE == 0:
        tp, tt, heads = t, GDN_TILE, GDN_HEADS
        xp = x
    else:
        tp = -(-t // CHUNK) * CHUNK
        tt, heads = tp, nk
        xp = jnp.pad(x.reshape(b, t, d), ((0, 0), (0, tp - t), (0, 0))).reshape(b * tp, d)
    mp = b * tp
    ph, ba = _norm_proj(xp, wts["norm_mix"][i], wts["w_in"], wts["w_gate"][j], layer=j, n=wts["n_main"],
                        tm=min(1024, mp), tn=1536)
    gates = _gates(ba, wts["alog"][j], wts["dtb"][j], c=CHUNK, tt=min(mp, 2048), valid=t, period=tp)
    hist = conv_hist.reshape(b, CONV_W - 1, nh, LANES).transpose(2, 0, 1, 3)
    hist = jnp.pad(hist, ((0, 0), (0, 0), (HIST_ROWS - (CONV_W - 1), 0), (0, 0)))
    o, s_new = _gdn_core(ph, hist, wts["cw"][j], gates, s_hist.astype(F32), wts["gnorm"][j], b=b, t=tp, nk=nk,
                         nv=nv, tt=tt, heads=heads, unroll=min(GDN_UNROLL, heads * tt // CHUNK))
    if tp != t:
        o = o.reshape(b, tp, o.shape[-1])[:, :t].reshape(m, o.shape[-1])
    x = _mm_residual(o, wts["w_out"], x, layer=j, tm=min(1024, m), tn=1024)
    tail = ph.reshape(ph.shape[0], b, tp, LANES)[:nh, :, t - (CONV_W - 1):t, :]
    return x, tail.transpose(1, 2, 0, 3).reshape(b, CONV_W - 1, qkv_dim), s_new


def _trunk(x3, conv_hist, s_hist, pool_hist, wts, depth):
    b, t, d = x3.shape
    m = b * t
    x = x3.reshape(m, d)
    past = pool_hist.shape[2]
    assert t >= CONV_W - 1 and t >= POOL_PAD - 1
    new_conv, new_s, new_pool = [], [], []
    for i in range(depth):
        j = i // 2
        if i % 2 == 0:
            x, conv_new, s_new = _gdn_layer(x, conv_hist[j], s_hist[j], wts, i, j, b=b, t=t)
            new_conv.append(conv_new)
            new_s.append(s_new)
        else:
            hist = jnp.pad(pool_hist[j], ((0, 0), (POOL_PAD - past, 0), (0, 0)))
            x, hs = _pool_mixer(x, wts["norm_mix"][i], hist, wts["pool_w"][j], wts["pool_scale"][j],
                                b=b, t=t, tt=min(t, 256), past=past)
            new_pool.append(hs[:, 1:, :])
        x = _ffn(x, wts["norm_ffn"][i], wts["w_gu"], wts["w_down"], wts["final"], layer=i, tm=min(1024, m), tf=512,
                 final=(i == depth - 1))
    return x.reshape(b, t, d), jnp.stack(new_conv), jnp.stack(new_s), jnp.stack(new_pool)


def kernel(x_prompt, x_sample, state_gdn_conv, state_gdn_S, state_pool, norm_mix_w, norm_ffn_w, final_norm_w,
           gdn_w_in, gdn_conv_w, gdn_A_log, gdn_dt_bias, gdn_norm_w, gdn_w_out, pool_w, pool_scale, ffn_w_gu,
           ffn_w_down):
    depth = norm_mix_w.shape[0]
    n_a, n_b = state_gdn_conv.shape[0], state_pool.shape[0]
    bp = x_prompt.shape[0]
    wts = _prep_weights(norm_mix_w, norm_ffn_w, final_norm_w, gdn_w_in, gdn_conv_w, gdn_A_log, gdn_dt_bias,
                        gdn_norm_w, gdn_w_out, pool_w, pool_scale, ffn_w_gu, ffn_w_down)
    conv0 = jnp.zeros((n_a, bp) + state_gdn_conv.shape[2:], x_prompt.dtype)
    s00 = jnp.zeros((n_a, bp) + state_gdn_S.shape[2:], F32)
    pool0 = jnp.zeros((n_b, bp, 0, x_prompt.shape[-1]), x_prompt.dtype)
    y_p, conv_p, s_p, pool_p = _trunk(x_prompt, conv0, s00, pool0, wts, depth)
    y_s, conv_s, s_s, pool_s = _trunk(x_sample, state_gdn_conv, state_gdn_S, state_pool, wts, depth)
    return (y_p, y_s, conv_p, s_p.astype(state_gdn_S.dtype), pool_p, conv_s, s_s.astype(state_gdn_S.dtype), pool_s)
```

```python
import functools

import jax
import jax.numpy as jnp
from jax import lax
from jax.experimental import pallas as pl
from jax.experimental.pallas import tpu as pltpu

F32 = jnp.float32
BF16 = jnp.bfloat16

EPS = 1e-6
LANES = 128
DK = 128
DV = 128
CONV_W = 4
HIST_ROWS = 8
CHUNK = 64
INV_BLOCK = 16
GDN_TILE = 128
GDN_HEADS = 16
GDN_UNROLL = 32
POOL_WINDOWS = (2, 4, 8, 16)
POOL_PAD = 16
GATE_SLOTS = 4
VMEM_LIMIT = 56 * 1024 * 1024


def _cparams(sem):
    return pltpu.CompilerParams(dimension_semantics=sem, vmem_limit_bytes=VMEM_LIMIT)


def _rms(x, w):
    ms = jnp.mean(x * x, axis=-1, keepdims=True)
    return x * lax.rsqrt(ms + EPS) * w


def _mm(a, b):
    return jnp.dot(a.astype(BF16), b.astype(BF16), preferred_element_type=F32)


def _silu(x):
    h = 0.5 * x
    return h * jnp.tanh(h) + h


def _norm_proj_kernel(x_ref, nw_ref, w_ref, wg_ref, o_ref, g_ref, h_ref, *, tn):
    @pl.when(pl.program_id(1) == 0)
    def _():
        h = _rms(x_ref[...], nw_ref[...]).astype(BF16)
        h_ref[...] = h
        g_ref[...] = jnp.dot(h, wg_ref[...], preferred_element_type=F32)

    acc = jnp.dot(h_ref[...], w_ref[...], preferred_element_type=F32)
    for c in range(tn // LANES):
        o_ref[c] = acc[:, c * LANES:(c + 1) * LANES]


def _norm_proj(x, nw, w, wg, *, layer, n, tm, tn):
    m, d = x.shape
    assert n % tn == 0 and n <= w.shape[2] and m % tm == 0
    return pl.pallas_call(
        functools.partial(_norm_proj_kernel, tn=tn),
        grid=(m // tm, n // tn),
        in_specs=[
            pl.BlockSpec((tm, d), lambda i, j: (i, 0)),
            pl.BlockSpec((1, d), lambda i, j: (0, 0)),
            pl.BlockSpec((None, d, tn), lambda i, j: (layer, 0, j)),
            pl.BlockSpec((d, LANES), lambda i, j: (0, 0)),
        ],
        out_specs=[
            pl.BlockSpec((tn // LANES, tm, LANES), lambda i, j: (j, i, 0)),
            pl.BlockSpec((tm, LANES), lambda i, j: (i, 0)),
        ],
        out_shape=[
            jax.ShapeDtypeStruct((n // LANES, m, LANES), F32),
            jax.ShapeDtypeStruct((m, LANES), F32),
        ],
        scratch_shapes=[pltpu.VMEM((tm, d), BF16)],
        compiler_params=_cparams(("parallel", "arbitrary")),
        name="norm_proj",
    )(x, nw, w, wg)


def _gates_kernel(ba_ref, alog_ref, dtb_ref, tri_ref, o_ref, *, c, valid, period):
    x = ba_ref[...]
    beta = jax.nn.sigmoid(x)
    y = x + dtb_ref[...]
    softplus = jnp.maximum(y, 0.0) + jnp.log1p(jnp.exp(-jnp.abs(y)))
    g = -jnp.exp(alog_ref[...]) * softplus
    if valid != period:
        assert period & (period - 1) == 0 and x.shape[0] % period == 0
        row = lax.broadcasted_iota(jnp.int32, x.shape, 0) & (period - 1)
        beta = jnp.where(row < valid, beta, 0.0)
        g = jnp.where(row < valid, g, 0.0)
    lane = lax.broadcasted_iota(jnp.int32, (c, x.shape[1]), 1)
    is_beta = (lane & (GATE_SLOTS - 1)) == 0
    tri = tri_ref[...]
    for n in range(x.shape[0] // c):
        sl = slice(n * c, (n + 1) * c)
        gc = jnp.dot(tri, g[sl], preferred_element_type=F32, precision=lax.Precision.HIGHEST)
        o_ref[sl, :] = jnp.where(is_beta, beta[sl], gc)


def _gates(ba, alog, dtb, *, c, tt, valid, period):
    m = ba.shape[0]
    tri = jnp.tril(jnp.ones((c, c), F32))
    return pl.pallas_call(
        functools.partial(_gates_kernel, c=c, valid=valid, period=period),
        grid=(m // tt,),
        in_specs=[
            pl.BlockSpec((tt, LANES), lambda i: (i, 0)),
            pl.BlockSpec((1, LANES), lambda i: (0, 0)),
            pl.BlockSpec((1, LANES), lambda i: (0, 0)),
            pl.BlockSpec((c, c), lambda i: (0, 0)),
        ],
        out_specs=pl.BlockSpec((tt, LANES), lambda i: (i, 0)),
        out_shape=jax.ShapeDtypeStruct((m, LANES), F32),
        compiler_params=_cparams(("parallel",)),
        name="gates",
    )(ba, alog, dtb, tri)


def _inv_unit_lower(l_mats):
    c, w = l_mats[0].shape
    npack = w // c
    r = lax.broadcasted_iota(jnp.int32, (c, w), 0)
    lane = lax.broadcasted_iota(jnp.int32, (c, w), 1)
    s = lane & (c - 1)
    shift = INV_BLOCK.bit_length() - 1
    same = lax.shift_right_logical(r, shift) == lax.shift_right_logical(s, shift)
    eye = jnp.where(r == s, 1.0, 0.0).astype(BF16)
    zero = jnp.zeros((c, w), BF16)
    part = [lax.shift_right_logical(lane, c.bit_length() - 1) == p for p in range(npack)]

    def mm(xs, ys):
        if npack > 1:
            ys = [jnp.concatenate([jnp.where(part[p], y, zero) for p in range(npack)], axis=0) for y in ys]
        return [jnp.dot(x, y, preferred_element_type=F32).astype(BF16) for x, y in zip(xs, ys)]

    def stack(xs, ys):
        return [jnp.concatenate([x, y], axis=0) for x, y in zip(xs, ys)]

    d1 = [jnp.where(same, l, 0.0).astype(BF16) for l in l_mats]
    d2 = mm(d1, d1)
    t = mm(stack(d2, d1), d2)
    d4 = [x[:c] for x in t]
    y = [eye - a + b - x[c:] for a, b, x in zip(d1, d2, t)]
    t = mm(stack(d4, y), d4)
    y = [a + x[c:] for a, x in zip(y, t)]
    x0 = [a + b for a, b in zip(y, mm(y, [x[:c] for x in t]))]
    nblk = c // INV_BLOCK
    if nblk == 1:
        return x0
    e = [jnp.where(same, 0.0, l).astype(BF16) for l in l_mats]
    nmat = mm(x0, e)
    z = [eye - a for a in nmat]
    if nblk > 2:
        assert nblk == 4
        n2 = mm(nmat, nmat)
        n3 = mm(nmat, n2)
        z = [a + b - x for a, b, x in zip(z, n2, n3)]
    return mm(z, x0)


def _gdn_kernel(q_ref, k_ref, v_ref, z_ref, hq_ref, hk_ref, hv_ref, cq_ref, ck_ref, cv_ref, g_ref, e_ref, s0_ref,
                nw_ref, o_ref, s_ref, xc_s, hist_s, gl_s, kd_s, wq_s, u_s, a_s, st_s, *, tt, c, heads, unroll):
    n = tt // c
    assert 2 * c == LANES

    @pl.when(pl.program_id(2) == 0)
    def _():
        for g in range(heads):
            hist_s[g, 0] = hq_ref[g, 0]
            hist_s[g, 1] = hk_ref[g, 0]
            for h in range(2):
                hist_s[g, 2 + h] = hv_ref[2 * g + h, 0]
                st_s[g, :, h * DV:(h + 1) * DV] = s0_ref[0, 2 * g + h]

    def l2n(x):
        return x * lax.rsqrt(jnp.sum(x * x, axis=-1, keepdims=True) + EPS)

    zero_k = jnp.zeros((c, LANES), BF16)
    base = HIST_ROWS - (CONV_W - 1)
    ghl = []
    for nn in range(n):
        graw = g_ref[nn * c:(nn + 1) * c, :]
        hi = graw.astype(BF16)
        ghl.append(jnp.concatenate([hi, (graw - hi.astype(F32)).astype(BF16)], axis=1))

    for g in range(heads):
        xin = (q_ref[g], k_ref[g], v_ref[2 * g], v_ref[2 * g + 1])
        for slot in range(4):
            xc_s[g, slot, 0:HIST_ROWS, :] = hist_s[g, slot]
            xc_s[g, slot, HIST_ROWS:HIST_ROWS + tt, :] = xin[slot]
            hist_s[g, slot] = xc_s[g, slot, tt:tt + HIST_ROWS, :]

    ex_all = [jnp.dot(ghl[nn], e_ref[g], preferred_element_type=F32) for nn in range(n) for g in range(heads)]

    def prep(job):
        nn, g = divmod(job, heads)
        r0 = nn * c
        cws =(cq_ref[g], ck_ref[g], cv_ref[2 * g], cv_ref[2 * g + 1])

        def conv_silu(slot):
            acc = xc_s[g, slot, r0 + base:r0 + base + c, :] * cws[slot][0:1, :]
            for j in range(1, CONV_W):
                acc = acc + xc_s[g, slot, r0 + base + j:r0 + base + j + c, :] * cws[slot][j:j + 1, :]
            return _silu(acc)

        ex = ex_all[job]
        gcol, bcol = ex[:, 0:LANES], ex[:, LANES:2 * LANES]
        gc = [ex[:, (2 + h) * LANES:(3 + h) * LANES] for h in range(2)]
        eg = [jnp.exp(x) for x in gc]
        gl = [x[c - 1:c, :] for x in gc]
        for h in range(2):
            gl_s[job, :, h * LANES:(h + 1) * LANES] = jnp.exp(gl[h])

        q = l2n(conv_silu(0)) * (DK ** -0.5)
        k = l2n(conv_silu(1))
        k16 = k.astype(BF16)
        v16 = [conv_silu(2 + h).astype(BF16) for h in range(2)]
        for h in range(2):
            wq_s[job, c:2 * c, h * DK:(h + 1) * DK] = (q * eg[h]).astype(BF16)
            kd_s[job, h * c:(h + 1) * c, :] = (k * jnp.exp(gl[h] - gc[h])).astype(BF16)
        return dict(gcol=gcol, bcol=bcol, begcol=bcol * jnp.exp(gcol),
                    qk=jnp.concatenate([q.astype(BF16), k16], axis=0), kk=jnp.concatenate([k16, k16], axis=0),
                    kbd=jnp.concatenate([jnp.concatenate([k16, zero_k], axis=1),
                                         jnp.concatenate([zero_k, k16], axis=1)], axis=0),
                    vbd=jnp.concatenate([jnp.concatenate([v16[0], zero_k], axis=1),
                                         jnp.concatenate([zero_k, v16[1]], axis=1)], axis=0))

    r = lax.broadcasted_iota(jnp.int32, (c, LANES), 0)
    sloc = lax.broadcasted_iota(jnp.int32, (c, LANES), 1) & (c - 1)
    causal = r >= sloc
    strict = r > sloc
    diag = r == sloc

    def as_row(x):
        return jnp.sum(jnp.where(diag, x, 0.0), axis=0, keepdims=True)

    def intra(js, ops):
        gcol = [p["gcol"] for p in ops]
        bcl = [p["bcol"] for p in ops]
        begcol = [p["begcol"] for p in ops]
        decay = [jnp.where(causal, jnp.exp(jnp.where(causal, x - as_row(x), 0.0)), 0.0) for x in gcol]
        qkk = [lax.dot_general(p["qk"], p["kk"], (((1,), (1,)), ((), ())), preferred_element_type=F32)
               for p in ops]
        for j, x, d in zip(js, qkk, decay):
            a_s[j] = (x[:c] * d).astype(BF16)
        tm = _inv_unit_lower([jnp.where(strict, x[c:] * d * b, 0.0) for x, d, b in zip(qkk, decay, bcl)])
        w = [jnp.dot(x * as_row(gg).astype(BF16), p["kbd"], preferred_element_type=F32)
             for p, x, gg in zip(ops, tm, begcol)]
        u = [jnp.dot(x * as_row(b).astype(BF16), p["vbd"], preferred_element_type=F32)
             for p, x, b in zip(ops, tm, bcl)]
        for j, wn, un in zip(js, w, u):
            u_s[j] = un
            wq_s[j, 0:c, :] = wn.astype(BF16)

    for i in range(heads * n // unroll):
        js = list(range(i * unroll, (i + 1) * unroll))
        intra(js, [prep(job) for job in js])

    zero_c = jnp.zeros((c, DV), BF16)
    zero_s = jnp.zeros((DK, DV), BF16)

    def blockdiag(x, zero):
        return jnp.concatenate([jnp.concatenate([x[:, :DV], zero], axis=1),
                                jnp.concatenate([zero, x[:, DV:]], axis=1)], axis=0)

    def scan(nn):
        js = [nn * heads + g for g in range(heads)]
        sp = [st_s[g] for g in range(heads)]
        r1 = [jnp.dot(wq_s[j], blockdiag(x.astype(BF16), zero_s), preferred_element_type=F32)
              for j, x in zip(js, sp)]
        vnbd = [blockdiag((u_s[j] - x[0:c]).astype(BF16), zero_c) for j, x in zip(js, r1)]
        av = [jnp.dot(a_s[j], v, preferred_element_type=F32) for j, v in zip(js, vnbd)]
        upd = [lax.dot_general(kd_s[j], v, (((0,), (0,)), ((), ())), preferred_element_type=F32)
               for j, v in zip(js, vnbd)]
        rows = slice(nn * c, (nn + 1) * c)
        for g, j in enumerate(js):
            st_s[g] = sp[g] * gl_s[j] + upd[g]
            for h in range(2):
                hv = 2 * g + h
                o = r1[g][c:2 * c, h * DV:(h + 1) * DV] + av[g][:, h * DV:(h + 1) * DV]
                o = (o * lax.rsqrt(jnp.mean(o * o, axis=-1, keepdims=True) + EPS) * nw_ref[...]
                     * _silu(z_ref[hv, rows, :]))
                o_ref[rows, hv * DV:(hv + 1) * DV] = o.astype(o_ref.dtype)

    for nn in range(n):
        scan(nn)

    for g in range(heads):
        for h in range(2):
            s_ref[0, 2 * g + h] = st_s[g, :, h * DV:(h + 1) * DV]


def _gate_selectors(nk):
    kh = jnp.arange(nk, dtype=jnp.int32)[:, None, None]
    row = jnp.arange(2 * LANES, dtype=jnp.int32)[None, :, None] & (LANES - 1)
    col = jnp.arange(4 * LANES, dtype=jnp.int32)[None, None, :]
    blk, second = col // LANES, (col % LANES) >= LANES // 2
    head = jnp.where(blk < 2, second.astype(jnp.int32), blk - 2)
    slot = jnp.where(blk == 1, 0, 1)
    return (row == GATE_SLOTS * (2 * kh + head) + slot).astype(BF16)


def _gdn_core(ph, hist, cw, gates, s0, nw, *, b, t, nk, nv, tt, heads, unroll):
    c = CHUNK
    assert t % tt == 0 and tt % c == 0 and nk % heads == 0 and nv == 2 * nk and DK == LANES and DV == LANES
    assert (heads * tt // c) % unroll == 0
    m = b * t
    nt = t // tt
    jobs = heads * tt // c
    vh = 2 * heads
    off_k, off_v, off_z = nk // heads, 2 * nk // vh, (2 * nk + nv) // vh
    seq = lambda off: (lambda bi, h, ti: (off + h, bi * nt + ti, 0))
    hst = lambda off: (lambda bi, h, ti: (off + h, bi, 0, 0))
    cwt = lambda off: (lambda bi, h, ti: (off + h, 0, 0))
    f32 = lambda *shape: pltpu.VMEM(shape, F32)
    bf16 = lambda *shape: pltpu.VMEM(shape, BF16)
    return pl.pallas_call(
        functools.partial(_gdn_kernel, tt=tt, c=c, heads=heads, unroll=unroll),
        grid=(b, nk // heads, nt),
        in_specs=[
            pl.BlockSpec((heads, tt, LANES), seq(0)), pl.BlockSpec((heads, tt, LANES), seq(off_k)),
            pl.BlockSpec((vh, tt, LANES), seq(off_v)), pl.BlockSpec((vh, tt, LANES), seq(off_z)),
            pl.BlockSpec((heads, 1, HIST_ROWS, LANES), hst(0)), pl.BlockSpec((heads, 1, HIST_ROWS, LANES), hst(off_k)),
            pl.BlockSpec((vh, 1, HIST_ROWS, LANES), hst(off_v)),
            pl.BlockSpec((heads, HIST_ROWS, LANES), cwt(0)), pl.BlockSpec((heads, HIST_ROWS, LANES), cwt(off_k)),
            pl.BlockSpec((vh, HIST_ROWS, LANES), cwt(off_v)),
            pl.BlockSpec((tt, LANES), lambda bi, h, ti: (bi * nt + ti, 0)),
            pl.BlockSpec((heads, 2 * LANES, 4 * LANES), lambda bi, h, ti: (h, 0, 0)),
            pl.BlockSpec((1, vh, DK, DV), lambda bi, h, ti: (bi, h, 0, 0)),
            pl.BlockSpec((1, DV), lambda bi, h, ti: (0, 0)),
        ],
        out_specs=[
            pl.BlockSpec((tt, vh * DV), lambda bi, h, ti: (bi * nt + ti, h)),
            pl.BlockSpec((1, vh, DK, DV), lambda bi, h, ti: (bi, h, 0, 0)),
        ],
        out_shape=[
            jax.ShapeDtypeStruct((m, nv * DV), BF16),
            jax.ShapeDtypeStruct((b, nv, DK, DV), F32),
        ],
        scratch_shapes=[
            f32(heads, 4, tt + HIST_ROWS, LANES), f32(heads, 4, HIST_ROWS, LANES), f32(jobs, 1, 2 * LANES),
            bf16(jobs, 2 * c, LANES), bf16(jobs, 2 * c, 2 * LANES),
            f32(jobs, c, 2 * DV), bf16(jobs, c, LANES), f32(heads, DK, 2 * DV),
        ],
        compiler_params=_cparams(("parallel", "parallel", "arbitrary")),
        name="gdn_core",
    )(ph, ph, ph, ph, hist, hist, hist, cw, cw, cw, gates, _gate_selectors(nk), s0, nw)


def _mm_res_kernel(a_ref, w_ref, r_ref, o_ref):
    o_ref[...] = r_ref[...] + jnp.dot(a_ref[...], w_ref[...], preferred_element_type=F32)


def _mm_residual(a, w, res, *, layer, tm, tn):
    m, k = a.shape
    n = w.shape[2]
    return pl.pallas_call(
        _mm_res_kernel,
        grid=(m // tm, n // tn),
        in_specs=[
            pl.BlockSpec((tm, k), lambda i, j: (i, 0)),
            pl.BlockSpec((None, k, tn), lambda i, j: (layer, 0, j)),
            pl.BlockSpec((tm, tn), lambda i, j: (i, j)),
        ],
        out_specs=pl.BlockSpec((tm, tn), lambda i, j: (i, j)),
        out_shape=jax.ShapeDtypeStruct((m, n), F32),
        compiler_params=_cparams(("parallel", "arbitrary")),
        name="mm_residual",
    )(a, w, res)


def _ffn_kernel(x_ref, nw_ref, wg_ref, wu_ref, wd_ref, fw_ref, o_ref, h_ref, *, final):
    j = pl.program_id(1)

    @pl.when(j == 0)
    def _():
        x = x_ref[...]
        h_ref[...] = _rms(x, nw_ref[...]).astype(BF16)
        o_ref[...] = x

    h = h_ref[...]
    g = jnp.dot(h, wg_ref[...], preferred_element_type=F32)
    u = jnp.dot(h, wu_ref[...], preferred_element_type=F32)
    act = (g * jax.nn.sigmoid(g) * u).astype(BF16)
    o_ref[...] += jnp.dot(act, wd_ref[...], preferred_element_type=F32)

    if final:
        @pl.when(j == pl.num_programs(1) - 1)
        def _():
            o_ref[...] = _rms(o_ref[...], fw_ref[...])


def _ffn(x, nw, wgu, wd, fw, *, layer, tm, tf, final):
    m, d = x.shape
    f = wd.shape[1]
    nf = f // tf
    return pl.pallas_call(
        functools.partial(_ffn_kernel, final=final),
        grid=(m // tm, nf),
        in_specs=[
            pl.BlockSpec((tm, d), lambda i, j: (i, 0)),
            pl.BlockSpec((1, d), lambda i, j: (0, 0)),
            pl.BlockSpec((None, d, tf), lambda i, j: (layer, 0, j)),
            pl.BlockSpec((None, d, tf), lambda i, j: (layer, 0, nf + j)),
            pl.BlockSpec((None, tf, d), lambda i, j: (layer, j, 0)),
            pl.BlockSpec((1, d), lambda i, j: (0, 0)),
        ],
        out_specs=pl.BlockSpec((tm, d), lambda i, j: (i, 0)),
        out_shape=jax.ShapeDtypeStruct((m, d), F32),
        scratch_shapes=[pltpu.VMEM((tm, d), BF16)],
        compiler_params=_cparams(("parallel", "arbitrary")),
        name="ffn",
    )(x, nw, wgu, wgu, wd, fw)


def _pool_kernel(x_ref, nw_ref, hist_ref, pw_ref, sc_ref, o_ref, hs_ref, hbuf, *lvl, tt, past):
    ti = pl.program_id(1)
    pad = 2 * POOL_PAD
    assert all(w == 2 << g for g, w in enumerate(POOL_WINDOWS)) and POOL_WINDOWS[-1] // 2 <= 8

    @pl.when(ti == 0)
    def _():
        hbuf[0:POOL_PAD, :] = jnp.zeros((POOL_PAD, hbuf.shape[1]), F32)
        hbuf[POOL_PAD:pad, :] = hist_ref[0]

    @pl.when(ti > 0)
    def _():
        hbuf[0:pad, :] = hbuf[tt:tt + pad, :]

    x = x_ref[...]
    h = _rms(x, nw_ref[...])
    hbuf[pad:pad + tt, :] = h
    tg = ti * tt + lax.broadcasted_iota(jnp.int32, (tt, 1), 0)
    d = x.shape[1]
    gcw = d // len(POOL_WINDOWS)
    prev, prev_col0 = hbuf, 0
    for gi, wlen in enumerate(POOL_WINDOWS):
        start, shift, col0 = 8 * (gi + 1), wlen // 2, gi * gcw
        rel = slice(col0 - prev_col0, d - prev_col0)
        cur = prev[start:pad + tt, rel] + prev[start - shift:pad + tt - shift, rel]
        if gi + 1 < len(POOL_WINDOWS):
            lvl[gi][start:pad + tt, :] = cur
            prev, prev_col0 = lvl[gi], col0
        cols = slice(col0, col0 + gcw)
        cnt = jnp.minimum(wlen, past + tg + 1).astype(F32)
        dlt = cur[pad - start:, 0:gcw] / cnt - h[:, cols]
        y = jnp.dot(dlt.astype(BF16), pw_ref[gi], preferred_element_type=F32)
        o_ref[:, cols] = x[:, cols] + y * sc_ref[:, cols]
    hs_ref[0] = hbuf[tt + POOL_PAD:tt + pad, :]


def _pool_mixer(x, nw, hist, pw, sc, *, b, t, tt, past):
    m, d = x.shape
    nt = t // tt
    g, gcw, _ = pw.shape
    return pl.pallas_call(
        functools.partial(_pool_kernel, tt=tt, past=past),
        grid=(b, nt),
        in_specs=[
            pl.BlockSpec((tt, d), lambda bi, ti: (bi * nt + ti, 0)),
            pl.BlockSpec((1, d), lambda bi, ti: (0, 0)),
            pl.BlockSpec((1, POOL_PAD, d), lambda bi, ti: (bi, 0, 0)),
            pl.BlockSpec((g, gcw, gcw), lambda bi, ti: (0, 0, 0)),
            pl.BlockSpec((1, d), lambda bi, ti: (0, 0)),
        ],
        out_specs=[
            pl.BlockSpec((tt, d), lambda bi, ti: (bi * nt + ti, 0)),
            pl.BlockSpec((1, POOL_PAD, d), lambda bi, ti: (bi, 0, 0)),
        ],
        out_shape=[
            jax.ShapeDtypeStruct((m, d), F32),
            jax.ShapeDtypeStruct((b, POOL_PAD, d), F32),
        ],
        scratch_shapes=[pltpu.VMEM((tt + 2 * POOL_PAD, d - gi * gcw), F32) for gi in (0, *range(g - 1))],
        compiler_params=_cparams(("parallel", "arbitrary")),
        name="pool_mixer",
    )(x, nw, hist, pw, sc)


def _prep_weights(norm_mix_w, norm_ffn_w, final_norm_w, gdn_w_in, gdn_conv_w, gdn_A_log, gdn_dt_bias, gdn_norm_w,
                  gdn_w_out, pool_w, pool_scale, ffn_w_gu, ffn_w_down):
    n_a, d, _ = gdn_w_in.shape
    nv = gdn_A_log.shape[1]
    val_dim = nv * DV
    qkv_dim = gdn_conv_w.shape[2]
    main = qkv_dim + val_dim
    zeros = jnp.zeros((n_a, d, nv), F32)
    w_gate = jnp.stack([gdn_w_in[:, :, main:main + nv], gdn_w_in[:, :, main + nv:main + 2 * nv], zeros, zeros],
                       axis=-1).reshape(n_a, d, nv * GATE_SLOTS).astype(BF16)
    zv = jnp.zeros((n_a, nv), F32)
    alog = jnp.stack([zv, gdn_A_log.astype(F32), zv, zv], axis=-1).reshape(n_a, 1, nv * GATE_SLOTS)
    dtb = jnp.stack([zv, gdn_dt_bias.astype(F32), zv, zv], axis=-1).reshape(n_a, 1, nv * GATE_SLOTS)
    nh = qkv_dim // LANES
    cw = gdn_conv_w.reshape(n_a, CONV_W, nh, LANES).transpose(0, 2, 1, 3)
    cw = jnp.pad(cw, ((0, 0), (0, 0), (0, HIST_ROWS - CONV_W), (0, 0)))
    return dict(
        norm_mix=norm_mix_w[:, None, :], norm_ffn=norm_ffn_w[:, None, :], final=final_norm_w[None, :],
        w_in=gdn_w_in.astype(BF16), n_main=main, w_gate=w_gate, alog=alog, dtb=dtb, cw=cw,
        gnorm=gdn_norm_w[:, None, :], w_out=gdn_w_out.astype(BF16), pool_w=pool_w.astype(BF16),
        pool_scale=pool_scale[:, None, :], w_gu=ffn_w_gu.astype(BF16), w_down=ffn_w_down.astype(BF16),
    )


def _gdn_layer(x, conv_hist, s_hist, wts, i, j, *, b, t):
    m, d = x.shape
    nv = s_hist.shape[1]
    nk = nv // 2
    qkv_dim = conv_hist.shape[-1]
    nh = qkv_dim // LANES
    if t % GDN_TILE == 0:
        tp, tt, heads = t, GDN_TILE, GDN_HEADS
        xp = x
    else:
        tp = -(-t // CHUNK) * CHUNK
        tt, heads = tp, nk
        xp = jnp.pad(x.reshape(b, t, d), ((0, 0), (0, tp - t), (0, 0))).reshape(b * tp, d)
    mp = b * tp
    ph, ba = _norm_proj(xp, wts["norm_mix"][i], wts["w_in"], wts["w_gate"][j], layer=j, n=wts["n_main"],
                        tm=min(1024, mp), tn=1536)
    gates = _gates(ba, wts["alog"][j], wts["dtb"][j], c=CHUNK, tt=min(mp, 2048), valid=t, period=tp)
    hist = conv_hist.reshape(b, CONV_W - 1, nh, LANES).transpose(2, 0, 1, 3)
    hist = jnp.pad(hist, ((0, 0), (0, 0), (HIST_ROWS - (CONV_W - 1), 0), (0, 0)))
    o, s_new = _gdn_core(ph, hist, wts["cw"][j], gates, s_hist.astype(F32), wts["gnorm"][j], b=b, t=tp, nk=nk,
                         nv=nv, tt=tt, heads=heads, unroll=min(GDN_UNROLL, heads * tt // CHUNK))
    if tp != t:
        o = o.reshape(b, tp, o.shape[-1])[:, :t].reshape(m, o.shape[-1])
    x = _mm_residual(o, wts["w_out"], x, layer=j, tm=min(1024, m), tn=1024)
    tail = ph.reshape(ph.shape[0], b, tp, LANES)[:nh, :, t - (CONV_W - 1):t, :]
    return x, tail.transpose(1, 2, 0, 3).reshape(b, CONV_W - 1, qkv_dim), s_new


def _trunk(x3, conv_hist, s_hist, pool_hist, wts, depth):
    b, t, d = x3.shape
    m = b * t
    x = x3.reshape(m, d)
    past = pool_hist.shape[2]
    assert t >= CONV_W - 1 and t >= POOL_PAD - 1
    new_conv, new_s, new_pool = [], [], []
    for i in range(depth):
        j = i // 2
        if i % 2 == 0:
            x, conv_new, s_new = _gdn_layer(x, conv_hist[j], s_hist[j], wts, i, j, b=b, t=t)
            new_conv.append(conv_new)
            new_s.append(s_new)
        else:
            hist = jnp.pad(pool_hist[j], ((0, 0), (POOL_PAD - past, 0), (0, 0)))
            x, hs = _pool_mixer(x, wts["norm_mix"][i], hist, wts["pool_w"][j], wts["pool_scale"][j],
                                b=b, t=t, tt=min(t, 256), past=past)
            new_pool.append(hs[:, 1:, :])
        x = _ffn(x, wts["norm_ffn"][i], wts["w_gu"], wts["w_down"], wts["final"], layer=i, tm=min(1024, m), tf=512,
                 final=(i == depth - 1))
    return x.reshape(b, t, d), jnp.stack(new_conv), jnp.stack(new_s), jnp.stack(new_pool)


def kernel(x_prompt, x_sample, state_gdn_conv, state_gdn_S, state_pool, norm_mix_w, norm_ffn_w, final_norm_w,
           gdn_w_in, gdn_conv_w, gdn_A_log, gdn_dt_bias, gdn_norm_w, gdn_w_out, pool_w, pool_scale, ffn_w_gu,
           ffn_w_down):
    depth = norm_mix_w.shape[0]
    n_a, n_b = state_gdn_conv.shape[0], state_pool.shape[0]
    bp = x_prompt.shape[0]
    wts = _prep_weights(norm_mix_w, norm_ffn_w, final_norm_w, gdn_w_in, gdn_conv_w, gdn_A_log, gdn_dt_bias,
                        gdn_norm_w, gdn_w_out, pool_w, pool_scale, ffn_w_gu, ffn_w_down)
    conv0 = jnp.zeros((n_a, bp) + state_gdn_conv.shape[2:], x_prompt.dtype)
    s00 = jnp.zeros((n_a, bp) + state_gdn_S.shape[2:], F32)
    pool0 = jnp.zeros((n_b, bp, 0, x_prompt.shape[-1]), x_prompt.dtype)
    y_p, conv_p, s_p, pool_p = _trunk(x_prompt, conv0, s00, pool0, wts, depth)
    y_s, conv_s, s_s, pool_s = _trunk(x_sample, state_gdn_conv, state_gdn_S, state_pool, wts, depth)
    return (y_p, y_s, conv_p, s_p.astype(state_gdn_S.dtype), pool_p, conv_s, s_s.astype(state_gdn_S.dtype), pool_s)
```

```python
import functools

import jax
import jax.numpy as jnp
from jax import lax
from jax.experimental import pallas as pl
from jax.experimental.pallas import tpu as pltpu

F32 = jnp.float32
BF16 = jnp.bfloat16

EPS = 1e-6
LANES = 128
DK = 128
DV = 128
CONV_W = 4
HIST_ROWS = 8
CHUNK = 64
INV_BLOCK = 16
GDN_TILE = 128
GDN_HEADS = 16
GDN_UNROLL = 32
POOL_WINDOWS = (2, 4, 8, 16)
POOL_PAD = 16
GATE_SLOTS = 4
V7X_VMEM_BYTES = 64 * 1024 * 1024
VMEM_LIMIT = V7X_VMEM_BYTES - 8 * 1024 * 1024
ROW_TILE = 1024
PROJ_COLS = 1536
OUT_COLS = 1024
FFN_COLS = 512
POOL_ROWS = 512
GATE_ROWS = 2048


def _cparams(sem):
    return pltpu.CompilerParams(dimension_semantics=sem, vmem_limit_bytes=VMEM_LIMIT)


def _rms(x, w):
    ms = jnp.mean(x * x, axis=-1, keepdims=True)
    return x * lax.rsqrt(ms + EPS) * w


def _silu(x):
    h = 0.5 * x
    return h * jnp.tanh(h) + h


def _norm_proj_kernel(x_ref, nw_ref, w_ref, wg_ref, o_ref, g_ref, h_ref, *, tn):
    @pl.when(pl.program_id(1) == 0)
    def _():
        h = _rms(x_ref[...], nw_ref[...]).astype(BF16)
        h_ref[...] = h
        g_ref[...] = jnp.dot(h, wg_ref[...], preferred_element_type=F32)

    acc = jnp.dot(h_ref[...], w_ref[...], preferred_element_type=F32)
    for c in range(tn // LANES):
        o_ref[c] = acc[:, c * LANES:(c + 1) * LANES]


def _norm_proj(x, nw, w, wg, *, layer, n, tm, tn):
    m, d = x.shape
    assert n % tn == 0 and n <= w.shape[2] and m % tm == 0
    return pl.pallas_call(
        functools.partial(_norm_proj_kernel, tn=tn),
        grid=(m // tm, n // tn),
        in_specs=[
            pl.BlockSpec((tm, d), lambda i, j: (i, 0)),
            pl.BlockSpec((1, d), lambda i, j: (0, 0)),
            pl.BlockSpec((None, d, tn), lambda i, j: (layer, 0, j)),
            pl.BlockSpec((d, LANES), lambda i, j: (0, 0)),
        ],
        out_specs=[
            pl.BlockSpec((tn // LANES, tm, LANES), lambda i, j: (j, i, 0)),
            pl.BlockSpec((tm, LANES), lambda i, j: (i, 0)),
        ],
        out_shape=[
            jax.ShapeDtypeStruct((n // LANES, m, LANES), F32),
            jax.ShapeDtypeStruct((m, LANES), F32),
        ],
        scratch_shapes=[pltpu.VMEM((tm, d), BF16)],
        compiler_params=_cparams(("parallel", "arbitrary")),
        name="norm_proj",
    )(x, nw, w, wg)


def _gates_kernel(ba_ref, alog_ref, dtb_ref, tri_ref, o_ref, *, c, valid, period):
    x = ba_ref[...]
    beta = jax.nn.sigmoid(x)
    y = x + dtb_ref[...]
    softplus = jnp.maximum(y, 0.0) + jnp.log1p(jnp.exp(-jnp.abs(y)))
    g = -jnp.exp(alog_ref[...]) * softplus
    if valid != period:
        assert period & (period - 1) == 0 and x.shape[0] % period == 0
        row = lax.broadcasted_iota(jnp.int32, x.shape, 0) & (period - 1)
        beta = jnp.where(row < valid, beta, 0.0)
        g = jnp.where(row < valid, g, 0.0)
    lane = lax.broadcasted_iota(jnp.int32, (c, x.shape[1]), 1)
    is_beta = (lane & (GATE_SLOTS - 1)) == 0
    tri = tri_ref[...]
    for n in range(x.shape[0] // c):
        sl = slice(n * c, (n + 1) * c)
        gc = jnp.dot(tri, g[sl], preferred_element_type=F32, precision=lax.Precision.HIGHEST)
        o_ref[sl, :] = jnp.where(is_beta, beta[sl], gc)


def _gates(ba, alog, dtb, *, c, tt, valid, period):
    m = ba.shape[0]
    tri = jnp.tril(jnp.ones((c, c), F32))
    return pl.pallas_call(
        functools.partial(_gates_kernel, c=c, valid=valid, period=period),
        grid=(m // tt,),
        in_specs=[
            pl.BlockSpec((tt, LANES), lambda i: (i, 0)),
            pl.BlockSpec((1, LANES), lambda i: (0, 0)),
            pl.BlockSpec((1, LANES), lambda i: (0, 0)),
            pl.BlockSpec((c, c), lambda i: (0, 0)),
        ],
        out_specs=pl.BlockSpec((tt, LANES), lambda i: (i, 0)),
        out_shape=jax.ShapeDtypeStruct((m, LANES), F32),
        compiler_params=_cparams(("parallel",)),
        name="gates",
    )(ba, alog, dtb, tri)


def _inv_unit_lower(l_mats):
    c, w = l_mats[0].shape
    npack = w // c
    r = lax.broadcasted_iota(jnp.int32, (c, w), 0)
    lane = lax.broadcasted_iota(jnp.int32, (c, w), 1)
    s = lane & (c - 1)
    shift = INV_BLOCK.bit_length() - 1
    same = lax.shift_right_logical(r, shift) == lax.shift_right_logical(s, shift)
    eye = jnp.where(r == s, 1.0, 0.0).astype(BF16)
    zero = jnp.zeros((c, w), BF16)
    part = [lax.shift_right_logical(lane, c.bit_length() - 1) == p for p in range(npack)]

    def mm(xs, ys):
        if npack > 1:
            ys = [jnp.concatenate([jnp.where(part[p], y, zero) for p in range(npack)], axis=0) for y in ys]
        return [jnp.dot(x, y, preferred_element_type=F32).astype(BF16) for x, y in zip(xs, ys)]

    def stack(xs, ys):
        return [jnp.concatenate([x, y], axis=0) for x, y in zip(xs, ys)]

    d1 = [jnp.where(same, l, 0.0).astype(BF16) for l in l_mats]
    d2 = mm(d1, d1)
    t = mm(stack(d2, d1), d2)
    d4 = [x[:c] for x in t]
    y = [eye - a + b - x[c:] for a, b, x in zip(d1, d2, t)]
    t = mm(stack(d4, y), d4)
    y = [a + x[c:] for a, x in zip(y, t)]
    x0 = [a + b for a, b in zip(y, mm(y, [x[:c] for x in t]))]
    nblk = c // INV_BLOCK
    if nblk == 1:
        return x0
    e = [jnp.where(same, 0.0, l).astype(BF16) for l in l_mats]
    nmat = mm(x0, e)
    z = [eye - a for a in nmat]
    if nblk > 2:
        assert nblk == 4
        n2 = mm(nmat, nmat)
        n3 = mm(nmat, n2)
        z = [a + b - x for a, b, x in zip(z, n2, n3)]
    return mm(z, x0)


def _gdn_kernel(q_ref, k_ref, v_ref, z_ref, hq_ref, hk_ref, hv_ref, cq_ref, ck_ref, cv_ref, g_ref, e_ref, s0_ref,
                nw_ref, o_ref, s_ref, xc_s, hist_s, gl_s, kd_s, wq_s, u_s, a_s, st_s, *, tt, c, heads, unroll):
    n = tt // c
    assert 2 * c == LANES

    @pl.when(pl.program_id(2) == 0)
    def _():
        for g in range(heads):
            hist_s[g, 0] = hq_ref[g, 0]
            hist_s[g, 1] = hk_ref[g, 0]
            for h in range(2):
                hist_s[g, 2 + h] = hv_ref[2 * g + h, 0]
                st_s[g, :, h * DV:(h + 1) * DV] = s0_ref[0, 2 * g + h]

    def l2n(x):
        return x * lax.rsqrt(jnp.sum(x * x, axis=-1, keepdims=True) + EPS)

    zero_k = jnp.zeros((c, LANES), BF16)
    base = HIST_ROWS - (CONV_W - 1)
    ghl = []
    for nn in range(n):
        graw = g_ref[nn * c:(nn + 1) * c, :]
        hi = graw.astype(BF16)
        ghl.append(jnp.concatenate([hi, (graw - hi.astype(F32)).astype(BF16)], axis=1))

    for g in range(heads):
        xin = (q_ref[g], k_ref[g], v_ref[2 * g], v_ref[2 * g + 1])
        for slot in range(4):
            xc_s[g, slot, 0:HIST_ROWS, :] = hist_s[g, slot]
            xc_s[g, slot, HIST_ROWS:HIST_ROWS + tt, :] = xin[slot]
            hist_s[g, slot] = xc_s[g, slot, tt:tt + HIST_ROWS, :]

    ex_all = [jnp.dot(ghl[nn], e_ref[g], preferred_element_type=F32) for nn in range(n) for g in range(heads)]

    def prep(job):
        nn, g = divmod(job, heads)
        r0 = nn * c
        cws =(cq_ref[g], ck_ref[g], cv_ref[2 * g], cv_ref[2 * g + 1])

        def conv_silu(slot):
            acc = xc_s[g, slot, r0 + base:r0 + base + c, :] * cws[slot][0:1, :]
            for j in range(1, CONV_W):
                acc = acc + xc_s[g, slot, r0 + base + j:r0 + base + j + c, :] * cws[slot][j:j + 1, :]
            return _silu(acc)

        ex = ex_all[job]
        gcol, bcol = ex[:, 0:LANES], ex[:, LANES:2 * LANES]
        gc = [ex[:, (2 + h) * LANES:(3 + h) * LANES] for h in range(2)]
        eg = [jnp.exp(x) for x in gc]
        gl = [x[c - 1:c, :] for x in gc]
        for h in range(2):
            gl_s[job, :, h * LANES:(h + 1) * LANES] = jnp.exp(gl[h])

        q = l2n(conv_silu(0)) * (DK ** -0.5)
        k = l2n(conv_silu(1))
        k16 = k.astype(BF16)
        v16 = [conv_silu(2 + h).astype(BF16) for h in range(2)]
        for h in range(2):
            wq_s[job, c:2 * c, h * DK:(h + 1) * DK] = (q * eg[h]).astype(BF16)
            kd_s[job, h * c:(h + 1) * c, :] = (k * jnp.exp(gl[h] - gc[h])).astype(BF16)
        return dict(gcol=gcol, bcol=bcol, begcol=bcol * jnp.exp(gcol),
                    qk=jnp.concatenate([q.astype(BF16), k16], axis=0), kk=jnp.concatenate([k16, k16], axis=0),
                    kbd=jnp.concatenate([jnp.concatenate([k16, zero_k], axis=1),
                                         jnp.concatenate([zero_k, k16], axis=1)], axis=0),
                    vbd=jnp.concatenate([jnp.concatenate([v16[0], zero_k], axis=1),
                                         jnp.concatenate([zero_k, v16[1]], axis=1)], axis=0))

    r = lax.broadcasted_iota(jnp.int32, (c, LANES), 0)
    sloc = lax.broadcasted_iota(jnp.int32, (c, LANES), 1) & (c - 1)
    causal = r >= sloc
    strict = r > sloc
    diag = r == sloc

    def as_row(x):
        return jnp.sum(jnp.where(diag, x, 0.0), axis=0, keepdims=True)

    def intra(js, ops):
        gcol = [p["gcol"] for p in ops]
        bcl = [p["bcol"] for p in ops]
        begcol = [p["begcol"] for p in ops]
        decay = [jnp.where(causal, jnp.exp(jnp.where(causal, x - as_row(x), 0.0)), 0.0) for x in gcol]
        qkk = [lax.dot_general(p["qk"], p["kk"], (((1,), (1,)), ((), ())), preferred_element_type=F32)
               for p in ops]
        for j, x, d in zip(js, qkk, decay):
            a_s[j] = (x[:c] * d).astype(BF16)
        tm = _inv_unit_lower([jnp.where(strict, x[c:] * d * b, 0.0) for x, d, b in zip(qkk, decay, bcl)])
        w = [jnp.dot(x * as_row(gg).astype(BF16), p["kbd"], preferred_element_type=F32)
             for p, x, gg in zip(ops, tm, begcol)]
        u = [jnp.dot(x * as_row(b).astype(BF16), p["vbd"], preferred_element_type=F32)
             for p, x, b in zip(ops, tm, bcl)]
        for j, wn, un in zip(js, w, u):
            u_s[j] = un
            wq_s[j, 0:c, :] = wn.astype(BF16)

    for i in range(heads * n // unroll):
        js = list(range(i * unroll, (i + 1) * unroll))
        intra(js, [prep(job) for job in js])

    zero_c = jnp.zeros((c, DV), BF16)
    zero_s = jnp.zeros((DK, DV), BF16)

    def blockdiag(x, zero):
        return jnp.concatenate([jnp.concatenate([x[:, :DV], zero], axis=1),
                                jnp.concatenate([zero, x[:, DV:]], axis=1)], axis=0)

    def scan(nn):
        js = [nn * heads + g for g in range(heads)]
        sp = [st_s[g] for g in range(heads)]
        r1 = [jnp.dot(wq_s[j], blockdiag(x.astype(BF16), zero_s), preferred_element_type=F32)
              for j, x in zip(js, sp)]
        vnbd = [blockdiag((u_s[j] - x[0:c]).astype(BF16), zero_c) for j, x in zip(js, r1)]
        av = [jnp.dot(a_s[j], v, preferred_element_type=F32) for j, v in zip(js, vnbd)]
        upd = [lax.dot_general(kd_s[j], v, (((0,), (0,)), ((), ())), preferred_element_type=F32)
               for j, v in zip(js, vnbd)]
        rows = slice(nn * c, (nn + 1) * c)
        for g, j in enumerate(js):
            st_s[g] = sp[g] * gl_s[j] + upd[g]
            for h in range(2):
                hv = 2 * g + h
                o = r1[g][c:2 * c, h * DV:(h + 1) * DV] + av[g][:, h * DV:(h + 1) * DV]
                o = (o * lax.rsqrt(jnp.mean(o * o, axis=-1, keepdims=True) + EPS) * nw_ref[...]
                     * _silu(z_ref[hv, rows, :]))
                o_ref[rows, hv * DV:(hv + 1) * DV] = o.astype(o_ref.dtype)

    for nn in range(n):
        scan(nn)

    for g in range(heads):
        for h in range(2):
            s_ref[0, 2 * g + h] = st_s[g, :, h * DV:(h + 1) * DV]


def _gate_selectors(nk):
    kh = jnp.arange(nk, dtype=jnp.int32)[:, None, None]
    row = jnp.arange(2 * LANES, dtype=jnp.int32)[None, :, None] & (LANES - 1)
    col = jnp.arange(4 * LANES, dtype=jnp.int32)[None, None, :]
    blk, second = col // LANES, (col % LANES) >= LANES // 2
    head = jnp.where(blk < 2, second.astype(jnp.int32), blk - 2)
    slot = jnp.where(blk == 1, 0, 1)
    return (row == GATE_SLOTS * (2 * kh + head) + slot).astype(BF16)


def _gdn_core(ph, hist, cw, gates, s0, nw, *, b, t, nk, nv, tt, heads, unroll):
    c = CHUNK
    assert t % tt == 0 and tt % c == 0 and nk % heads == 0 and nv == 2 * nk and DK == LANES and DV == LANES
    assert (heads * tt // c) % unroll == 0
    m = b * t
    nt = t // tt
    jobs = heads * tt // c
    vh = 2 * heads
    off_k, off_v, off_z = nk // heads, 2 * nk // vh, (2 * nk + nv) // vh
    seq = lambda off: (lambda bi, h, ti: (off + h, bi * nt + ti, 0))
    hst = lambda off: (lambda bi, h, ti: (off + h, bi, 0, 0))
    cwt = lambda off: (lambda bi, h, ti: (off + h, 0, 0))
    f32 = lambda *shape: pltpu.VMEM(shape, F32)
    bf16 = lambda *shape: pltpu.VMEM(shape, BF16)
    return pl.pallas_call(
        functools.partial(_gdn_kernel, tt=tt, c=c, heads=heads, unroll=unroll),
        grid=(b, nk // heads, nt),
        in_specs=[
            pl.BlockSpec((heads, tt, LANES), seq(0)), pl.BlockSpec((heads, tt, LANES), seq(off_k)),
            pl.BlockSpec((vh, tt, LANES), seq(off_v)), pl.BlockSpec((vh, tt, LANES), seq(off_z)),
            pl.BlockSpec((heads, 1, HIST_ROWS, LANES), hst(0)), pl.BlockSpec((heads, 1, HIST_ROWS, LANES), hst(off_k)),
            pl.BlockSpec((vh, 1, HIST_ROWS, LANES), hst(off_v)),
            pl.BlockSpec((heads, HIST_ROWS, LANES), cwt(0)), pl.BlockSpec((heads, HIST_ROWS, LANES), cwt(off_k)),
            pl.BlockSpec((vh, HIST_ROWS, LANES), cwt(off_v)),
            pl.BlockSpec((tt, LANES), lambda bi, h, ti: (bi * nt + ti, 0)),
            pl.BlockSpec((heads, 2 * LANES, 4 * LANES), lambda bi, h, ti: (h, 0, 0)),
            pl.BlockSpec((1, vh, DK, DV), lambda bi, h, ti: (bi, h, 0, 0)),
            pl.BlockSpec((1, DV), lambda bi, h, ti: (0, 0)),
        ],
        out_specs=[
            pl.BlockSpec((tt, vh * DV), lambda bi, h, ti: (bi * nt + ti, h)),
            pl.BlockSpec((1, vh, DK, DV), lambda bi, h, ti: (bi, h, 0, 0)),
        ],
        out_shape=[
            jax.ShapeDtypeStruct((m, nv * DV), BF16),
            jax.ShapeDtypeStruct((b, nv, DK, DV), F32),
        ],
        scratch_shapes=[
            f32(heads, 4, tt + HIST_ROWS, LANES), f32(heads, 4, HIST_ROWS, LANES), f32(jobs, 1, 2 * LANES),
            bf16(jobs, 2 * c, LANES), bf16(jobs, 2 * c, 2 * LANES),
            f32(jobs, c, 2 * DV), bf16(jobs, c, LANES), f32(heads, DK, 2 * DV),
        ],
        compiler_params=_cparams(("parallel", "parallel", "arbitrary")),
        name="gdn_core",
    )(ph, ph, ph, ph, hist, hist, hist, cw, cw, cw, gates, _gate_selectors(nk), s0, nw)


def _mm_res_kernel(a_ref, w_ref, r_ref, o_ref):
    o_ref[...] = r_ref[...] + jnp.dot(a_ref[...], w_ref[...], preferred_element_type=F32)


def _mm_residual(a, w, res, *, layer, tm, tn):
    m, k = a.shape
    n = w.shape[2]
    return pl.pallas_call(
        _mm_res_kernel,
        grid=(m // tm, n // tn),
        in_specs=[
            pl.BlockSpec((tm, k), lambda i, j: (i, 0)),
            pl.BlockSpec((None, k, tn), lambda i, j: (layer, 0, j)),
            pl.BlockSpec((tm, tn), lambda i, j: (i, j)),
        ],
        out_specs=pl.BlockSpec((tm, tn), lambda i, j: (i, j)),
        out_shape=jax.ShapeDtypeStruct((m, n), F32),
        compiler_params=_cparams(("parallel", "arbitrary")),
        name="mm_residual",
    )(a, w, res)


def _ffn_kernel(x_ref, nw_ref, wg_ref, wu_ref, wd_ref, fw_ref, o_ref, h_ref, *, final):
    j = pl.program_id(1)

    @pl.when(j == 0)
    def _():
        x = x_ref[...]
        h_ref[...] = _rms(x, nw_ref[...]).astype(BF16)
        o_ref[...] = x

    h = h_ref[...]
    g = jnp.dot(h, wg_ref[...], preferred_element_type=F32)
    u = jnp.dot(h, wu_ref[...], preferred_element_type=F32)
    act = (_silu(g) * u).astype(BF16)
    o_ref[...] += jnp.dot(act, wd_ref[...], preferred_element_type=F32)

    if final:
        @pl.when(j == pl.num_programs(1) - 1)
        def _():
            o_ref[...] = _rms(o_ref[...], fw_ref[...])


def _ffn(x, nw, wgu, wd, fw, *, layer, tm, tf, final):
    m, d = x.shape
    f = wd.shape[1]
    nf = f // tf
    return pl.pallas_call(
        functools.partial(_ffn_kernel, final=final),
        grid=(m // tm, nf),
        in_specs=[
            pl.BlockSpec((tm, d), lambda i, j: (i, 0)),
            pl.BlockSpec((1, d), lambda i, j: (0, 0)),
            pl.BlockSpec((None, d, tf), lambda i, j: (layer, 0, j)),
            pl.BlockSpec((None, d, tf), lambda i, j: (layer, 0, nf + j)),
            pl.BlockSpec((None, tf, d), lambda i, j: (layer, j, 0)),
            pl.BlockSpec((1, d), lambda i, j: (0, 0)),
        ],
        out_specs=pl.BlockSpec((tm, d), lambda i, j: (i, 0)),
        out_shape=jax.ShapeDtypeStruct((m, d), F32),
        scratch_shapes=[pltpu.VMEM((tm, d), BF16)],
        compiler_params=_cparams(("parallel", "arbitrary")),
        name="ffn",
    )(x, nw, wgu, wgu, wd, fw)


def _pool_kernel(x_ref, nw_ref, hist_ref, pw_ref, sc_ref, o_ref, hs_ref, hbuf, *lvl, tt, past):
    ti = pl.program_id(1)
    pad = 2 * POOL_PAD
    assert all(w == 2 << g for g, w in enumerate(POOL_WINDOWS)) and POOL_WINDOWS[-1] // 2 <= 8

    @pl.when(ti == 0)
    def _():
        hbuf[0:POOL_PAD, :] = jnp.zeros((POOL_PAD, hbuf.shape[1]), F32)
        hbuf[POOL_PAD:pad, :] = hist_ref[0]

    @pl.when(ti > 0)
    def _():
        hbuf[0:pad, :] = hbuf[tt:tt + pad, :]

    x = x_ref[...]
    h = _rms(x, nw_ref[...])
    hbuf[pad:pad + tt, :] = h
    tg = ti * tt + lax.broadcasted_iota(jnp.int32, (tt, 1), 0)
    d = x.shape[1]
    gcw = d // len(POOL_WINDOWS)
    prev, prev_col0 = hbuf, 0
    for gi, wlen in enumerate(POOL_WINDOWS):
        start, shift, col0 = 8 * (gi + 1), wlen // 2, gi * gcw
        rel = slice(col0 - prev_col0, d - prev_col0)
        cur = prev[start:pad + tt, rel] + prev[start - shift:pad + tt - shift, rel]
        if gi + 1 < len(POOL_WINDOWS):
            lvl[gi][start:pad + tt, :] = cur
            prev, prev_col0 = lvl[gi], col0
        cols = slice(col0, col0 + gcw)
        cnt = jnp.minimum(wlen, past + tg + 1).astype(F32)
        dlt = cur[pad - start:, 0:gcw] / cnt - h[:, cols]
        y = jnp.dot(dlt.astype(BF16), pw_ref[gi], preferred_element_type=F32)
        o_ref[:, cols] = x[:, cols] + y * sc_ref[:, cols]
    hs_ref[0] = hbuf[tt + POOL_PAD:tt + pad, :]


def _pool_mixer(x, nw, hist, pw, sc, *, b, t, tt, past):
    m, d = x.shape
    nt = t // tt
    g, gcw, _ = pw.shape
    return pl.pallas_call(
        functools.partial(_pool_kernel, tt=tt, past=past),
        grid=(b, nt),
        in_specs=[
            pl.BlockSpec((tt, d), lambda bi, ti: (bi * nt + ti, 0)),
            pl.BlockSpec((1, d), lambda bi, ti: (0, 0)),
            pl.BlockSpec((1, POOL_PAD, d), lambda bi, ti: (bi, 0, 0)),
            pl.BlockSpec((g, gcw, gcw), lambda bi, ti: (0, 0, 0)),
            pl.BlockSpec((1, d), lambda bi, ti: (0, 0)),
        ],
        out_specs=[
            pl.BlockSpec((tt, d), lambda bi, ti: (bi * nt + ti, 0)),
            pl.BlockSpec((1, POOL_PAD, d), lambda bi, ti: (bi, 0, 0)),
        ],
        out_shape=[
            jax.ShapeDtypeStruct((m, d), F32),
            jax.ShapeDtypeStruct((b, POOL_PAD, d), F32),
        ],
        scratch_shapes=[pltpu.VMEM((tt + 2 * POOL_PAD, d - gi * gcw), F32) for gi in (0, *range(g - 1))],
        compiler_params=_cparams(("parallel", "arbitrary")),
        name="pool_mixer",
    )(x, nw, hist, pw, sc)


def _prep_weights(norm_mix_w, norm_ffn_w, final_norm_w, gdn_w_in, gdn_conv_w, gdn_A_log, gdn_dt_bias, gdn_norm_w,
                  gdn_w_out, pool_w, pool_scale, ffn_w_gu, ffn_w_down):
    n_a, d, _ = gdn_w_in.shape
    nv = gdn_A_log.shape[1]
    val_dim = nv * DV
    qkv_dim = gdn_conv_w.shape[2]
    main = qkv_dim + val_dim
    zeros = jnp.zeros((n_a, d, nv), F32)
    w_gate = jnp.stack([gdn_w_in[:, :, main:main + nv], gdn_w_in[:, :, main + nv:main + 2 * nv], zeros, zeros],
                       axis=-1).reshape(n_a, d, nv * GATE_SLOTS).astype(BF16)
    zv = jnp.zeros((n_a, nv), F32)
    alog = jnp.stack([zv, gdn_A_log.astype(F32), zv, zv], axis=-1).reshape(n_a, 1, nv * GATE_SLOTS)
    dtb = jnp.stack([zv, gdn_dt_bias.astype(F32), zv, zv], axis=-1).reshape(n_a, 1, nv * GATE_SLOTS)
    nh = qkv_dim // LANES
    cw = gdn_conv_w.reshape(n_a, CONV_W, nh, LANES).transpose(0, 2, 1, 3)
    cw = jnp.pad(cw, ((0, 0), (0, 0), (0, HIST_ROWS - CONV_W), (0, 0)))
    return dict(
        norm_mix=norm_mix_w[:, None, :], norm_ffn=norm_ffn_w[:, None, :], final=final_norm_w[None, :],
        w_in=gdn_w_in.astype(BF16), n_main=main, w_gate=w_gate, alog=alog, dtb=dtb, cw=cw,
        gnorm=gdn_norm_w[:, None, :], w_out=gdn_w_out.astype(BF16), pool_w=pool_w.astype(BF16),
        pool_scale=pool_scale[:, None, :], w_gu=ffn_w_gu.astype(BF16), w_down=ffn_w_down.astype(BF16),
    )


def _gdn_layer(x, conv_hist, s_hist, wts, i, j, *, b, t):
    m, d = x.shape
    nv = s_hist.shape[1]
    nk = nv // 2
    qkv_dim = conv_hist.shape[-1]
    nh = qkv_dim // LANES
    if t % GDN_TILE == 0:
        tp, tt, heads = t, GDN_TILE, GDN_HEADS
        xp = x
    else:
        tp = -(-t // CHUNK) * CHUNK
        tt, heads = tp, nk
        xp = jnp.pad(x.reshape(b, t, d), ((0, 0), (0, tp - t), (0, 0))).reshape(b * tp, d)
    mp = b * tp
    ph, ba = _norm_proj(xp, wts["norm_mix"][i], wts["w_in"], wts["w_gate"][j], layer=j, n=wts["n_main"],
                        tm=min(ROW_TILE, mp), tn=PROJ_COLS)
    gates = _gates(ba, wts["alog"][j], wts["dtb"][j], c=CHUNK, tt=min(mp, GATE_ROWS), valid=t, period=tp)
    hist = conv_hist.reshape(b, CONV_W - 1, nh, LANES).transpose(2, 0, 1, 3)
    hist = jnp.pad(hist, ((0, 0), (0, 0), (HIST_ROWS - (CONV_W - 1), 0), (0, 0)))
    o, s_new = _gdn_core(ph, hist, wts["cw"][j], gates, s_hist.astype(F32), wts["gnorm"][j], b=b, t=tp, nk=nk,
                         nv=nv, tt=tt, heads=heads, unroll=min(GDN_UNROLL, heads * tt // CHUNK))
    if tp != t:
        o = o.reshape(b, tp, o.shape[-1])[:, :t].reshape(m, o.shape[-1])
    x = _mm_residual(o, wts["w_out"], x, layer=j, tm=min(ROW_TILE, m), tn=OUT_COLS)
    tail = ph.reshape(ph.shape[0], b, tp, LANES)[:nh, :, t - (CONV_W - 1):t, :]
    return x, tail.transpose(1, 2, 0, 3).reshape(b, CONV_W - 1, qkv_dim), s_new


def _trunk(x3, conv_hist, s_hist, pool_hist, wts, depth):
    b, t, d = x3.shape
    m = b * t
    x = x3.reshape(m, d)
    past = pool_hist.shape[2]
    assert t >= CONV_W - 1 and t >= POOL_PAD - 1
    new_conv, new_s, new_pool = [], [], []
    for i in range(depth):
        j = i // 2
        if i % 2 == 0:
            x, conv_new, s_new = _gdn_layer(x, conv_hist[j], s_hist[j], wts, i, j, b=b, t=t)
            new_conv.append(conv_new)
            new_s.append(s_new)
        else:
            hist = jnp.pad(pool_hist[j], ((0, 0), (POOL_PAD - past, 0), (0, 0)))
            x, hs = _pool_mixer(x, wts["norm_mix"][i], hist, wts["pool_w"][j], wts["pool_scale"][j],
                                b=b, t=t, tt=min(t, POOL_ROWS), past=past)
            new_pool.append(hs[:, 1:, :])
        x = _ffn(x, wts["norm_ffn"][i], wts["w_gu"], wts["w_down"], wts["final"], layer=i, tm=min(ROW_TILE, m),
                 tf=FFN_COLS, final=(i == depth - 1))
    return x.reshape(b, t, d), jnp.stack(new_conv), jnp.stack(new_s), jnp.stack(new_pool)


def kernel(x_prompt, x_sample, state_gdn_conv, state_gdn_S, state_pool, norm_mix_w, norm_ffn_w, final_norm_w,
           gdn_w_in, gdn_conv_w, gdn_A_log, gdn_dt_bias, gdn_norm_w, gdn_w_out, pool_w, pool_scale, ffn_w_gu,
           ffn_w_down):
    depth = norm_mix_w.shape[0]
    n_a, n_b = state_gdn_conv.shape[0], state_pool.shape[0]
    bp = x_prompt.shape[0]
    wts = _prep_weights(norm_mix_w, norm_ffn_w, final_norm_w, gdn_w_in, gdn_conv_w, gdn_A_log, gdn_dt_bias,
                        gdn_norm_w, gdn_w_out, pool_w, pool_scale, ffn_w_gu, ffn_w_down)
    conv0 = jnp.zeros((n_a, bp) + state_gdn_conv.shape[2:], x_prompt.dtype)
    s00 = jnp.zeros((n_a, bp) + state_gdn_S.shape[2:], F32)
    pool0 = jnp.zeros((n_b, bp, 0, x_prompt.shape[-1]), x_prompt.dtype)
    y_p, conv_p, s_p, pool_p = _trunk(x_prompt, conv0, s00, pool0, wts, depth)
    y_s, conv_s, s_s, pool_s = _trunk(x_sample, state_gdn_conv, state_gdn_S, state_pool, wts, depth)
    return (y_p, y_s, conv_p, s_p.astype(state_gdn_S.dtype), pool_p, conv_s, s_s.astype(state_gdn_S.dtype), pool_s)
```

```python
import functools

import jax
import jax.numpy as jnp
from jax import lax
from jax.experimental import pallas as pl
from jax.experimental.pallas import tpu as pltpu

F32 = jnp.float32
BF16 = jnp.bfloat16

EPS = 1e-6
LANES = 128
DK = 128
DV = 128
CONV_W = 4
HIST_ROWS = 8
CHUNK = 64
INV_BLOCK = 16
GDN_TILE = 128
GDN_HEADS = 16
GDN_UNROLL = 32
POOL_WINDOWS = (2, 4, 8, 16)
POOL_PAD = 16
GATE_SLOTS = 4
V7X_VMEM_BYTES = 64 * 1024 * 1024
VMEM_LIMIT = V7X_VMEM_BYTES - 8 * 1024 * 1024
ROW_TILE = 1024
PROJ_COLS = 1536
OUT_COLS = 1024
FFN_COLS = 512
POOL_ROWS = 512
GATE_ROWS = 2048


def _cparams(sem):
    return pltpu.CompilerParams(dimension_semantics=sem, vmem_limit_bytes=VMEM_LIMIT)


def _rms(x, w):
    ms = jnp.mean(x * x, axis=-1, keepdims=True)
    return x * lax.rsqrt(ms + EPS) * w


def _silu(x):
    h = 0.5 * x
    return h * jnp.tanh(h) + h


def _norm_proj_kernel(x_ref, nw_ref, w_ref, wg_ref, o_ref, g_ref, h_ref, *, tn):
    @pl.when(pl.program_id(1) == 0)
    def _():
        h = _rms(x_ref[...], nw_ref[...]).astype(BF16)
        h_ref[...] = h
        g_ref[...] = jnp.dot(h, wg_ref[...], preferred_element_type=F32)

    acc = jnp.dot(h_ref[...], w_ref[...], preferred_element_type=F32)
    for c in range(tn // LANES):
        o_ref[c] = acc[:, c * LANES:(c + 1) * LANES]


def _norm_proj(x, nw, w, wg, *, layer, n, tm, tn):
    m, d = x.shape
    assert n % tn == 0 and n <= w.shape[2] and m % tm == 0
    return pl.pallas_call(
        functools.partial(_norm_proj_kernel, tn=tn),
        grid=(m // tm, n // tn),
        in_specs=[
            pl.BlockSpec((tm, d), lambda i, j: (i, 0)),
            pl.BlockSpec((1, d), lambda i, j: (0, 0)),
            pl.BlockSpec((None, d, tn), lambda i, j: (layer, 0, j)),
            pl.BlockSpec((d, LANES), lambda i, j: (0, 0)),
        ],
        out_specs=[
            pl.BlockSpec((tn // LANES, tm, LANES), lambda i, j: (j, i, 0)),
            pl.BlockSpec((tm, LANES), lambda i, j: (i, 0)),
        ],
        out_shape=[
            jax.ShapeDtypeStruct((n // LANES, m, LANES), F32),
            jax.ShapeDtypeStruct((m, LANES), F32),
        ],
        scratch_shapes=[pltpu.VMEM((tm, d), BF16)],
        compiler_params=_cparams(("parallel", "arbitrary")),
        name="norm_proj",
    )(x, nw, w, wg)


def _gates_kernel(ba_ref, alog_ref, dtb_ref, tri_ref, o_ref, *, c, valid, period):
    x = ba_ref[...]
    beta = jax.nn.sigmoid(x)
    y = x + dtb_ref[...]
    softplus = jnp.maximum(y, 0.0) + jnp.log1p(jnp.exp(-jnp.abs(y)))
    g = -jnp.exp(alog_ref[...]) * softplus
    if valid != period:
        assert period & (period - 1) == 0 and x.shape[0] % period == 0
        row = lax.broadcasted_iota(jnp.int32, x.shape, 0) & (period - 1)
        beta = jnp.where(row < valid, beta, 0.0)
        g = jnp.where(row < valid, g, 0.0)
    lane = lax.broadcasted_iota(jnp.int32, (c, x.shape[1]), 1)
    is_beta = (lane & (GATE_SLOTS - 1)) == 0
    tri = tri_ref[...]
    for n in range(x.shape[0] // c):
        sl = slice(n * c, (n + 1) * c)
        gc = jnp.dot(tri, g[sl], preferred_element_type=F32, precision=lax.Precision.HIGHEST)
        o_ref[sl, :] = jnp.where(is_beta, beta[sl], gc)


def _gates(ba, alog, dtb, *, c, tt, valid, period):
    m = ba.shape[0]
    tri = jnp.tril(jnp.ones((c, c), F32))
    return pl.pallas_call(
        functools.partial(_gates_kernel, c=c, valid=valid, period=period),
        grid=(m // tt,),
        in_specs=[
            pl.BlockSpec((tt, LANES), lambda i: (i, 0)),
            pl.BlockSpec((1, LANES), lambda i: (0, 0)),
            pl.BlockSpec((1, LANES), lambda i: (0, 0)),
            pl.BlockSpec((c, c), lambda i: (0, 0)),
        ],
        out_specs=pl.BlockSpec((tt, LANES), lambda i: (i, 0)),
        out_shape=jax.ShapeDtypeStruct((m, LANES), F32),
        compiler_params=_cparams(("parallel",)),
        name="gates",
    )(ba, alog, dtb, tri)


def _inv_unit_lower(l_mats):
    c, w = l_mats[0].shape
    npack = w // c
    r = lax.broadcasted_iota(jnp.int32, (c, w), 0)
    lane = lax.broadcasted_iota(jnp.int32, (c, w), 1)
    s = lane & (c - 1)
    shift = INV_BLOCK.bit_length() - 1
    same = lax.shift_right_logical(r, shift) == lax.shift_right_logical(s, shift)
    eye = jnp.where(r == s, 1.0, 0.0).astype(BF16)
    zero = jnp.zeros((c, w), BF16)
    part = [lax.shift_right_logical(lane, c.bit_length() - 1) == p for p in range(npack)]

    def mm(xs, ys):
        if npack > 1:
            ys = [jnp.concatenate([jnp.where(part[p], y, zero) for p in range(npack)], axis=0) for y in ys]
        return [jnp.dot(x, y, preferred_element_type=F32).astype(BF16) for x, y in zip(xs, ys)]

    def stack(xs, ys):
        return [jnp.concatenate([x, y], axis=0) for x, y in zip(xs, ys)]

    d1 = [jnp.where(same, l, 0.0).astype(BF16) for l in l_mats]
    d2 = mm(d1, d1)
    t = mm(stack(d2, d1), d2)
    d4 = [x[:c] for x in t]
    y = [eye - a + b - x[c:] for a, b, x in zip(d1, d2, t)]
    t = mm(stack(d4, y), d4)
    y = [a + x[c:] for a, x in zip(y, t)]
    x0 = [a + b for a, b in zip(y, mm(y, [x[:c] for x in t]))]
    nblk = c // INV_BLOCK
    if nblk == 1:
        return x0
    e = [jnp.where(same, 0.0, l).astype(BF16) for l in l_mats]
    nmat = mm(x0, e)
    z = [eye - a for a in nmat]
    if nblk > 2:
        assert nblk == 4
        n2 = mm(nmat, nmat)
        n3 = mm(nmat, n2)
        z = [a + b - x for a, b, x in zip(z, n2, n3)]
    return mm(z, x0)


def _gdn_kernel(q_ref, k_ref, v_ref, z_ref, hq_ref, hk_ref, hv_ref, cq_ref, ck_ref, cv_ref, g_ref, e_ref, s0_ref,
                nw_ref, *rest, tt, c, heads, unroll):
    o_ref, s_ref, xc_s, hist_s, gl_s, kd_s, wq_s, u_s, a_s, st_s = rest[-10:]
    n = tt // c
    assert 2 * c == LANES

    @pl.when(pl.program_id(2) == 0)
    def _():
        for g in range(heads):
            hist_s[g, 0] = hq_ref[g, 0]
            hist_s[g, 1] = hk_ref[g, 0]
            for h in range(2):
                hist_s[g, 2 + h] = hv_ref[2 * g + h, 0]
                st_s[g, :, h * DV:(h + 1) * DV] = s0_ref[0, 2 * g + h]

    def l2n(x):
        return x * lax.rsqrt(jnp.sum(x * x, axis=-1, keepdims=True) + EPS)

    zero_k = jnp.zeros((c, LANES), BF16)
    base = HIST_ROWS - (CONV_W - 1)
    ghl = []
    for nn in range(n):
        graw = g_ref[nn * c:(nn + 1) * c, :]
        hi = graw.astype(BF16)
        ghl.append(jnp.concatenate([hi, (graw - hi.astype(F32)).astype(BF16)], axis=1))

    for g in range(heads):
        xin = (q_ref[g], k_ref[g], v_ref[2 * g], v_ref[2 * g + 1])
        for slot in range(4):
            xc_s[g, slot, 0:HIST_ROWS, :] = hist_s[g, slot]
            xc_s[g, slot, HIST_ROWS:HIST_ROWS + tt, :] = xin[slot]
            hist_s[g, slot] = xc_s[g, slot, tt:tt + HIST_ROWS, :]

    ex_all = [jnp.dot(ghl[nn], e_ref[g], preferred_element_type=F32) for nn in range(n) for g in range(heads)]

    def prep(job):
        nn, g = divmod(job, heads)
        r0 = nn * c
        cws =(cq_ref[g], ck_ref[g], cv_ref[2 * g], cv_ref[2 * g + 1])

        def conv_silu(slot):
            acc = xc_s[g, slot, r0 + base:r0 + base + c, :] * cws[slot][0:1, :]
            for j in range(1, CONV_W):
                acc = acc + xc_s[g, slot, r0 + base + j:r0 + base + j + c, :] * cws[slot][j:j + 1, :]
            return _silu(acc)

        ex = ex_all[job]
        gcol, bcol = ex[:, 0:LANES], ex[:, LANES:2 * LANES]
        gc = [ex[:, (2 + h) * LANES:(3 + h) * LANES] for h in range(2)]
        eg = [jnp.exp(x) for x in gc]
        gl = [x[c - 1:c, :] for x in gc]
        for h in range(2):
            gl_s[job, :, h * LANES:(h + 1) * LANES] = jnp.exp(gl[h])

        q = l2n(conv_silu(0)) * (DK ** -0.5)
        k = l2n(conv_silu(1))
        k16 = k.astype(BF16)
        v16 = [conv_silu(2 + h).astype(BF16) for h in range(2)]
        for h in range(2):
            wq_s[job, c:2 * c, h * DK:(h + 1) * DK] = (q * eg[h]).astype(BF16)
            kd_s[job, h * c:(h + 1) * c, :] = (k * jnp.exp(gl[h] - gc[h])).astype(BF16)
        return dict(gcol=gcol, bcol=bcol, begcol=bcol * jnp.exp(gcol),
                    qk=jnp.concatenate([q.astype(BF16), k16], axis=0), kk=jnp.concatenate([k16, k16], axis=0),
                    kbd=jnp.concatenate([jnp.concatenate([k16, zero_k], axis=1),
                                         jnp.concatenate([zero_k, k16], axis=1)], axis=0),
                    vbd=jnp.concatenate([jnp.concatenate([v16[0], zero_k], axis=1),
                                         jnp.concatenate([zero_k, v16[1]], axis=1)], axis=0))

    r = lax.broadcasted_iota(jnp.int32, (c, LANES), 0)
    sloc = lax.broadcasted_iota(jnp.int32, (c, LANES), 1) & (c - 1)
    causal = r >= sloc
    strict = r > sloc
    diag = r == sloc

    def as_row(x):
        return jnp.sum(jnp.where(diag, x, 0.0), axis=0, keepdims=True)

    def intra(js, ops):
        gcol = [p["gcol"] for p in ops]
        bcl = [p["bcol"] for p in ops]
        begcol = [p["begcol"] for p in ops]
        decay = [jnp.where(causal, jnp.exp(jnp.where(causal, x - as_row(x), 0.0)), 0.0) for x in gcol]
        qkk = [lax.dot_general(p["qk"], p["kk"], (((1,), (1,)), ((), ())), preferred_element_type=F32)
               for p in ops]
        for j, x, d in zip(js, qkk, decay):
            a_s[j] = (x[:c] * d).astype(BF16)
        tm = _inv_unit_lower([jnp.where(strict, x[c:] * d * b, 0.0) for x, d, b in zip(qkk, decay, bcl)])
        w = [jnp.dot(x * as_row(gg).astype(BF16), p["kbd"], preferred_element_type=F32)
             for p, x, gg in zip(ops, tm, begcol)]
        u = [jnp.dot(x * as_row(b).astype(BF16), p["vbd"], preferred_element_type=F32)
             for p, x, b in zip(ops, tm, bcl)]
        for j, wn, un in zip(js, w, u):
            u_s[j] = un
            wq_s[j, 0:c, :] = wn.astype(BF16)

    for i in range(heads * n // unroll):
        js = list(range(i * unroll, (i + 1) * unroll))
        intra(js, [prep(job) for job in js])

    zero_c = jnp.zeros((c, DV), BF16)
    zero_s = jnp.zeros((DK, DV), BF16)

    def blockdiag(x, zero):
        return jnp.concatenate([jnp.concatenate([x[:, :DV], zero], axis=1),
                                jnp.concatenate([zero, x[:, DV:]], axis=1)], axis=0)

    def scan(nn):
        js = [nn * heads + g for g in range(heads)]
        sp = [st_s[g] for g in range(heads)]
        r1 = [jnp.dot(wq_s[j], blockdiag(x.astype(BF16), zero_s), preferred_element_type=F32)
              for j, x in zip(js, sp)]
        vnbd = [blockdiag((u_s[j] - x[0:c]).astype(BF16), zero_c) for j, x in zip(js, r1)]
        av = [jnp.dot(a_s[j], v, preferred_element_type=F32) for j, v in zip(js, vnbd)]
        upd = [lax.dot_general(kd_s[j], v, (((0,), (0,)), ((), ())), preferred_element_type=F32)
               for j, v in zip(js, vnbd)]
        rows = slice(nn * c, (nn + 1) * c)
        for g, j in enumerate(js):
            st_s[g] = sp[g] * gl_s[j] + upd[g]
            for h in range(2):
                hv = 2 * g + h
                o = r1[g][c:2 * c, h * DV:(h + 1) * DV] + av[g][:, h * DV:(h + 1) * DV]
                o = (o * lax.rsqrt(jnp.mean(o * o, axis=-1, keepdims=True) + EPS) * nw_ref[...]
                     * _silu(z_ref[hv, rows, :]))
                o_ref[rows, hv * DV:(hv + 1) * DV] = o.astype(o_ref.dtype)

    for nn in range(n):
        scan(nn)

    for g in range(heads):
        for h in range(2):
            s_ref[0, 2 * g + h] = st_s[g, :, h * DV:(h + 1) * DV]


def _gate_selectors(nk):
    kh = jnp.arange(nk, dtype=jnp.int32)[:, None, None]
    row = jnp.arange(2 * LANES, dtype=jnp.int32)[None, :, None] & (LANES - 1)
    col = jnp.arange(4 * LANES, dtype=jnp.int32)[None, None, :]
    blk, second = col // LANES, (col % LANES) >= LANES // 2
    head = jnp.where(blk < 2, second.astype(jnp.int32), blk - 2)
    slot = jnp.where(blk == 1, 0, 1)
    return (row == GATE_SLOTS * (2 * kh + head) + slot).astype(BF16)


def _gdn_core(ph, hist, cw, gates, s0, nw, s_prev, *, layer, b, t, nk, nv, tt, heads, unroll):
    c = CHUNK
    assert t % tt == 0 and tt % c == 0 and nk % heads == 0 and nv == 2 * nk and DK == LANES and DV == LANES
    assert (heads * tt // c) % unroll == 0
    m = b * t
    nt = t // tt
    jobs = heads * tt // c
    vh = 2 * heads
    off_k, off_v, off_z = nk // heads, 2 * nk // vh, (2 * nk + nv) // vh
    seq = lambda off: (lambda bi, h, ti: (off + h, bi * nt + ti, 0))
    hst = lambda off: (lambda bi, h, ti: (off + h, bi, 0, 0))
    cwt = lambda off: (lambda bi, h, ti: (off + h, 0, 0))
    f32 = lambda *shape: pltpu.VMEM(shape, F32)
    bf16 = lambda *shape: pltpu.VMEM(shape, BF16)
    state = pl.BlockSpec((None, 1, vh, DK, DV), lambda bi, h, ti: (layer, bi, h, 0, 0))
    carried = () if s_prev is None else (s_prev,)
    inputs = (ph, ph, ph, ph, hist, hist, hist, cw, cw, cw, gates, _gate_selectors(nk), s0, nw) + carried
    return pl.pallas_call(
        functools.partial(_gdn_kernel, tt=tt, c=c, heads=heads, unroll=unroll),
        grid=(b, nk // heads, nt),
        in_specs=[
            pl.BlockSpec((heads, tt, LANES), seq(0)), pl.BlockSpec((heads, tt, LANES), seq(off_k)),
            pl.BlockSpec((vh, tt, LANES), seq(off_v)), pl.BlockSpec((vh, tt, LANES), seq(off_z)),
            pl.BlockSpec((heads, 1, HIST_ROWS, LANES), hst(0)), pl.BlockSpec((heads, 1, HIST_ROWS, LANES), hst(off_k)),
            pl.BlockSpec((vh, 1, HIST_ROWS, LANES), hst(off_v)),
            pl.BlockSpec((heads, HIST_ROWS, LANES), cwt(0)), pl.BlockSpec((heads, HIST_ROWS, LANES), cwt(off_k)),
            pl.BlockSpec((vh, HIST_ROWS, LANES), cwt(off_v)),
            pl.BlockSpec((tt, LANES), lambda bi, h, ti: (bi * nt + ti, 0)),
            pl.BlockSpec((heads, 2 * LANES, 4 * LANES), lambda bi, h, ti: (h, 0, 0)),
            state,
            pl.BlockSpec((1, DV), lambda bi, h, ti: (0, 0)),
        ] + [pl.BlockSpec(memory_space=pl.ANY) for _ in carried],
        out_specs=[pl.BlockSpec((tt, vh * DV), lambda bi, h, ti: (bi * nt + ti, h)), state],
        out_shape=[
            jax.ShapeDtypeStruct((m, nv * DV), BF16),
            jax.ShapeDtypeStruct(s0.shape, F32),
        ],
        input_output_aliases={len(inputs) - 1: 1} if carried else {},
        scratch_shapes=[
            f32(heads, 4, tt + HIST_ROWS, LANES), f32(heads, 4, HIST_ROWS, LANES), f32(jobs, 1, 2 * LANES),
            bf16(jobs, 2 * c, LANES), bf16(jobs, 2 * c, 2 * LANES),
            f32(jobs, c, 2 * DV), bf16(jobs, c, LANES), f32(heads, DK, 2 * DV),
        ],
        compiler_params=_cparams(("parallel", "parallel", "arbitrary")),
        name="gdn_core",
    )(*inputs)


def _mm_res_kernel(a_ref, w_ref, r_ref, o_ref):
    o_ref[...] = r_ref[...] + jnp.dot(a_ref[...], w_ref[...], preferred_element_type=F32)


def _mm_residual(a, w, res, *, layer, tm, tn):
    m, k = a.shape
    n = w.shape[2]
    return pl.pallas_call(
        _mm_res_kernel,
        grid=(m // tm, n // tn),
        in_specs=[
            pl.BlockSpec((tm, k), lambda i, j: (i, 0)),
            pl.BlockSpec((None, k, tn), lambda i, j: (layer, 0, j)),
            pl.BlockSpec((tm, tn), lambda i, j: (i, j)),
        ],
        out_specs=pl.BlockSpec((tm, tn), lambda i, j: (i, j)),
        out_shape=jax.ShapeDtypeStruct((m, n), F32),
        compiler_params=_cparams(("parallel", "arbitrary")),
        name="mm_residual",
    )(a, w, res)


def _ffn_kernel(x_ref, nw_ref, wg_ref, wu_ref, wd_ref, fw_ref, o_ref, h_ref, *, final):
    j = pl.program_id(1)

    @pl.when(j == 0)
    def _():
        x = x_ref[...]
        h_ref[...] = _rms(x, nw_ref[...]).astype(BF16)
        o_ref[...] = x

    h = h_ref[...]
    g = jnp.dot(h, wg_ref[...], preferred_element_type=F32)
    u = jnp.dot(h, wu_ref[...], preferred_element_type=F32)
    act = (_silu(g) * u).astype(BF16)
    o_ref[...] += jnp.dot(act, wd_ref[...], preferred_element_type=F32)

    if final:
        @pl.when(j == pl.num_programs(1) - 1)
        def _():
            o_ref[...] = _rms(o_ref[...], fw_ref[...])


def _ffn(x, nw, wgu, wd, fw, *, layer, tm, tf, final):
    m, d = x.shape
    f = wd.shape[1]
    nf = f // tf
    return pl.pallas_call(
        functools.partial(_ffn_kernel, final=final),
        grid=(m // tm, nf),
        in_specs=[
            pl.BlockSpec((tm, d), lambda i, j: (i, 0)),
            pl.BlockSpec((1, d), lambda i, j: (0, 0)),
            pl.BlockSpec((None, d, tf), lambda i, j: (layer, 0, j)),
            pl.BlockSpec((None, d, tf), lambda i, j: (layer, 0, nf + j)),
            pl.BlockSpec((None, tf, d), lambda i, j: (layer, j, 0)),
            pl.BlockSpec((1, d), lambda i, j: (0, 0)),
        ],
        out_specs=pl.BlockSpec((tm, d), lambda i, j: (i, 0)),
        out_shape=jax.ShapeDtypeStruct((m, d), F32),
        scratch_shapes=[pltpu.VMEM((tm, d), BF16)],
        compiler_params=_cparams(("parallel", "arbitrary")),
        name="ffn",
    )(x, nw, wgu, wgu, wd, fw)


def _pool_kernel(x_ref, nw_ref, hist_ref, pw_ref, sc_ref, o_ref, hs_ref, hbuf, *lvl, tt, past):
    ti = pl.program_id(1)
    pad = 2 * POOL_PAD
    assert all(w == 2 << g for g, w in enumerate(POOL_WINDOWS)) and POOL_WINDOWS[-1] // 2 <= 8

    @pl.when(ti == 0)
    def _():
        hbuf[0:POOL_PAD, :] = jnp.zeros((POOL_PAD, hbuf.shape[1]), F32)
        hbuf[POOL_PAD:pad, :] = hist_ref[0]

    @pl.when(ti > 0)
    def _():
        hbuf[0:pad, :] = hbuf[tt:tt + pad, :]

    x = x_ref[...]
    h = _rms(x, nw_ref[...])
    hbuf[pad:pad + tt, :] = h
    tg = ti * tt + lax.broadcasted_iota(jnp.int32, (tt, 1), 0)
    d = x.shape[1]
    gcw = d // len(POOL_WINDOWS)
    prev, prev_col0 = hbuf, 0
    for gi, wlen in enumerate(POOL_WINDOWS):
        start, shift, col0 = 8 * (gi + 1), wlen // 2, gi * gcw
        rel = slice(col0 - prev_col0, d - prev_col0)
        cur = prev[start:pad + tt, rel] + prev[start - shift:pad + tt - shift, rel]
        if gi + 1 < len(POOL_WINDOWS):
            lvl[gi][start:pad + tt, :] = cur
            prev, prev_col0 = lvl[gi], col0
        cols = slice(col0, col0 + gcw)
        cnt = jnp.minimum(wlen, past + tg + 1).astype(F32)
        dlt = cur[pad - start:, 0:gcw] / cnt - h[:, cols]
        y = jnp.dot(dlt.astype(BF16), pw_ref[gi], preferred_element_type=F32)
        o_ref[:, cols] = x[:, cols] + y * sc_ref[:, cols]
    hs_ref[0] = hbuf[tt + POOL_PAD:tt + pad, :]


def _pool_mixer(x, nw, hist, pw, sc, *, b, t, tt, past):
    m, d = x.shape
    nt = t // tt
    g, gcw, _ = pw.shape
    return pl.pallas_call(
        functools.partial(_pool_kernel, tt=tt, past=past),
        grid=(b, nt),
        in_specs=[
            pl.BlockSpec((tt, d), lambda bi, ti: (bi * nt + ti, 0)),
            pl.BlockSpec((1, d), lambda bi, ti: (0, 0)),
            pl.BlockSpec((1, POOL_PAD, d), lambda bi, ti: (bi, 0, 0)),
            pl.BlockSpec((g, gcw, gcw), lambda bi, ti: (0, 0, 0)),
            pl.BlockSpec((1, d), lambda bi, ti: (0, 0)),
        ],
        out_specs=[
            pl.BlockSpec((tt, d), lambda bi, ti: (bi * nt + ti, 0)),
            pl.BlockSpec((1, POOL_PAD, d), lambda bi, ti: (bi, 0, 0)),
        ],
        out_shape=[
            jax.ShapeDtypeStruct((m, d), F32),
            jax.ShapeDtypeStruct((b, POOL_PAD, d), F32),
        ],
        scratch_shapes=[pltpu.VMEM((tt + 2 * POOL_PAD, d - gi * gcw), F32) for gi in (0, *range(g - 1))],
        compiler_params=_cparams(("parallel", "arbitrary")),
        name="pool_mixer",
    )(x, nw, hist, pw, sc)


def _prep_weights(norm_mix_w, norm_ffn_w, final_norm_w, gdn_w_in, gdn_conv_w, gdn_A_log, gdn_dt_bias, gdn_norm_w,
                  gdn_w_out, pool_w, pool_scale, ffn_w_gu, ffn_w_down):
    n_a, d, _ = gdn_w_in.shape
    nv = gdn_A_log.shape[1]
    val_dim = nv * DV
    qkv_dim = gdn_conv_w.shape[2]
    main = qkv_dim + val_dim
    zeros = jnp.zeros((n_a, d, nv), F32)
    w_gate = jnp.stack([gdn_w_in[:, :, main:main + nv], gdn_w_in[:, :, main + nv:main + 2 * nv], zeros, zeros],
                       axis=-1).reshape(n_a, d, nv * GATE_SLOTS).astype(BF16)
    zv = jnp.zeros((n_a, nv), F32)
    alog = jnp.stack([zv, gdn_A_log.astype(F32), zv, zv], axis=-1).reshape(n_a, 1, nv * GATE_SLOTS)
    dtb = jnp.stack([zv, gdn_dt_bias.astype(F32), zv, zv], axis=-1).reshape(n_a, 1, nv * GATE_SLOTS)
    nh = qkv_dim // LANES
    cw = gdn_conv_w.reshape(n_a, CONV_W, nh, LANES).transpose(0, 2, 1, 3)
    cw = jnp.pad(cw, ((0, 0), (0, 0), (0, HIST_ROWS - CONV_W), (0, 0)))
    return dict(
        norm_mix=norm_mix_w[:, None, :], norm_ffn=norm_ffn_w[:, None, :], final=final_norm_w[None, :],
        w_in=gdn_w_in.astype(BF16), n_main=main, w_gate=w_gate, alog=alog, dtb=dtb, cw=cw,
        gnorm=gdn_norm_w[:, None, :], w_out=gdn_w_out.astype(BF16), pool_w=pool_w.astype(BF16),
        pool_scale=pool_scale[:, None, :], w_gu=ffn_w_gu.astype(BF16), w_down=ffn_w_down.astype(BF16),
    )


def _gdn_layer(x, conv_hist, s_hist, s_prev, wts, i, j, *, b, t):
    m, d = x.shape
    nv = s_hist.shape[2]
    nk = nv // 2
    qkv_dim = conv_hist.shape[-1]
    nh = qkv_dim // LANES
    if t % GDN_TILE == 0:
        tp, tt, heads = t, GDN_TILE, GDN_HEADS
        xp = x
    else:
        tp = -(-t // CHUNK) * CHUNK
        tt, heads = tp, nk
        xp = jnp.pad(x.reshape(b, t, d), ((0, 0), (0, tp - t), (0, 0))).reshape(b * tp, d)
    mp = b * tp
    ph, ba = _norm_proj(xp, wts["norm_mix"][i], wts["w_in"], wts["w_gate"][j], layer=j, n=wts["n_main"],
                        tm=min(ROW_TILE, mp), tn=PROJ_COLS)
    gates = _gates(ba, wts["alog"][j], wts["dtb"][j], c=CHUNK, tt=min(mp, GATE_ROWS), valid=t, period=tp)
    hist = conv_hist.reshape(b, CONV_W - 1, nh, LANES).transpose(2, 0, 1, 3)
    hist = jnp.pad(hist, ((0, 0), (0, 0), (HIST_ROWS - (CONV_W - 1), 0), (0, 0)))
    o, s_new = _gdn_core(ph, hist, wts["cw"][j], gates, s_hist.astype(F32), wts["gnorm"][j], s_prev, layer=j, b=b,
                         t=tp, nk=nk, nv=nv, tt=tt, heads=heads, unroll=min(GDN_UNROLL, heads * tt // CHUNK))
    if tp != t:
        o = o.reshape(b, tp, o.shape[-1])[:, :t].reshape(m, o.shape[-1])
    x = _mm_residual(o, wts["w_out"], x, layer=j, tm=min(ROW_TILE, m), tn=OUT_COLS)
    tail = ph.reshape(ph.shape[0], b, tp, LANES)[:nh, :, t - (CONV_W - 1):t, :]
    return x, tail.transpose(1, 2, 0, 3).reshape(b, CONV_W - 1, qkv_dim), s_new


def _trunk(x3, conv_hist, s_hist, pool_hist, wts, depth):
    b, t, d = x3.shape
    m = b * t
    x = x3.reshape(m, d)
    past = pool_hist.shape[2]
    assert t >= CONV_W - 1 and t >= POOL_PAD - 1
    new_conv, new_s, new_pool = [], None, []
    for i in range(depth):
        j = i // 2
        if i % 2 == 0:
            x, conv_new, new_s = _gdn_layer(x, conv_hist[j], s_hist, new_s, wts, i, j, b=b, t=t)
            new_conv.append(conv_new)
        else:
            hist = jnp.pad(pool_hist[j], ((0, 0), (POOL_PAD - past, 0), (0, 0)))
            x, hs = _pool_mixer(x, wts["norm_mix"][i], hist, wts["pool_w"][j], wts["pool_scale"][j],
                                b=b, t=t, tt=min(t, POOL_ROWS), past=past)
            new_pool.append(hs[:, 1:, :])
        x = _ffn(x, wts["norm_ffn"][i], wts["w_gu"], wts["w_down"], wts["final"], layer=i, tm=min(ROW_TILE, m),
                 tf=FFN_COLS, final=(i == depth - 1))
    return x.reshape(b, t, d), jnp.stack(new_conv), new_s, jnp.stack(new_pool)


def kernel(x_prompt, x_sample, state_gdn_conv, state_gdn_S, state_pool, norm_mix_w, norm_ffn_w, final_norm_w,
           gdn_w_in, gdn_conv_w, gdn_A_log, gdn_dt_bias, gdn_norm_w, gdn_w_out, pool_w, pool_scale, ffn_w_gu,
           ffn_w_down):
    depth = norm_mix_w.shape[0]
    n_a, n_b = state_gdn_conv.shape[0], state_pool.shape[0]
    bp = x_prompt.shape[0]
    wts = _prep_weights(norm_mix_w, norm_ffn_w, final_norm_w, gdn_w_in, gdn_conv_w, gdn_A_log, gdn_dt_bias,
                        gdn_norm_w, gdn_w_out, pool_w, pool_scale, ffn_w_gu, ffn_w_down)
    conv0 = jnp.zeros((n_a, bp) + state_gdn_conv.shape[2:], x_prompt.dtype)
    s00 = jnp.zeros((n_a, bp) + state_gdn_S.shape[2:], F32)
    pool0 = jnp.zeros((n_b, bp, 0, x_prompt.shape[-1]), x_prompt.dtype)
    y_p, conv_p, s_p, pool_p = _trunk(x_prompt, conv0, s00, pool0, wts, depth)
    y_s, conv_s, s_s, pool_s = _trunk(x_sample, state_gdn_conv, state_gdn_S, state_pool, wts, depth)
    return (y_p, y_s, conv_p, s_p.astype(state_gdn_S.dtype), pool_p, conv_s, s_s.astype(state_gdn_S.dtype), pool_s)
```

```python
import functools

import jax
import jax.numpy as jnp
from jax import lax
from jax.experimental import pallas as pl
from jax.experimental.pallas import tpu as pltpu

F32 = jnp.float32
BF16 = jnp.bfloat16

EPS = 1e-6
LANES = 128
DK = 128
DV = 128
CONV_W = 4
HIST_ROWS = 8
CHUNK = 64
INV_BLOCK = 8
GDN_TILE = 128
GDN_HEADS = 16
GDN_UNROLL = 32
POOL_WINDOWS = (2, 4, 8, 16)
POOL_PAD = 16
GATE_SLOTS = 4
V7X_VMEM_BYTES = 64 * 1024 * 1024
VMEM_LIMIT = V7X_VMEM_BYTES - 8 * 1024 * 1024
ROW_TILE = 1024
PROJ_COLS = 1536
OUT_COLS = 1024
FFN_COLS = 512
POOL_ROWS = 512
GATE_ROWS = 2048


def _cparams(sem):
    return pltpu.CompilerParams(dimension_semantics=sem, vmem_limit_bytes=VMEM_LIMIT)


def _rms(x, w):
    ms = jnp.mean(x * x, axis=-1, keepdims=True)
    return x * lax.rsqrt(ms + EPS) * w


def _silu(x):
    h = 0.5 * x
    return h * jnp.tanh(h) + h


def _norm_proj_kernel(x_ref, nw_ref, w_ref, wg_ref, o_ref, g_ref, h_ref, *, tn):
    @pl.when(pl.program_id(1) == 0)
    def _():
        h = _rms(x_ref[...], nw_ref[...]).astype(BF16)
        h_ref[...] = h
        g_ref[...] = jnp.dot(h, wg_ref[...], preferred_element_type=F32)

    acc = jnp.dot(h_ref[...], w_ref[...], preferred_element_type=F32)
    for c in range(tn // LANES):
        o_ref[c] = acc[:, c * LANES:(c + 1) * LANES]


def _norm_proj(x, nw, w, wg, *, layer, n, tm, tn):
    m, d = x.shape
    assert n % tn == 0 and n <= w.shape[2] and m % tm == 0
    return pl.pallas_call(
        functools.partial(_norm_proj_kernel, tn=tn),
        grid=(m // tm, n // tn),
        in_specs=[
            pl.BlockSpec((tm, d), lambda i, j: (i, 0)),
            pl.BlockSpec((1, d), lambda i, j: (0, 0)),
            pl.BlockSpec((None, d, tn), lambda i, j: (layer, 0, j)),
            pl.BlockSpec((d, LANES), lambda i, j: (0, 0)),
        ],
        out_specs=[
            pl.BlockSpec((tn // LANES, tm, LANES), lambda i, j: (j, i, 0)),
            pl.BlockSpec((tm, LANES), lambda i, j: (i, 0)),
        ],
        out_shape=[
            jax.ShapeDtypeStruct((n // LANES, m, LANES), F32),
            jax.ShapeDtypeStruct((m, LANES), F32),
        ],
        scratch_shapes=[pltpu.VMEM((tm, d), BF16)],
        compiler_params=_cparams(("parallel", "arbitrary")),
        name="norm_proj",
    )(x, nw, w, wg)


def _gates_kernel(ba_ref, alog_ref, dtb_ref, tri_ref, o_ref, *, c, valid, period):
    x = ba_ref[...]
    beta = jax.nn.sigmoid(x)
    y = x + dtb_ref[...]
    softplus = jnp.maximum(y, 0.0) + jnp.log1p(jnp.exp(-jnp.abs(y)))
    g = -jnp.exp(alog_ref[...]) * softplus
    if valid != period:
        assert period & (period - 1) == 0 and x.shape[0] % period == 0
        row = lax.broadcasted_iota(jnp.int32, x.shape, 0) & (period - 1)
        beta = jnp.where(row < valid, beta, 0.0)
        g = jnp.where(row < valid, g, 0.0)
    lane = lax.broadcasted_iota(jnp.int32, (c, x.shape[1]), 1)
    is_beta = (lane & (GATE_SLOTS - 1)) == 0
    tri = tri_ref[...]
    for n in range(x.shape[0] // c):
        sl = slice(n * c, (n + 1) * c)
        gc = jnp.dot(tri, g[sl], preferred_element_type=F32, precision=lax.Precision.HIGHEST)
        o_ref[sl, :] = jnp.where(is_beta, beta[sl], gc)


def _gates(ba, alog, dtb, *, c, tt, valid, period):
    m = ba.shape[0]
    tri = jnp.tril(jnp.ones((c, c), F32))
    return pl.pallas_call(
        functools.partial(_gates_kernel, c=c, valid=valid, period=period),
        grid=(m // tt,),
        in_specs=[
            pl.BlockSpec((tt, LANES), lambda i: (i, 0)),
            pl.BlockSpec((1, LANES), lambda i: (0, 0)),
            pl.BlockSpec((1, LANES), lambda i: (0, 0)),
            pl.BlockSpec((c, c), lambda i: (0, 0)),
        ],
        out_specs=pl.BlockSpec((tt, LANES), lambda i: (i, 0)),
        out_shape=jax.ShapeDtypeStruct((m, LANES), F32),
        compiler_params=_cparams(("parallel",)),
        name="gates",
    )(ba, alog, dtb, tri)


def _inv_unit_lower(l_mats):
    c, w = l_mats[0].shape
    npack = w // c
    r = lax.broadcasted_iota(jnp.int32, (c, w), 0)
    lane = lax.broadcasted_iota(jnp.int32, (c, w), 1)
    s = lane & (c - 1)

    def same_block(size):
        shift = size.bit_length() - 1
        return lax.shift_right_logical(r, shift) == lax.shift_right_logical(s, shift)

    same = same_block(INV_BLOCK)
    eye = jnp.where(r == s, 1.0, 0.0).astype(BF16)
    zero = jnp.zeros((c, w), BF16)
    part = [lax.shift_right_logical(lane, c.bit_length() - 1) == p for p in range(npack)]

    def mm(xs, ys):
        if npack > 1:
            ys = [jnp.concatenate([jnp.where(part[p], y, zero) for p in range(npack)], axis=0) for y in ys]
        return [jnp.dot(x, y, preferred_element_type=F32).astype(BF16) for x, y in zip(xs, ys)]

    def stack(xs, ys):
        return [jnp.concatenate([x, y], axis=0) for x, y in zip(xs, ys)]

    assert INV_BLOCK == 8
    l16 = [l.astype(BF16) for l in l_mats]
    d1 = [jnp.where(same, l, zero) for l in l16]
    d2 = mm(d1, d1)
    t = mm(stack(d2, d1), d2)
    y = [eye - a + b - x[c:] for a, b, x in zip(d1, d2, t)]
    x = [a + b for a, b in zip(y, mm(y, [v[:c] for v in t]))]
    size = INV_BLOCK
    while size < c:
        pair, inner = same_block(2 * size), same_block(size)
        e = [jnp.where(pair & ~inner, l, zero) for l in l16]
        x = [a - b for a, b in zip(x, mm(mm(x, e), x))]
        size *= 2
    return x


def _gdn_kernel(q_ref, k_ref, v_ref, z_ref, hq_ref, hk_ref, hv_ref, cq_ref, ck_ref, cv_ref, g_ref, e_ref, s0_ref,
                nw_ref, *rest, tt, c, heads, unroll):
    o_ref, s_ref, xc_s, hist_s, gl_s, kd_s, wq_s, u_s, a_s, st_s = rest[-10:]
    n = tt // c
    assert 2 * c == LANES

    @pl.when(pl.program_id(2) == 0)
    def _():
        for g in range(heads):
            hist_s[g, 0] = hq_ref[g, 0]
            hist_s[g, 1] = hk_ref[g, 0]
            for h in range(2):
                hist_s[g, 2 + h] = hv_ref[2 * g + h, 0]
                st_s[g, :, h * DV:(h + 1) * DV] = s0_ref[0, 2 * g + h]

    def l2n(x):
        return x * lax.rsqrt(jnp.sum(x * x, axis=-1, keepdims=True) + EPS)

    zero_k = jnp.zeros((c, LANES), BF16)
    base = HIST_ROWS - (CONV_W - 1)
    ghl = []
    for nn in range(n):
        graw = g_ref[nn * c:(nn + 1) * c, :]
        hi = graw.astype(BF16)
        ghl.append(jnp.concatenate([hi, (graw - hi.astype(F32)).astype(BF16)], axis=1))

    for g in range(heads):
        xin = (q_ref[g], k_ref[g], v_ref[2 * g], v_ref[2 * g + 1])
        for slot in range(4):
            xc_s[g, slot, 0:HIST_ROWS, :] = hist_s[g, slot]
            xc_s[g, slot, HIST_ROWS:HIST_ROWS + tt, :] = xin[slot]
            hist_s[g, slot] = xc_s[g, slot, tt:tt + HIST_ROWS, :]

    ex_all = [jnp.dot(ghl[nn], e_ref[g], preferred_element_type=F32) for nn in range(n) for g in range(heads)]

    def prep(job):
        nn, g = divmod(job, heads)
        r0 = nn * c
        cws =(cq_ref[g], ck_ref[g], cv_ref[2 * g], cv_ref[2 * g + 1])

        def conv_silu(slot):
            acc = xc_s[g, slot, r0 + base:r0 + base + c, :] * cws[slot][0:1, :]
            for j in range(1, CONV_W):
                acc = acc + xc_s[g, slot, r0 + base + j:r0 + base + j + c, :] * cws[slot][j:j + 1, :]
            return _silu(acc)

        ex = ex_all[job]
        gcol, bcol = ex[:, 0:LANES], ex[:, LANES:2 * LANES]
        gc = [ex[:, (2 + h) * LANES:(3 + h) * LANES] for h in range(2)]
        eg = [jnp.exp(x) for x in gc]
        gl = [x[c - 1:c, :] for x in gc]
        for h in range(2):
            gl_s[job, :, h * LANES:(h + 1) * LANES] = jnp.exp(gl[h])

        q = l2n(conv_silu(0)) * (DK ** -0.5)
        k = l2n(conv_silu(1))
        k16 = k.astype(BF16)
        v16 = [conv_silu(2 + h).astype(BF16) for h in range(2)]
        for h in range(2):
            wq_s[job, c:2 * c, h * DK:(h + 1) * DK] = (q * eg[h]).astype(BF16)
            kd_s[job, h * c:(h + 1) * c, :] = (k * jnp.exp(gl[h] - gc[h])).astype(BF16)
        return dict(gcol=gcol, bcol=bcol, begcol=bcol * jnp.exp(gcol),
                    qk=jnp.concatenate([q.astype(BF16), k16], axis=0), kk=jnp.concatenate([k16, k16], axis=0),
                    kbd=jnp.concatenate([jnp.concatenate([k16, zero_k], axis=1),
                                         jnp.concatenate([zero_k, k16], axis=1)], axis=0),
                    vbd=jnp.concatenate([jnp.concatenate([v16[0], zero_k], axis=1),
                                         jnp.concatenate([zero_k, v16[1]], axis=1)], axis=0))

    r = lax.broadcasted_iota(jnp.int32, (c, LANES), 0)
    sloc = lax.broadcasted_iota(jnp.int32, (c, LANES), 1) & (c - 1)
    causal = r >= sloc
    strict = r > sloc
    diag = r == sloc

    def as_row(x):
        return jnp.sum(jnp.where(diag, x, 0.0), axis=0, keepdims=True)

    def intra(js, ops):
        gcol = [p["gcol"] for p in ops]
        bcl = [p["bcol"] for p in ops]
        begcol = [p["begcol"] for p in ops]
        decay = [jnp.where(causal, jnp.exp(jnp.where(causal, x - as_row(x), 0.0)), 0.0) for x in gcol]
        qkk = [lax.dot_general(p["qk"], p["kk"], (((1,), (1,)), ((), ())), preferred_element_type=F32)
               for p in ops]
        for j, x, d in zip(js, qkk, decay):
            a_s[j] = (x[:c] * d).astype(BF16)
        tm = _inv_unit_lower([jnp.where(strict, x[c:] * d * b, 0.0) for x, d, b in zip(qkk, decay, bcl)])
        w = [jnp.dot(x * as_row(gg).astype(BF16), p["kbd"], preferred_element_type=F32)
             for p, x, gg in zip(ops, tm, begcol)]
        u = [jnp.dot(x * as_row(b).astype(BF16), p["vbd"], preferred_element_type=F32)
             for p, x, b in zip(ops, tm, bcl)]
        for j, wn, un in zip(js, w, u):
            u_s[j] = un
            wq_s[j, 0:c, :] = wn.astype(BF16)

    for i in range(heads * n // unroll):
        js = list(range(i * unroll, (i + 1) * unroll))
        intra(js, [prep(job) for job in js])

    zero_c = jnp.zeros((c, DV), BF16)
    zero_s = jnp.zeros((DK, DV), BF16)

    def blockdiag(x, zero):
        return jnp.concatenate([jnp.concatenate([x[:, :DV], zero], axis=1),
                                jnp.concatenate([zero, x[:, DV:]], axis=1)], axis=0)

    def scan(nn):
        js = [nn * heads + g for g in range(heads)]
        sp = [st_s[g] for g in range(heads)]
        r1 = [jnp.dot(wq_s[j], blockdiag(x.astype(BF16), zero_s), preferred_element_type=F32)
              for j, x in zip(js, sp)]
        vnbd = [blockdiag((u_s[j] - x[0:c]).astype(BF16), zero_c) for j, x in zip(js, r1)]
        av = [jnp.dot(a_s[j], v, preferred_element_type=F32) for j, v in zip(js, vnbd)]
        upd = [lax.dot_general(kd_s[j], v, (((0,), (0,)), ((), ())), preferred_element_type=F32)
               for j, v in zip(js, vnbd)]
        rows = slice(nn * c, (nn + 1) * c)
        for g, j in enumerate(js):
            st_s[g] = sp[g] * gl_s[j] + upd[g]
            for h in range(2):
                hv = 2 * g + h
                o = r1[g][c:2 * c, h * DV:(h + 1) * DV] + av[g][:, h * DV:(h + 1) * DV]
                o = (o * lax.rsqrt(jnp.mean(o * o, axis=-1, keepdims=True) + EPS) * nw_ref[...]
                     * _silu(z_ref[hv, rows, :]))
                o_ref[rows, hv * DV:(hv + 1) * DV] = o.astype(o_ref.dtype)

    for nn in range(n):
        scan(nn)

    for g in range(heads):
        for h in range(2):
            s_ref[0, 2 * g + h] = st_s[g, :, h * DV:(h + 1) * DV]


def _gate_selectors(nk):
    kh = jnp.arange(nk, dtype=jnp.int32)[:, None, None]
    row = jnp.arange(2 * LANES, dtype=jnp.int32)[None, :, None] & (LANES - 1)
    col = jnp.arange(4 * LANES, dtype=jnp.int32)[None, None, :]
    blk, second = col // LANES, (col % LANES) >= LANES // 2
    head = jnp.where(blk < 2, second.astype(jnp.int32), blk - 2)
    slot = jnp.where(blk == 1, 0, 1)
    return (row == GATE_SLOTS * (2 * kh + head) + slot).astype(BF16)


def _gdn_core(ph, hist, cw, gates, s0, nw, s_prev, *, layer, b, t, nk, nv, tt, heads, unroll):
    c = CHUNK
    assert t % tt == 0 and tt % c == 0 and nk % heads == 0 and nv == 2 * nk and DK == LANES and DV == LANES
    assert (heads * tt // c) % unroll == 0
    m = b * t
    nt = t // tt
    jobs = heads * tt // c
    vh = 2 * heads
    off_k, off_v, off_z = nk // heads, 2 * nk // vh, (2 * nk + nv) // vh
    seq = lambda off: (lambda bi, h, ti: (off + h, bi * nt + ti, 0))
    hst = lambda off: (lambda bi, h, ti: (off + h, bi, 0, 0))
    cwt = lambda off: (lambda bi, h, ti: (off + h, 0, 0))
    f32 = lambda *shape: pltpu.VMEM(shape, F32)
    bf16 = lambda *shape: pltpu.VMEM(shape, BF16)
    state = pl.BlockSpec((None, 1, vh, DK, DV), lambda bi, h, ti: (layer, bi, h, 0, 0))
    carried = () if s_prev is None else (s_prev,)
    inputs = (ph, ph, ph, ph, hist, hist, hist, cw, cw, cw, gates, _gate_selectors(nk), s0, nw) + carried
    return pl.pallas_call(
        functools.partial(_gdn_kernel, tt=tt, c=c, heads=heads, unroll=unroll),
        grid=(b, nk // heads, nt),
        in_specs=[
            pl.BlockSpec((heads, tt, LANES), seq(0)), pl.BlockSpec((heads, tt, LANES), seq(off_k)),
            pl.BlockSpec((vh, tt, LANES), seq(off_v)), pl.BlockSpec((vh, tt, LANES), seq(off_z)),
            pl.BlockSpec((heads, 1, HIST_ROWS, LANES), hst(0)), pl.BlockSpec((heads, 1, HIST_ROWS, LANES), hst(off_k)),
            pl.BlockSpec((vh, 1, HIST_ROWS, LANES), hst(off_v)),
            pl.BlockSpec((heads, HIST_ROWS, LANES), cwt(0)), pl.BlockSpec((heads, HIST_ROWS, LANES), cwt(off_k)),
            pl.BlockSpec((vh, HIST_ROWS, LANES), cwt(off_v)),
            pl.BlockSpec((tt, LANES), lambda bi, h, ti: (bi * nt + ti, 0)),
            pl.BlockSpec((heads, 2 * LANES, 4 * LANES), lambda bi, h, ti: (h, 0, 0)),
            state,
            pl.BlockSpec((1, DV), lambda bi, h, ti: (0, 0)),
        ] + [pl.BlockSpec(memory_space=pl.ANY) for _ in carried],
        out_specs=[pl.BlockSpec((tt, vh * DV), lambda bi, h, ti: (bi * nt + ti, h)), state],
        out_shape=[
            jax.ShapeDtypeStruct((m, nv * DV), BF16),
            jax.ShapeDtypeStruct(s0.shape, F32),
        ],
        input_output_aliases={len(inputs) - 1: 1} if carried else {},
        scratch_shapes=[
            f32(heads, 4, tt + HIST_ROWS, LANES), f32(heads, 4, HIST_ROWS, LANES), f32(jobs, 1, 2 * LANES),
            bf16(jobs, 2 * c, LANES), bf16(jobs, 2 * c, 2 * LANES),
            f32(jobs, c, 2 * DV), bf16(jobs, c, LANES), f32(heads, DK, 2 * DV),
        ],
        compiler_params=_cparams(("parallel", "parallel", "arbitrary")),
        name="gdn_core",
    )(*inputs)


def _mm_res_kernel(a_ref, w_ref, r_ref, o_ref):
    o_ref[...] = r_ref[...] + jnp.dot(a_ref[...], w_ref[...], preferred_element_type=F32)


def _mm_residual(a, w, res, *, layer, tm, tn):
    m, k = a.shape
    n = w.shape[2]
    return pl.pallas_call(
        _mm_res_kernel,
        grid=(m // tm, n // tn),
        in_specs=[
            pl.BlockSpec((tm, k), lambda i, j: (i, 0)),
            pl.BlockSpec((None, k, tn), lambda i, j: (layer, 0, j)),
            pl.BlockSpec((tm, tn), lambda i, j: (i, j)),
        ],
        out_specs=pl.BlockSpec((tm, tn), lambda i, j: (i, j)),
        out_shape=jax.ShapeDtypeStruct((m, n), F32),
        compiler_params=_cparams(("parallel", "arbitrary")),
        name="mm_residual",
    )(a, w, res)


def _ffn_kernel(x_ref, nw_ref, wg_ref, wu_ref, wd_ref, fw_ref, o_ref, h_ref, *, final):
    j = pl.program_id(1)

    @pl.when(j == 0)
    def _():
        x = x_ref[...]
        h_ref[...] = _rms(x, nw_ref[...]).astype(BF16)
        o_ref[...] = x

    h = h_ref[...]
    g = jnp.dot(h, wg_ref[...], preferred_element_type=F32)
    u = jnp.dot(h, wu_ref[...], preferred_element_type=F32)
    act = (_silu(g) * u).astype(BF16)
    o_ref[...] += jnp.dot(act, wd_ref[...], preferred_element_type=F32)

    if final:
        @pl.when(j == pl.num_programs(1) - 1)
        def _():
            o_ref[...] = _rms(o_ref[...], fw_ref[...])


def _ffn(x, nw, wgu, wd, fw, *, layer, tm, tf, final):
    m, d = x.shape
    f = wd.shape[1]
    nf = f // tf
    return pl.pallas_call(
        functools.partial(_ffn_kernel, final=final),
        grid=(m // tm, nf),
        in_specs=[
            pl.BlockSpec((tm, d), lambda i, j: (i, 0)),
            pl.BlockSpec((1, d), lambda i, j: (0, 0)),
            pl.BlockSpec((None, d, tf), lambda i, j: (layer, 0, j)),
            pl.BlockSpec((None, d, tf), lambda i, j: (layer, 0, nf + j)),
            pl.BlockSpec((None, tf, d), lambda i, j: (layer, j, 0)),
            pl.BlockSpec((1, d), lambda i, j: (0, 0)),
        ],
        out_specs=pl.BlockSpec((tm, d), lambda i, j: (i, 0)),
        out_shape=jax.ShapeDtypeStruct((m, d), F32),
        scratch_shapes=[pltpu.VMEM((tm, d), BF16)],
        compiler_params=_cparams(("parallel", "arbitrary")),
        name="ffn",
    )(x, nw, wgu, wgu, wd, fw)


def _pool_kernel(x_ref, nw_ref, hist_ref, pw_ref, sc_ref, o_ref, hs_ref, hbuf, *lvl, tt, past):
    ti = pl.program_id(1)
    pad = 2 * POOL_PAD
    assert all(w == 2 << g for g, w in enumerate(POOL_WINDOWS)) and POOL_WINDOWS[-1] // 2 <= 8

    @pl.when(ti == 0)
    def _():
        hbuf[0:POOL_PAD, :] = jnp.zeros((POOL_PAD, hbuf.shape[1]), F32)
        hbuf[POOL_PAD:pad, :] = hist_ref[0]

    @pl.when(ti > 0)
    def _():
        hbuf[0:pad, :] = hbuf[tt:tt + pad, :]

    x = x_ref[...]
    h = _rms(x, nw_ref[...])
    hbuf[pad:pad + tt, :] = h
    tg = ti * tt + lax.broadcasted_iota(jnp.int32, (tt, 1), 0)
    d = x.shape[1]
    gcw = d // len(POOL_WINDOWS)
    prev, prev_col0 = hbuf, 0
    for gi, wlen in enumerate(POOL_WINDOWS):
        start, shift, col0 = 8 * (gi + 1), wlen // 2, gi * gcw
        rel = slice(col0 - prev_col0, d - prev_col0)
        cur = prev[start:pad + tt, rel] + prev[start - shift:pad + tt - shift, rel]
        if gi + 1 < len(POOL_WINDOWS):
            lvl[gi][start:pad + tt, :] = cur
            prev, prev_col0 = lvl[gi], col0
        cols = slice(col0, col0 + gcw)
        cnt = jnp.minimum(wlen, past + tg + 1).astype(F32)
        dlt = cur[pad - start:, 0:gcw] / cnt - h[:, cols]
        y = jnp.dot(dlt.astype(BF16), pw_ref[gi], preferred_element_type=F32)
        o_ref[:, cols] = x[:, cols] + y * sc_ref[:, cols]
    hs_ref[0] = hbuf[tt + POOL_PAD:tt + pad, :]


def _pool_mixer(x, nw, hist, pw, sc, *, b, t, tt, past):
    m, d = x.shape
    nt = t // tt
    g, gcw, _ = pw.shape
    return pl.pallas_call(
        functools.partial(_pool_kernel, tt=tt, past=past),
        grid=(b, nt),
        in_specs=[
            pl.BlockSpec((tt, d), lambda bi, ti: (bi * nt + ti, 0)),
            pl.BlockSpec((1, d), lambda bi, ti: (0, 0)),
            pl.BlockSpec((1, POOL_PAD, d), lambda bi, ti: (bi, 0, 0)),
            pl.BlockSpec((g, gcw, gcw), lambda bi, ti: (0, 0, 0)),
            pl.BlockSpec((1, d), lambda bi, ti: (0, 0)),
        ],
        out_specs=[
            pl.BlockSpec((tt, d), lambda bi, ti: (bi * nt + ti, 0)),
            pl.BlockSpec((1, POOL_PAD, d), lambda bi, ti: (bi, 0, 0)),
        ],
        out_shape=[
            jax.ShapeDtypeStruct((m, d), F32),
            jax.ShapeDtypeStruct((b, POOL_PAD, d), F32),
        ],
        scratch_shapes=[pltpu.VMEM((tt + 2 * POOL_PAD, d - gi * gcw), F32) for gi in (0, *range(g - 1))],
        compiler_params=_cparams(("parallel", "arbitrary")),
        name="pool_mixer",
    )(x, nw, hist, pw, sc)


def _prep_weights(norm_mix_w, norm_ffn_w, final_norm_w, gdn_w_in, gdn_conv_w, gdn_A_log, gdn_dt_bias, gdn_norm_w,
                  gdn_w_out, pool_w, pool_scale, ffn_w_gu, ffn_w_down):
    n_a, d, _ = gdn_w_in.shape
    nv = gdn_A_log.shape[1]
    val_dim = nv * DV
    qkv_dim = gdn_conv_w.shape[2]
    main = qkv_dim + val_dim
    zeros = jnp.zeros((n_a, d, nv), F32)
    w_gate = jnp.stack([gdn_w_in[:, :, main:main + nv], gdn_w_in[:, :, main + nv:main + 2 * nv], zeros, zeros],
                       axis=-1).reshape(n_a, d, nv * GATE_SLOTS).astype(BF16)
    zv = jnp.zeros((n_a, nv), F32)
    alog = jnp.stack([zv, gdn_A_log.astype(F32), zv, zv], axis=-1).reshape(n_a, 1, nv * GATE_SLOTS)
    dtb = jnp.stack([zv, gdn_dt_bias.astype(F32), zv, zv], axis=-1).reshape(n_a, 1, nv * GATE_SLOTS)
    nh = qkv_dim // LANES
    cw = gdn_conv_w.reshape(n_a, CONV_W, nh, LANES).transpose(0, 2, 1, 3)
    cw = jnp.pad(cw, ((0, 0), (0, 0), (0, HIST_ROWS - CONV_W), (0, 0)))
    return dict(
        norm_mix=norm_mix_w[:, None, :], norm_ffn=norm_ffn_w[:, None, :], final=final_norm_w[None, :],
        w_in=gdn_w_in.astype(BF16), n_main=main, w_gate=w_gate, alog=alog, dtb=dtb, cw=cw,
        gnorm=gdn_norm_w[:, None, :], w_out=gdn_w_out.astype(BF16), pool_w=pool_w.astype(BF16),
        pool_scale=pool_scale[:, None, :], w_gu=ffn_w_gu.astype(BF16), w_down=ffn_w_down.astype(BF16),
    )


def _gdn_layer(x, conv_hist, s_hist, s_prev, wts, i, j, *, b, t):
    m, d = x.shape
    nv = s_hist.shape[2]
    nk = nv // 2
    qkv_dim = conv_hist.shape[-1]
    nh = qkv_dim // LANES
    if t % GDN_TILE == 0:
        tp, tt, heads = t, GDN_TILE, GDN_HEADS
        xp = x
    else:
        tp = -(-t // CHUNK) * CHUNK
        tt, heads = tp, nk
        xp = jnp.pad(x.reshape(b, t, d), ((0, 0), (0, tp - t), (0, 0))).reshape(b * tp, d)
    mp = b * tp
    ph, ba = _norm_proj(xp, wts["norm_mix"][i], wts["w_in"], wts["w_gate"][j], layer=j, n=wts["n_main"],
                        tm=min(ROW_TILE, mp), tn=PROJ_COLS)
    gates = _gates(ba, wts["alog"][j], wts["dtb"][j], c=CHUNK, tt=min(mp, GATE_ROWS), valid=t, period=tp)
    hist = conv_hist.reshape(b, CONV_W - 1, nh, LANES).transpose(2, 0, 1, 3)
    hist = jnp.pad(hist, ((0, 0), (0, 0), (HIST_ROWS - (CONV_W - 1), 0), (0, 0)))
    o, s_new = _gdn_core(ph, hist, wts["cw"][j], gates, s_hist.astype(F32), wts["gnorm"][j], s_prev, layer=j, b=b,
                         t=tp, nk=nk, nv=nv, tt=tt, heads=heads, unroll=min(GDN_UNROLL, heads * tt // CHUNK))
    if tp != t:
        o = o.reshape(b, tp, o.shape[-1])[:, :t].reshape(m, o.shape[-1])
    x = _mm_residual(o, wts["w_out"], x, layer=j, tm=min(ROW_TILE, m), tn=OUT_COLS)
    tail = ph.reshape(ph.shape[0], b, tp, LANES)[:nh, :, t - (CONV_W - 1):t, :]
    return x, tail.transpose(1, 2, 0, 3).reshape(b, CONV_W - 1, qkv_dim), s_new


def _trunk(x3, conv_hist, s_hist, pool_hist, wts, depth):
    b, t, d = x3.shape
    m = b * t
    x = x3.reshape(m, d)
    past = pool_hist.shape[2]
    assert t >= CONV_W - 1 and t >= POOL_PAD - 1
    new_conv, new_s, new_pool = [], None, []
    for i in range(depth):
        j = i // 2
        if i % 2 == 0:
            x, conv_new, new_s = _gdn_layer(x, conv_hist[j], s_hist, new_s, wts, i, j, b=b, t=t)
            new_conv.append(conv_new)
        else:
            hist = jnp.pad(pool_hist[j], ((0, 0), (POOL_PAD - past, 0), (0, 0)))
            x, hs = _pool_mixer(x, wts["norm_mix"][i], hist, wts["pool_w"][j], wts["pool_scale"][j],
                                b=b, t=t, tt=min(t, POOL_ROWS), past=past)
            new_pool.append(hs[:, 1:, :])
        x = _ffn(x, wts["norm_ffn"][i], wts["w_gu"], wts["w_down"], wts["final"], layer=i, tm=min(ROW_TILE, m),
                 tf=FFN_COLS, final=(i == depth - 1))
    return x.reshape(b, t, d), jnp.stack(new_conv), new_s, jnp.stack(new_pool)


def kernel(x_prompt, x_sample, state_gdn_conv, state_gdn_S, state_pool, norm_mix_w, norm_ffn_w, final_norm_w,
           gdn_w_in, gdn_conv_w, gdn_A_log, gdn_dt_bias, gdn_norm_w, gdn_w_out, pool_w, pool_scale, ffn_w_gu,
           ffn_w_down):
    depth = norm_mix_w.shape[0]
    n_a, n_b = state_gdn_conv.shape[0], state_pool.shape[0]
    bp = x_prompt.shape[0]
    wts = _prep_weights(norm_mix_w, norm_ffn_w, final_norm_w, gdn_w_in, gdn_conv_w, gdn_A_log, gdn_dt_bias,
                        gdn_norm_w, gdn_w_out, pool_w, pool_scale, ffn_w_gu, ffn_w_down)
    conv0 = jnp.zeros((n_a, bp) + state_gdn_conv.shape[2:], x_prompt.dtype)
    s00 = jnp.zeros((n_a, bp) + state_gdn_S.shape[2:], F32)
    pool0 = jnp.zeros((n_b, bp, 0, x_prompt.shape[-1]), x_prompt.dtype)
    y_p, conv_p, s_p, pool_p = _trunk(x_prompt, conv0, s00, pool0, wts, depth)
    y_s, conv_s, s_s, pool_s = _trunk(x_sample, state_gdn_conv, state_gdn_S, state_pool, wts, depth)
    return (y_p, y_s, conv_p, s_p.astype(state_gdn_S.dtype), pool_p, conv_s, s_s.astype(state_gdn_S.dtype), pool_s)
```

```python
import functools

import jax
import jax.numpy as jnp
from jax import lax
from jax.experimental import pallas as pl
from jax.experimental.pallas import tpu as pltpu

F32 = jnp.float32
BF16 = jnp.bfloat16

EPS = 1e-6
LANES = 128
DK = 128
DV = 128
CONV_W = 4
HIST_ROWS = 8
CHUNK = 64
INV_BLOCK = 8
GDN_TILE = 128
GDN_HEADS = 16
GDN_UNROLL = 32
POOL_WINDOWS = (2, 4, 8, 16)
POOL_PAD = 16
GATE_SLOTS = 4
V7X_VMEM_BYTES = 64 * 1024 * 1024
VMEM_LIMIT = V7X_VMEM_BYTES - 8 * 1024 * 1024
ROW_TILE = 1024
PROJ_COLS = 1536
OUT_COLS = 1024
FFN_COLS = 512
POOL_ROWS = 512
GATE_ROWS = 2048


def _cparams(sem):
    return pltpu.CompilerParams(dimension_semantics=sem, vmem_limit_bytes=VMEM_LIMIT)


def _rms(x, w):
    ms = jnp.mean(x * x, axis=-1, keepdims=True)
    return x * lax.rsqrt(ms + EPS) * w


def _silu(x):
    h = 0.5 * x
    return h * jnp.tanh(h) + h


def _norm_proj_kernel(x_ref, nw_ref, w_ref, wg_ref, o_ref, g_ref, h_ref, *, tn):
    @pl.when(pl.program_id(1) == 0)
    def _():
        h = _rms(x_ref[...], nw_ref[...]).astype(BF16)
        h_ref[...] = h
        g_ref[...] = jnp.dot(h, wg_ref[...], preferred_element_type=F32)

    acc = jnp.dot(h_ref[...], w_ref[...], preferred_element_type=F32)
    for c in range(tn // LANES):
        o_ref[c] = acc[:, c * LANES:(c + 1) * LANES]


def _norm_proj(x, nw, w, wg, *, layer, n, tm, tn):
    m, d = x.shape
    assert n % tn == 0 and n <= w.shape[2] and m % tm == 0
    return pl.pallas_call(
        functools.partial(_norm_proj_kernel, tn=tn),
        grid=(m // tm, n // tn),
        in_specs=[
            pl.BlockSpec((tm, d), lambda i, j: (i, 0)),
            pl.BlockSpec((1, d), lambda i, j: (0, 0)),
            pl.BlockSpec((None, d, tn), lambda i, j: (layer, 0, j)),
            pl.BlockSpec((d, LANES), lambda i, j: (0, 0)),
        ],
        out_specs=[
            pl.BlockSpec((tn // LANES, tm, LANES), lambda i, j: (j, i, 0)),
            pl.BlockSpec((tm, LANES), lambda i, j: (i, 0)),
        ],
        out_shape=[
            jax.ShapeDtypeStruct((n // LANES, m, LANES), F32),
            jax.ShapeDtypeStruct((m, LANES), F32),
        ],
        scratch_shapes=[pltpu.VMEM((tm, d), BF16)],
        compiler_params=_cparams(("parallel", "arbitrary")),
        name="norm_proj",
    )(x, nw, w, wg)


def _gates_kernel(ba_ref, alog_ref, dtb_ref, tri_ref, o_ref, *, c, valid, period):
    x = ba_ref[...]
    beta = jax.nn.sigmoid(x)
    y = x + dtb_ref[...]
    softplus = jnp.maximum(y, 0.0) + jnp.log1p(jnp.exp(-jnp.abs(y)))
    g = -jnp.exp(alog_ref[...]) * softplus
    if valid != period:
        assert period & (period - 1) == 0 and x.shape[0] % period == 0
        row = lax.broadcasted_iota(jnp.int32, x.shape, 0) & (period - 1)
        beta = jnp.where(row < valid, beta, 0.0)
        g = jnp.where(row < valid, g, 0.0)
    lane = lax.broadcasted_iota(jnp.int32, (c, x.shape[1]), 1)
    is_beta = (lane & (GATE_SLOTS - 1)) == 0
    tri = tri_ref[...]
    for n in range(x.shape[0] // c):
        sl = slice(n * c, (n + 1) * c)
        gc = jnp.dot(tri, g[sl], preferred_element_type=F32, precision=lax.Precision.HIGHEST)
        o_ref[sl, :] = jnp.where(is_beta, beta[sl], gc)


def _gates(ba, alog, dtb, *, c, tt, valid, period):
    m = ba.shape[0]
    tri = jnp.tril(jnp.ones((c, c), F32))
    return pl.pallas_call(
        functools.partial(_gates_kernel, c=c, valid=valid, period=period),
        grid=(m // tt,),
        in_specs=[
            pl.BlockSpec((tt, LANES), lambda i: (i, 0)),
            pl.BlockSpec((1, LANES), lambda i: (0, 0)),
            pl.BlockSpec((1, LANES), lambda i: (0, 0)),
            pl.BlockSpec((c, c), lambda i: (0, 0)),
        ],
        out_specs=pl.BlockSpec((tt, LANES), lambda i: (i, 0)),
        out_shape=jax.ShapeDtypeStruct((m, LANES), F32),
        compiler_params=_cparams(("parallel",)),
        name="gates",
    )(ba, alog, dtb, tri)


def _inv_unit_lower(l_mats):
    c, w = l_mats[0].shape
    npack = w // c
    r = lax.broadcasted_iota(jnp.int32, (c, w), 0)
    lane = lax.broadcasted_iota(jnp.int32, (c, w), 1)
    s = lane & (c - 1)

    def same_block(size):
        shift = size.bit_length() - 1
        return lax.shift_right_logical(r, shift) == lax.shift_right_logical(s, shift)

    same = same_block(INV_BLOCK)
    eye = jnp.where(r == s, 1.0, 0.0).astype(F32)
    zero = jnp.zeros((c, w), BF16)
    part = [lax.shift_right_logical(lane, c.bit_length() - 1) == p for p in range(npack)]

    def mm(xs, ys):
        ys = [y.astype(BF16) for y in ys]
        if npack > 1:
            ys = [jnp.concatenate([jnp.where(part[p], y, zero) for p in range(npack)], axis=0) for y in ys]
        return [jnp.dot(x.astype(BF16), y, preferred_element_type=F32) for x, y in zip(xs, ys)]

    def stack(xs, ys):
        return [jnp.concatenate([x, y], axis=0) for x, y in zip(xs, ys)]

    assert INV_BLOCK == 8
    d1 = [jnp.where(same, l, 0.0) for l in l_mats]
    d2 = mm(d1, d1)
    t = mm(stack(d2, d1), d2)
    y = [eye - a + b - x[c:] for a, b, x in zip(d1, d2, t)]
    x = [a + b for a, b in zip(y, mm(y, [v[:c] for v in t]))]
    size = INV_BLOCK
    while size < c:
        pair, inner = same_block(2 * size), same_block(size)
        e = [jnp.where(pair & ~inner, l, 0.0) for l in l_mats]
        x = [a - b for a, b in zip(x, mm(mm(x, e), x))]
        size *= 2
    return x


def _gdn_kernel(q_ref, k_ref, v_ref, z_ref, hq_ref, hk_ref, hv_ref, cq_ref, ck_ref, cv_ref, g_ref, e_ref, s0_ref,
                nw_ref, *rest, tt, c, heads, unroll):
    o_ref, s_ref, xc_s, hist_s, gl_s, kd_s, wq_s, u_s, a_s, st_s = rest[-10:]
    n = tt // c
    assert 2 * c == LANES

    @pl.when(pl.program_id(2) == 0)
    def _():
        for g in range(heads):
            hist_s[g, 0] = hq_ref[g, 0]
            hist_s[g, 1] = hk_ref[g, 0]
            for h in range(2):
                hist_s[g, 2 + h] = hv_ref[2 * g + h, 0]
                st_s[g, :, h * DV:(h + 1) * DV] = s0_ref[0, 2 * g + h]

    def l2n(x):
        return x * lax.rsqrt(jnp.sum(x * x, axis=-1, keepdims=True) + EPS)

    zero_k = jnp.zeros((c, LANES), BF16)
    base = HIST_ROWS - (CONV_W - 1)
    ghl = []
    for nn in range(n):
        graw = g_ref[nn * c:(nn + 1) * c, :]
        hi = graw.astype(BF16)
        ghl.append(jnp.concatenate([hi, (graw - hi.astype(F32)).astype(BF16)], axis=1))

    for g in range(heads):
        xin = (q_ref[g], k_ref[g], v_ref[2 * g], v_ref[2 * g + 1])
        for slot in range(4):
            xc_s[g, slot, 0:HIST_ROWS, :] = hist_s[g, slot]
            xc_s[g, slot, HIST_ROWS:HIST_ROWS + tt, :] = xin[slot]
            hist_s[g, slot] = xc_s[g, slot, tt:tt + HIST_ROWS, :]

    ex_all = [jnp.dot(ghl[nn], e_ref[g], preferred_element_type=F32) for nn in range(n) for g in range(heads)]

    def prep(job):
        nn, g = divmod(job, heads)
        r0 = nn * c
        cws =(cq_ref[g], ck_ref[g], cv_ref[2 * g], cv_ref[2 * g + 1])

        def conv_silu(slot):
            acc = xc_s[g, slot, r0 + base:r0 + base + c, :] * cws[slot][0:1, :]
            for j in range(1, CONV_W):
                acc = acc + xc_s[g, slot, r0 + base + j:r0 + base + j + c, :] * cws[slot][j:j + 1, :]
            return _silu(acc)

        ex = ex_all[job]
        gcol, bcol = ex[:, 0:LANES], ex[:, LANES:2 * LANES]
        gc = [ex[:, (2 + h) * LANES:(3 + h) * LANES] for h in range(2)]
        eg = [jnp.exp(x) for x in gc]
        gl = [x[c - 1:c, :] for x in gc]
        for h in range(2):
            gl_s[job, :, h * LANES:(h + 1) * LANES] = jnp.exp(gl[h])

        q = l2n(conv_silu(0)) * (DK ** -0.5)
        k = l2n(conv_silu(1))
        k16 = k.astype(BF16)
        v16 = [conv_silu(2 + h).astype(BF16) for h in range(2)]
        for h in range(2):
            wq_s[job, c:2 * c, h * DK:(h + 1) * DK] = (q * eg[h]).astype(BF16)
            kd_s[job, h * c:(h + 1) * c, :] = (k * jnp.exp(gl[h] - gc[h])).astype(BF16)
        return dict(gcol=gcol, bcol=bcol, begcol=bcol * jnp.exp(gcol),
                    qk=jnp.concatenate([q.astype(BF16), k16], axis=0), kk=jnp.concatenate([k16, k16], axis=0),
                    kbd=jnp.concatenate([jnp.concatenate([k16, zero_k], axis=1),
                                         jnp.concatenate([zero_k, k16], axis=1)], axis=0),
                    vbd=jnp.concatenate([jnp.concatenate([v16[0], zero_k], axis=1),
                                         jnp.concatenate([zero_k, v16[1]], axis=1)], axis=0))

    r = lax.broadcasted_iota(jnp.int32, (c, LANES), 0)
    sloc = lax.broadcasted_iota(jnp.int32, (c, LANES), 1) & (c - 1)
    causal = r >= sloc
    strict = r > sloc
    diag = r == sloc

    def as_row(x):
        return jnp.sum(jnp.where(diag, x, 0.0), axis=0, keepdims=True)

    def intra(js, ops):
        gcol = [p["gcol"] for p in ops]
        bcl = [p["bcol"] for p in ops]
        begcol = [p["begcol"] for p in ops]
        decay = [jnp.where(causal, jnp.exp(jnp.where(causal, x - as_row(x), 0.0)), 0.0) for x in gcol]
        qkk = [lax.dot_general(p["qk"], p["kk"], (((1,), (1,)), ((), ())), preferred_element_type=F32)
               for p in ops]
        for j, x, d in zip(js, qkk, decay):
            a_s[j] = (x[:c] * d).astype(BF16)
        tm = _inv_unit_lower([jnp.where(strict, x[c:] * d * b, 0.0) for x, d, b in zip(qkk, decay, bcl)])
        w = [jnp.dot((x * as_row(gg)).astype(BF16), p["kbd"], preferred_element_type=F32)
             for p, x, gg in zip(ops, tm, begcol)]
        u = [jnp.dot((x * as_row(b)).astype(BF16), p["vbd"], preferred_element_type=F32)
             for p, x, b in zip(ops, tm, bcl)]
        for j, wn, un in zip(js, w, u):
            u_s[j] = un
            wq_s[j, 0:c, :] = wn.astype(BF16)

    for i in range(heads * n // unroll):
        js = list(range(i * unroll, (i + 1) * unroll))
        intra(js, [prep(job) for job in js])

    zero_c = jnp.zeros((c, DV), BF16)
    zero_s = jnp.zeros((DK, DV), BF16)

    def blockdiag(x, zero):
        return jnp.concatenate([jnp.concatenate([x[:, :DV], zero], axis=1),
                                jnp.concatenate([zero, x[:, DV:]], axis=1)], axis=0)

    def scan(nn):
        js = [nn * heads + g for g in range(heads)]
        sp = [st_s[g] for g in range(heads)]
        r1 = [jnp.dot(wq_s[j], blockdiag(x.astype(BF16), zero_s), preferred_element_type=F32)
              for j, x in zip(js, sp)]
        vnbd = [blockdiag((u_s[j] - x[0:c]).astype(BF16), zero_c) for j, x in zip(js, r1)]
        av = [jnp.dot(a_s[j], v, preferred_element_type=F32) for j, v in zip(js, vnbd)]
        upd = [lax.dot_general(kd_s[j], v, (((0,), (0,)), ((), ())), preferred_element_type=F32)
               for j, v in zip(js, vnbd)]
        rows = slice(nn * c, (nn + 1) * c)
        for g, j in enumerate(js):
            st_s[g] = sp[g] * gl_s[j] + upd[g]
            for h in range(2):
                hv = 2 * g + h
                o = r1[g][c:2 * c, h * DV:(h + 1) * DV] + av[g][:, h * DV:(h + 1) * DV]
                o = (o * lax.rsqrt(jnp.mean(o * o, axis=-1, keepdims=True) + EPS) * nw_ref[...]
                     * _silu(z_ref[hv, rows, :]))
                o_ref[rows, hv * DV:(hv + 1) * DV] = o.astype(o_ref.dtype)

    for nn in range(n):
        scan(nn)

    for g in range(heads):
        for h in range(2):
            s_ref[0, 2 * g + h] = st_s[g, :, h * DV:(h + 1) * DV]


def _gate_selectors(nk):
    kh = jnp.arange(nk, dtype=jnp.int32)[:, None, None]
    row = jnp.arange(2 * LANES, dtype=jnp.int32)[None, :, None] & (LANES - 1)
    col = jnp.arange(4 * LANES, dtype=jnp.int32)[None, None, :]
    blk, second = col // LANES, (col % LANES) >= LANES // 2
    head = jnp.where(blk < 2, second.astype(jnp.int32), blk - 2)
    slot = jnp.where(blk == 1, 0, 1)
    return (row == GATE_SLOTS * (2 * kh + head) + slot).astype(BF16)


def _gdn_core(ph, hist, cw, gates, s0, nw, s_prev, *, layer, b, t, nk, nv, tt, heads, unroll):
    c = CHUNK
    assert t % tt == 0 and tt % c == 0 and nk % heads == 0 and nv == 2 * nk and DK == LANES and DV == LANES
    assert (heads * tt // c) % unroll == 0
    m = b * t
    nt = t // tt
    jobs = heads * tt // c
    vh = 2 * heads
    off_k, off_v, off_z = nk // heads, 2 * nk // vh, (2 * nk + nv) // vh
    seq = lambda off: (lambda bi, h, ti: (off + h, bi * nt + ti, 0))
    hst = lambda off: (lambda bi, h, ti: (off + h, bi, 0, 0))
    cwt = lambda off: (lambda bi, h, ti: (off + h, 0, 0))
    f32 = lambda *shape: pltpu.VMEM(shape, F32)
    bf16 = lambda *shape: pltpu.VMEM(shape, BF16)
    state = pl.BlockSpec((None, 1, vh, DK, DV), lambda bi, h, ti: (layer, bi, h, 0, 0))
    carried = () if s_prev is None else (s_prev,)
    inputs = (ph, ph, ph, ph, hist, hist, hist, cw, cw, cw, gates, _gate_selectors(nk), s0, nw) + carried
    return pl.pallas_call(
        functools.partial(_gdn_kernel, tt=tt, c=c, heads=heads, unroll=unroll),
        grid=(b, nk // heads, nt),
        in_specs=[
            pl.BlockSpec((heads, tt, LANES), seq(0)), pl.BlockSpec((heads, tt, LANES), seq(off_k)),
            pl.BlockSpec((vh, tt, LANES), seq(off_v)), pl.BlockSpec((vh, tt, LANES), seq(off_z)),
            pl.BlockSpec((heads, 1, HIST_ROWS, LANES), hst(0)), pl.BlockSpec((heads, 1, HIST_ROWS, LANES), hst(off_k)),
            pl.BlockSpec((vh, 1, HIST_ROWS, LANES), hst(off_v)),
            pl.BlockSpec((heads, HIST_ROWS, LANES), cwt(0)), pl.BlockSpec((heads, HIST_ROWS, LANES), cwt(off_k)),
            pl.BlockSpec((vh, HIST_ROWS, LANES), cwt(off_v)),
            pl.BlockSpec((tt, LANES), lambda bi, h, ti: (bi * nt + ti, 0)),
            pl.BlockSpec((heads, 2 * LANES, 4 * LANES), lambda bi, h, ti: (h, 0, 0)),
            state,
            pl.BlockSpec((1, DV), lambda bi, h, ti: (0, 0)),
        ] + [pl.BlockSpec(memory_space=pl.ANY) for _ in carried],
        out_specs=[pl.BlockSpec((tt, vh * DV), lambda bi, h, ti: (bi * nt + ti, h)), state],
        out_shape=[
            jax.ShapeDtypeStruct((m, nv * DV), BF16),
            jax.ShapeDtypeStruct(s0.shape, F32),
        ],
        input_output_aliases={len(inputs) - 1: 1} if carried else {},
        scratch_shapes=[
            f32(heads, 4, tt + HIST_ROWS, LANES), f32(heads, 4, HIST_ROWS, LANES), f32(jobs, 1, 2 * LANES),
            bf16(jobs, 2 * c, LANES), bf16(jobs, 2 * c, 2 * LANES),
            f32(jobs, c, 2 * DV), bf16(jobs, c, LANES), f32(heads, DK, 2 * DV),
        ],
        compiler_params=_cparams(("parallel", "parallel", "arbitrary")),
        name="gdn_core",
    )(*inputs)


def _mm_res_kernel(a_ref, w_ref, r_ref, o_ref):
    o_ref[...] = r_ref[...] + jnp.dot(a_ref[...], w_ref[...], preferred_element_type=F32)


def _mm_residual(a, w, res, *, layer, tm, tn):
    m, k = a.shape
    n = w.shape[2]
    return pl.pallas_call(
        _mm_res_kernel,
        grid=(m // tm, n // tn),
        in_specs=[
            pl.BlockSpec((tm, k), lambda i, j: (i, 0)),
            pl.BlockSpec((None, k, tn), lambda i, j: (layer, 0, j)),
            pl.BlockSpec((tm, tn), lambda i, j: (i, j)),
        ],
        out_specs=pl.BlockSpec((tm, tn), lambda i, j: (i, j)),
        out_shape=jax.ShapeDtypeStruct((m, n), F32),
        compiler_params=_cparams(("parallel", "arbitrary")),
        name="mm_residual",
    )(a, w, res)


def _ffn_kernel(x_ref, nw_ref, wg_ref, wu_ref, wd_ref, fw_ref, o_ref, h_ref, *, final):
    j = pl.program_id(1)

    @pl.when(j == 0)
    def _():
        x = x_ref[...]
        h_ref[...] = _rms(x, nw_ref[...]).astype(BF16)
        o_ref[...] = x

    h = h_ref[...]
    g = jnp.dot(h, wg_ref[...], preferred_element_type=F32)
    u = jnp.dot(h, wu_ref[...], preferred_element_type=F32)
    act = (_silu(g) * u).astype(BF16)
    o_ref[...] += jnp.dot(act, wd_ref[...], preferred_element_type=F32)

    if final:
        @pl.when(j == pl.num_programs(1) - 1)
        def _():
            o_ref[...] = _rms(o_ref[...], fw_ref[...])


def _ffn(x, nw, wgu, wd, fw, *, layer, tm, tf, final):
    m, d = x.shape
    f = wd.shape[1]
    nf = f // tf
    return pl.pallas_call(
        functools.partial(_ffn_kernel, final=final),
        grid=(m // tm, nf),
        in_specs=[
            pl.BlockSpec((tm, d), lambda i, j: (i, 0)),
            pl.BlockSpec((1, d), lambda i, j: (0, 0)),
            pl.BlockSpec((None, d, tf), lambda i, j: (layer, 0, j)),
            pl.BlockSpec((None, d, tf), lambda i, j: (layer, 0, nf + j)),
            pl.BlockSpec((None, tf, d), lambda i, j: (layer, j, 0)),
            pl.BlockSpec((1, d), lambda i, j: (0, 0)),
        ],
        out_specs=pl.BlockSpec((tm, d), lambda i, j: (i, 0)),
        out_shape=jax.ShapeDtypeStruct((m, d), F32),
        scratch_shapes=[pltpu.VMEM((tm, d), BF16)],
        compiler_params=_cparams(("parallel", "arbitrary")),
        name="ffn",
    )(x, nw, wgu, wgu, wd, fw)


def _pool_kernel(x_ref, nw_ref, hist_ref, pw_ref, sc_ref, o_ref, hs_ref, hbuf, *lvl, tt, past):
    ti = pl.program_id(1)
    pad = 2 * POOL_PAD
    assert all(w == 2 << g for g, w in enumerate(POOL_WINDOWS)) and POOL_WINDOWS[-1] // 2 <= 8

    @pl.when(ti == 0)
    def _():
        hbuf[0:POOL_PAD, :] = jnp.zeros((POOL_PAD, hbuf.shape[1]), F32)
        hbuf[POOL_PAD:pad, :] = hist_ref[0]

    @pl.when(ti > 0)
    def _():
        hbuf[0:pad, :] = hbuf[tt:tt + pad, :]

    x = x_ref[...]
    h = _rms(x, nw_ref[...])
    hbuf[pad:pad + tt, :] = h
    tg = ti * tt + lax.broadcasted_iota(jnp.int32, (tt, 1), 0)
    d = x.shape[1]
    gcw = d // len(POOL_WINDOWS)
    prev, prev_col0 = hbuf, 0
    for gi, wlen in enumerate(POOL_WINDOWS):
        start, shift, col0 = 8 * (gi + 1), wlen // 2, gi * gcw
        rel = slice(col0 - prev_col0, d - prev_col0)
        cur = prev[start:pad + tt, rel] + prev[start - shift:pad + tt - shift, rel]
        if gi + 1 < len(POOL_WINDOWS):
            lvl[gi][start:pad + tt, :] = cur
            prev, prev_col0 = lvl[gi], col0
        cols = slice(col0, col0 + gcw)
        cnt = jnp.minimum(wlen, past + tg + 1).astype(F32)
        dlt = cur[pad - start:, 0:gcw] / cnt - h[:, cols]
        y = jnp.dot(dlt.astype(BF16), pw_ref[gi], preferred_element_type=F32)
        o_ref[:, cols] = x[:, cols] + y * sc_ref[:, cols]
    hs_ref[0] = hbuf[tt + POOL_PAD:tt + pad, :]


def _pool_mixer(x, nw, hist, pw, sc, *, b, t, tt, past):
    m, d = x.shape
    nt = t // tt
    g, gcw, _ = pw.shape
    return pl.pallas_call(
        functools.partial(_pool_kernel, tt=tt, past=past),
        grid=(b, nt),
        in_specs=[
            pl.BlockSpec((tt, d), lambda bi, ti: (bi * nt + ti, 0)),
            pl.BlockSpec((1, d), lambda bi, ti: (0, 0)),
            pl.BlockSpec((1, POOL_PAD, d), lambda bi, ti: (bi, 0, 0)),
            pl.BlockSpec((g, gcw, gcw), lambda bi, ti: (0, 0, 0)),
            pl.BlockSpec((1, d), lambda bi, ti: (0, 0)),
        ],
        out_specs=[
            pl.BlockSpec((tt, d), lambda bi, ti: (bi * nt + ti, 0)),
            pl.BlockSpec((1, POOL_PAD, d), lambda bi, ti: (bi, 0, 0)),
        ],
        out_shape=[
            jax.ShapeDtypeStruct((m, d), F32),
            jax.ShapeDtypeStruct((b, POOL_PAD, d), F32),
        ],
        scratch_shapes=[pltpu.VMEM((tt + 2 * POOL_PAD, d - gi * gcw), F32) for gi in (0, *range(g - 1))],
        compiler_params=_cparams(("parallel", "arbitrary")),
        name="pool_mixer",
    )(x, nw, hist, pw, sc)


def _prep_weights(norm_mix_w, norm_ffn_w, final_norm_w, gdn_w_in, gdn_conv_w, gdn_A_log, gdn_dt_bias, gdn_norm_w,
                  gdn_w_out, pool_w, pool_scale, ffn_w_gu, ffn_w_down):
    n_a, d, _ = gdn_w_in.shape
    nv = gdn_A_log.shape[1]
    val_dim = nv * DV
    qkv_dim = gdn_conv_w.shape[2]
    main = qkv_dim + val_dim
    zeros = jnp.zeros((n_a, d, nv), F32)
    w_gate = jnp.stack([gdn_w_in[:, :, main:main + nv], gdn_w_in[:, :, main + nv:main + 2 * nv], zeros, zeros],
                       axis=-1).reshape(n_a, d, nv * GATE_SLOTS).astype(BF16)
    zv = jnp.zeros((n_a, nv), F32)
    alog = jnp.stack([zv, gdn_A_log.astype(F32), zv, zv], axis=-1).reshape(n_a, 1, nv * GATE_SLOTS)
    dtb = jnp.stack([zv, gdn_dt_bias.astype(F32), zv, zv], axis=-1).reshape(n_a, 1, nv * GATE_SLOTS)
    nh = qkv_dim // LANES
    cw = gdn_conv_w.reshape(n_a, CONV_W, nh, LANES).transpose(0, 2, 1, 3)
    cw = jnp.pad(cw, ((0, 0), (0, 0), (0, HIST_ROWS - CONV_W), (0, 0)))
    return dict(
        norm_mix=norm_mix_w[:, None, :], norm_ffn=norm_ffn_w[:, None, :], final=final_norm_w[None, :],
        w_in=gdn_w_in.astype(BF16), n_main=main, w_gate=w_gate, alog=alog, dtb=dtb, cw=cw,
        gnorm=gdn_norm_w[:, None, :], w_out=gdn_w_out.astype(BF16), pool_w=pool_w.astype(BF16),
        pool_scale=pool_scale[:, None, :], w_gu=ffn_w_gu.astype(BF16), w_down=ffn_w_down.astype(BF16),
    )


def _gdn_layer(x, conv_hist, s_hist, s_prev, wts, i, j, *, b, t):
    m, d = x.shape
    nv = s_hist.shape[2]
    nk = nv // 2
    qkv_dim = conv_hist.shape[-1]
    nh = qkv_dim // LANES
    if t % GDN_TILE == 0:
        tp, tt, heads = t, GDN_TILE, GDN_HEADS
        xp = x
    else:
        tp = -(-t // CHUNK) * CHUNK
        tt, heads = tp, nk
        xp = jnp.pad(x.reshape(b, t, d), ((0, 0), (0, tp - t), (0, 0))).reshape(b * tp, d)
    mp = b * tp
    ph, ba = _norm_proj(xp, wts["norm_mix"][i], wts["w_in"], wts["w_gate"][j], layer=j, n=wts["n_main"],
                        tm=min(ROW_TILE, mp), tn=PROJ_COLS)
    gates = _gates(ba, wts["alog"][j], wts["dtb"][j], c=CHUNK, tt=min(mp, GATE_ROWS), valid=t, period=tp)
    hist = conv_hist.reshape(b, CONV_W - 1, nh, LANES).transpose(2, 0, 1, 3)
    hist = jnp.pad(hist, ((0, 0), (0, 0), (HIST_ROWS - (CONV_W - 1), 0), (0, 0)))
    o, s_new = _gdn_core(ph, hist, wts["cw"][j], gates, s_hist.astype(F32), wts["gnorm"][j], s_prev, layer=j, b=b,
                         t=tp, nk=nk, nv=nv, tt=tt, heads=heads, unroll=min(GDN_UNROLL, heads * tt // CHUNK))
    if tp != t:
        o = o.reshape(b, tp, o.shape[-1])[:, :t].reshape(m, o.shape[-1])
    x = _mm_residual(o, wts["w_out"], x, layer=j, tm=min(ROW_TILE, m), tn=OUT_COLS)
    tail = ph.reshape(ph.shape[0], b, tp, LANES)[:nh, :, t - (CONV_W - 1):t, :]
    return x, tail.transpose(1, 2, 0, 3).reshape(b, CONV_W - 1, qkv_dim), s_new


def _trunk(x3, conv_hist, s_hist, pool_hist, wts, depth):
    b, t, d = x3.shape
    m = b * t
    x = x3.reshape(m, d)
    past = pool_hist.shape[2]
    assert t >= CONV_W - 1 and t >= POOL_PAD - 1
    new_conv, new_s, new_pool = [], None, []
    for i in range(depth):
        j = i // 2
        if i % 2 == 0:
            x, conv_new, new_s = _gdn_layer(x, conv_hist[j], s_hist, new_s, wts, i, j, b=b, t=t)
            new_conv.append(conv_new)
        else:
            hist = jnp.pad(pool_hist[j], ((0, 0), (POOL_PAD - past, 0), (0, 0)))
            x, hs = _pool_mixer(x, wts["norm_mix"][i], hist, wts["pool_w"][j], wts["pool_scale"][j],
                                b=b, t=t, tt=min(t, POOL_ROWS), past=past)
            new_pool.append(hs[:, 1:, :])
        x = _ffn(x, wts["norm_ffn"][i], wts["w_gu"], wts["w_down"], wts["final"], layer=i, tm=min(ROW_TILE, m),
                 tf=FFN_COLS, final=(i == depth - 1))
    return x.reshape(b, t, d), jnp.stack(new_conv), new_s, jnp.stack(new_pool)


def kernel(x_prompt, x_sample, state_gdn_conv, state_gdn_S, state_pool, norm_mix_w, norm_ffn_w, final_norm_w,
           gdn_w_in, gdn_conv_w, gdn_A_log, gdn_dt_bias, gdn_norm_w, gdn_w_out, pool_w, pool_scale, ffn_w_gu,
           ffn_w_down):
    depth = norm_mix_w.shape[0]
    n_a, n_b = state_gdn_conv.shape[0], state_pool.shape[0]
    bp = x_prompt.shape[0]
    wts = _prep_weights(norm_mix_w, norm_ffn_w, final_norm_w, gdn_w_in, gdn_conv_w, gdn_A_log, gdn_dt_bias,
                        gdn_norm_w, gdn_w_out, pool_w, pool_scale, ffn_w_gu, ffn_w_down)
    conv0 = jnp.zeros((n_a, bp) + state_gdn_conv.shape[2:], x_prompt.dtype)
    s00 = jnp.zeros((n_a, bp) + state_gdn_S.shape[2:], F32)
    pool0 = jnp.zeros((n_b, bp, 0, x_prompt.shape[-1]), x_prompt.dtype)
    y_p, conv_p, s_p, pool_p = _trunk(x_prompt, conv0, s00, pool0, wts, depth)
    y_s, conv_s, s_s, pool_s = _trunk(x_sample, state_gdn_conv, state_gdn_S, state_pool, wts, depth)
    return (y_p, y_s, conv_p, s_p.astype(state_gdn_S.dtype), pool_p, conv_s, s_s.astype(state_gdn_S.dtype), pool_s)
```
